```python
import jax, jax.numpy as jnp
from jax import lax
import numpy as np

D_MODEL = 2048
BATCH = 4
SEQ = 2048
DEPTH = 2

N_EVEN = (DEPTH + 1) // 2
N_ODD = DEPTH // 2
EPS = 1e-6

A_HEADS = 8
A_DK = 128
A_DV = 128
A_QK = A_HEADS * A_DK
A_WIDTH = A_HEADS * A_DV
A_CHUNK = 64
A_SUB = 16

B_GROUPS = 8
B_GROUP_DIM = 128
B_WIDTH = B_GROUPS * B_GROUP_DIM
B_CHUNK = 128

EVEN_IN = 2 * A_QK + 2 * A_WIDTH + 3 * B_WIDTH

C_HEADS = 16
C_Q_RANK = 512
C_KV_RANK = 512
C_NOPE = 128
C_ROPE = 64
C_QK = C_NOPE + C_ROPE
C_V = 128
C_WIDTH = C_HEADS * C_V
ROPE_THETA = 10000.0
Q_BLOCK = 128
ODD_IN = C_Q_RANK + C_KV_RANK + C_ROPE + C_WIDTH

MAX_POS_OFFSET = 4096

kernel_name = 'hybrid_hgrn2_gmlp_mla_sandwich'


def rms_norm(x, w):
    xf = x.astype(jnp.float32)
    y = xf * lax.rsqrt(jnp.mean(xf * xf, axis=-1, keepdims=True) + EPS)
    return (y * w.astype(jnp.float32)).astype(x.dtype)


def layer_norm(x, w, b):
    xf = x.astype(jnp.float32)
    mu = jnp.mean(xf, axis=-1, keepdims=True)
    xc = xf - mu
    y = xc * lax.rsqrt(jnp.mean(xc * xc, axis=-1, keepdims=True) + EPS)
    return (y * w.astype(jnp.float32) + b.astype(jnp.float32)).astype(x.dtype)


def split_sizes(z, sizes):
    offs, acc = [], 0
    for s in sizes[:-1]:
        acc += s
        offs.append(acc)
    return jnp.split(z, offs, axis=-1)


def hgrn2_chunked(q, k, v, log_f):
    bsz, T, H, K = q.shape
    V = v.shape[-1]
    C, c = A_CHUNK, A_SUB
    N, M = T // C, C // c

    def to_chunks(a):
        return a.reshape(bsz, N, C, H, a.shape[-1]).transpose(0, 3, 1, 2, 4)

    q, k, v, log_f = (to_chunks(a) for a in (q, k, v, log_f))
    b = jnp.cumsum(log_f, axis=3)
    b_last = b[:, :, :, -1, :]

    chunk_kv = jnp.einsum('bhnck,bhncv->bhnkv', k * jnp.exp(b_last[:, :, :, None, :] - b), v)

    def carry_state(state, inputs):
        decay, kv = inputs
        return decay[..., None] * state + kv, state

    _, s_prev = lax.scan(carry_state, jnp.zeros((bsz, H, K, V), q.dtype),
                         (jnp.moveaxis(jnp.exp(b_last), 2, 0), jnp.moveaxis(chunk_kv, 2, 0)))
    s_prev = jnp.moveaxis(s_prev, 0, 2)
    o_inter = jnp.einsum('bhnck,bhnkv->bhncv', q * jnp.exp(b), s_prev)

    g = (b - log_f)[:, :, :, ::c, :]
    q_sub = q.reshape(bsz, H, N, M, c, K) * jnp.exp(b.reshape(bsz, H, N, M, c, K) - g[:, :, :, :, None, :])
    sub_idx = jnp.arange(C) // c
    reach = sub_idx[None, :] <= jnp.arange(M)[:, None]
    expo = g[:, :, :, :, None, :] - b[:, :, :, None, :, :]
    k_sub = k[:, :, :, None] * jnp.exp(jnp.where(reach[:, :, None], expo, -jnp.inf))
    scores = jnp.einsum('bhnitk,bhnisk->bhnits', q_sub, k_sub).reshape(bsz, H, N, C, C)
    scores = jnp.where(jnp.tril(jnp.ones((C, C), bool)), scores, 0.0)
    o_intra = jnp.einsum('bhnts,bhnsv->bhntv', scores, v)
    o = o_inter + o_intra
    return o.transpose(0, 2, 3, 1, 4).reshape(bsz, T, H, V)


def even_mixer(h, w_in, lb, a_onorm, b_ln_w, b_ln_b, b_ws, b_bias, w_out):
    bsz, T, _ = h.shape
    f32 = jnp.float32
    z = h @ w_in
    qa, fa, ia, ga, ub, vb, gb = split_sizes(
        z, [A_QK, A_QK, A_WIDTH, A_WIDTH, B_WIDTH, B_WIDTH, B_WIDTH])

    zf = fa.astype(f32).reshape(bsz, T, A_HEADS, A_DK)
    lb = lb.reshape(A_HEADS, A_DK)
    log_f = jnp.log(lb + (1.0 - lb) * jax.nn.sigmoid(zf))
    k = (1.0 - lb) * jax.nn.sigmoid(-zf)
    q = qa.astype(f32).reshape(bsz, T, A_HEADS, A_DK)
    v = ia.astype(f32).reshape(bsz, T, A_HEADS, A_DV)
    o = hgrn2_chunked(q, k, v, log_f)
    o = rms_norm(o, a_onorm).reshape(bsz, T, A_WIDTH).astype(h.dtype)
    out_a = o * jax.nn.silu(ga)

    vg = layer_norm(vb.reshape(bsz, T, B_GROUPS, B_GROUP_DIM),
                    b_ln_w.reshape(B_GROUPS, B_GROUP_DIM), b_ln_b.reshape(B_GROUPS, B_GROUP_DIM))
    nc = T // B_CHUNK
    vg = vg.reshape(bsz, nc, B_CHUNK, B_GROUPS, B_GROUP_DIM)
    ws_causal = jnp.where(jnp.tril(jnp.ones((B_CHUNK, B_CHUNK), bool)), b_ws, 0.0)
    sv = jnp.einsum('gts,bnsgd->bntgd', ws_causal, vg) + b_bias.T[:, :, None]
    sv = sv.reshape(bsz, T, B_WIDTH)
    out_b = ub * sv * jax.nn.silu(gb)

    return jnp.concatenate([out_a, out_b], axis=-1) @ w_out


def rope_tables(positions):
    inv_freq = ROPE_THETA ** (-jnp.arange(0, C_ROPE, 2, dtype=jnp.float32) / C_ROPE)
    ang = positions.astype(jnp.float32)[..., None] * inv_freq
    return jnp.cos(ang)[:, :, None, :], jnp.sin(ang)[:, :, None, :]


def apply_rope(x, cos, sin):
    x1, x2 = jnp.split(x, 2, axis=-1)
    cos = cos.astype(x.dtype)
    sin = sin.astype(x.dtype)
    return jnp.concatenate([x1 * cos - x2 * sin, x2 * cos + x1 * sin], axis=-1)


def causal_block_attention(q, k, v):
    bsz, T, H, dqk = q.shape
    nb = T // Q_BLOCK
    scale = dqk ** -0.5
    qb = jnp.moveaxis(q.reshape(bsz, nb, Q_BLOCK, H, dqk), 1, 0)
    kf = k.astype(jnp.float32)
    k_idx = jnp.arange(T)
    neg = jnp.finfo(jnp.float32).min

    def one_block(args):
        q_blk, blk = args
        s = jnp.einsum('bqhd,bkhd->bhqk', q_blk.astype(jnp.float32), kf) * scale
        q_idx = blk * Q_BLOCK + jnp.arange(Q_BLOCK)
        s = jnp.where(k_idx[None, :] <= q_idx[:, None], s, neg)
        p = jax.nn.softmax(s, axis=-1)
        return jnp.einsum('bhqk,bkhv->bqhv', p.astype(v.dtype), v)

    out = lax.map(one_block, (qb, jnp.arange(nb)))
    return jnp.moveaxis(out, 0, 1).reshape(bsz, T, H, v.shape[-1])


def odd_mixer(h, cos, sin, w_in, q_norm, w_qb, kv_norm, w_kvb, w_out):
    bsz, T, _ = h.shape
    z = h @ w_in
    cq, ckv, kpe, gate = split_sizes(z, [C_Q_RANK, C_KV_RANK, C_ROPE, C_WIDTH])
    q = (rms_norm(cq, q_norm) @ w_qb).reshape(bsz, T, C_HEADS, C_QK)
    q_nope, q_pe = q[..., :C_NOPE], q[..., C_NOPE:]
    kv = (rms_norm(ckv, kv_norm) @ w_kvb).reshape(bsz, T, C_HEADS, C_NOPE + C_V)
    k_nope, v = kv[..., :C_NOPE], kv[..., C_NOPE:]
    q_pe = apply_rope(q_pe, cos, sin)
    k_pe = apply_rope(kpe[:, :, None, :], cos, sin)
    q = jnp.concatenate([q_nope, q_pe], axis=-1)
    k = jnp.concatenate([k_nope, jnp.broadcast_to(k_pe, (bsz, T, C_HEADS, C_ROPE))], axis=-1)
    o = causal_block_attention(q, k, v).reshape(bsz, T, C_WIDTH)
    return (o * jax.nn.silu(gate)) @ w_out


def setup_inputs(seed: int = 0) -> dict:
    key = jax.random.key(seed)
    ks = jax.random.split(key, 18)
    f32 = jnp.float32

    def normal(k, shape, scale):
        return scale * jax.random.normal(k, shape, f32)

    x = normal(ks[0], (BATCH, SEQ, D_MODEL), 1.0)
    positions = (jax.random.randint(ks[1], (BATCH, 1), 0, MAX_POS_OFFSET)
                 + jnp.arange(SEQ)[None, :]).astype(jnp.int32)
    norm_pre = 1.0 + normal(ks[2], (DEPTH, D_MODEL), 0.05)
    norm_post = 1.0 + normal(ks[3], (DEPTH, D_MODEL), 0.05)
    ev_w_in = normal(ks[4], (N_EVEN, D_MODEL, EVEN_IN), D_MODEL ** -0.5)
    ev_lb_logits = normal(ks[5], (N_EVEN + 1, A_QK), 0.1)
    ev_a_onorm = 1.0 + normal(ks[6], (N_EVEN, A_DV), 0.05)
    ev_b_ln_w = 1.0 + normal(ks[7], (N_EVEN, B_WIDTH), 0.05)
    ev_b_ln_b = normal(ks[8], (N_EVEN, B_WIDTH), 0.02)
    ev_b_ws = normal(ks[9], (N_EVEN, B_GROUPS, B_CHUNK, B_CHUNK), B_CHUNK ** -0.5)
    ev_b_bias = 1.0 + normal(ks[10], (N_EVEN, B_GROUPS, B_CHUNK), 0.02)
    ev_w_out = normal(ks[11], (N_EVEN, A_WIDTH + B_WIDTH, D_MODEL), (A_WIDTH + B_WIDTH) ** -0.5)
    od_w_in = normal(ks[12], (N_ODD, D_MODEL, ODD_IN), D_MODEL ** -0.5)
    od_q_norm = 1.0 + normal(ks[13], (N_ODD, C_Q_RANK), 0.05)
    od_w_qb = normal(ks[14], (N_ODD, C_Q_RANK, C_HEADS * C_QK), C_Q_RANK ** -0.5)
    od_kv_norm = 1.0 + normal(ks[15], (N_ODD, C_KV_RANK), 0.05)
    od_w_kvb = normal(ks[16], (N_ODD, C_KV_RANK, C_HEADS * (C_NOPE + C_V)), C_KV_RANK ** -0.5)
    od_w_out = normal(ks[17], (N_ODD, C_WIDTH, D_MODEL), C_WIDTH ** -0.5)
    return {'x': x, 'positions': positions, 'norm_pre': norm_pre, 'norm_post': norm_post,
            'ev_w_in': ev_w_in, 'ev_lb_logits': ev_lb_logits, 'ev_a_onorm': ev_a_onorm,
            'ev_b_ln_w': ev_b_ln_w, 'ev_b_ln_b': ev_b_ln_b, 'ev_b_ws': ev_b_ws,
            'ev_b_bias': ev_b_bias, 'ev_w_out': ev_w_out, 'od_w_in': od_w_in,
            'od_q_norm': od_q_norm, 'od_w_qb': od_w_qb, 'od_kv_norm': od_kv_norm,
            'od_w_kvb': od_w_kvb, 'od_w_out': od_w_out}


def reference(x, positions, norm_pre, norm_post, ev_w_in, ev_lb_logits, ev_a_onorm,
              ev_b_ln_w, ev_b_ln_b, ev_b_ws, ev_b_bias, ev_w_out, od_w_in, od_q_norm,
              od_w_qb, od_kv_norm, od_w_kvb, od_w_out):
    lower_bounds = jnp.cumsum(jax.nn.softmax(ev_lb_logits.astype(jnp.float32), axis=0), axis=0)[:N_EVEN]
    cos, sin = rope_tables(positions)
    for layer in range(DEPTH):
        j = layer // 2
        h = rms_norm(x, norm_pre[layer])
        if layer % 2 == 0:
            y = even_mixer(h, ev_w_in[j], lower_bounds[j], ev_a_onorm[j], ev_b_ln_w[j],
                           ev_b_ln_b[j], ev_b_ws[j], ev_b_bias[j], ev_w_out[j])
        else:
            y = odd_mixer(h, cos, sin, od_w_in[j], od_q_norm[j], od_w_qb[j], od_kv_norm[j],
                          od_w_kvb[j], od_w_out[j])
        x = x + rms_norm(y, norm_post[layer])
    return x
```

```python
import functools

import jax
import jax.numpy as jnp
from jax import lax
from jax.experimental import pallas as pl
from jax.experimental.pallas import tpu as pltpu

F32 = jnp.float32
BF16 = jnp.bfloat16

EPS = 1e-6
LANES = 128
VMEM_LIMIT_BYTES = 56 * 1024 * 1024

A_HEADS = 8
A_DK = 128
A_CHUNK = 64
A_SUB = 16
B_GROUPS = 8
B_DIM = 128
B_CHUNK = 128
C_HEADS = 16
C_RANK = 512
C_NOPE = 128
C_ROPE = 64
C_V = 128
ROPE_THETA = 10000.0

NT_DIMS = (((1,), (1,)), ((), ()))
TN_DIMS = (((0,), (0,)), ((), ()))


def _params(*sem):
    return pltpu.CompilerParams(dimension_semantics=sem, vmem_limit_bytes=VMEM_LIMIT_BYTES)


def _rms(x, w):
    ms = jnp.mean(x * x, axis=-1, keepdims=True)
    return x * lax.rsqrt(ms + EPS) * w


def _silu(x):
    return x * (1.0 / (1.0 + jnp.exp(-x)))


def _rms_matmul_kernel(x_ref, g_ref, w_ref, o_ref, h_ref):
    @pl.when(pl.program_id(1) == 0)
    def _():
        h_ref[...] = _rms(x_ref[...], g_ref[...]).astype(BF16)

    o_ref[...] = jnp.dot(h_ref[...], w_ref[...], preferred_element_type=F32).astype(o_ref.dtype)


def rms_matmul(x, g, w, *, tm, tn):
    m, k = x.shape
    n = w.shape[1]
    return pl.pallas_call(
        _rms_matmul_kernel,
        grid=(m // tm, n // tn),
        in_specs=[
            pl.BlockSpec((tm, k), lambda i, j: (i, 0)),
            pl.BlockSpec((1, k), lambda i, j: (0, 0)),
            pl.BlockSpec((k, tn), lambda i, j: (0, j)),
        ],
        out_specs=pl.BlockSpec((tm, tn), lambda i, j: (i, j)),
        out_shape=jax.ShapeDtypeStruct((m, n), BF16),
        scratch_shapes=[pltpu.VMEM((tm, k), BF16)],
        compiler_params=_params("parallel", "arbitrary"),
        name="rms_matmul",
    )(x, g, w)


def _cumsum_mats():
    c, s = A_CHUNK, A_SUB
    r = lax.broadcasted_iota(jnp.int32, (3 * c, c), 0)
    col = lax.broadcasted_iota(jnp.int32, (3 * c, c), 1)
    t = r % c
    kind = r // c
    limit = jnp.where(kind == 0, t + 1, jnp.where(kind == 1, (t // s) * s, c))
    return jnp.where(col < limit, 1.0, 0.0).astype(F32)


def _hgrn2_chunk(q, zf, v, lb, st, mats, tril_mask):
    c, s = A_CHUNK, A_SUB
    a = jnp.abs(zf)
    t = jnp.exp(-a)
    r = 1.0 / (1.0 + t)
    tr = t * r
    pos = zf >= 0
    sig = jnp.where(pos, r, tr)
    sig_neg = jnp.where(pos, tr, r)
    logf = jnp.log(lb + (1.0 - lb) * sig)
    kk = (1.0 - lb) * sig_neg

    sums = jnp.dot(mats, logf, preferred_element_type=F32, precision=lax.Precision.HIGHEST)
    b = sums[0:c]
    g = sums[c:2 * c]
    b_last = sums[2 * c:3 * c]

    q_in = (q * jnp.exp(b)).astype(BF16)
    k_in = (kk * jnp.exp(b_last - b)).astype(BF16)
    vb = v.astype(BF16)
    o = lax.dot_general(q_in, st.astype(BF16), NT_DIMS, preferred_element_type=F32)

    q_sub = (q * jnp.exp(b - g)).astype(BF16)
    rows = lax.broadcasted_iota(jnp.int32, (c, 1), 0)
    score_rows = []
    for i in range(c // s):
        g_i = g[i * s:i * s + 1]
        expo = jnp.where(rows < (i + 1) * s, g_i - b, -jnp.inf)
        k_sub = (kk * jnp.exp(expo)).astype(BF16)
        score_rows.append(lax.dot_general(q_sub[i * s:(i + 1) * s], k_sub, NT_DIMS,
                                          preferred_element_type=F32))
    scores = jnp.concatenate(score_rows, axis=0)
    scores = jnp.where(tril_mask, scores, 0.0).astype(BF16)
    o = o + jnp.dot(scores, vb, preferred_element_type=F32)

    decay = jnp.exp(b_last[0:1])
    st_new = st * decay + lax.dot_general(vb, k_in, TN_DIMS, preferred_element_type=F32)
    return o, st_new


def _even_mixer_kernel(q_ref, f_ref, i_ref, ga_ref, u_ref, v_ref, gb_ref,
                       lbl_ref, onorm_ref, lnw_ref, lnb_ref, ws_ref, bias_ref,
                       oa_ref, ob_ref, st_ref, *, layer_j):
    tb = q_ref.shape[0]

    @pl.when(pl.program_id(2) == 0)
    def _():
        st_ref[...] = jnp.zeros_like(st_ref)

    logits = lbl_ref[...]
    e = jnp.exp(logits - jnp.max(logits, axis=0, keepdims=True))
    lb = jnp.sum(e[:layer_j + 1], axis=0, keepdims=True) / jnp.sum(e, axis=0, keepdims=True)

    mats = _cumsum_mats()
    c = A_CHUNK
    tril_mask = (lax.broadcasted_iota(jnp.int32, (c, c), 1)
                 <= lax.broadcasted_iota(jnp.int32, (c, c), 0))
    onorm = onorm_ref[...]

    def chunk_body(n, carry):
        rows = pl.ds(pl.multiple_of(n * c, c), c)
        o, st_new = _hgrn2_chunk(q_ref[rows, :].astype(F32), f_ref[rows, :].astype(F32),
                                 i_ref[rows, :].astype(F32), lb, st_ref[...], mats, tril_mask)
        st_ref[...] = st_new
        gate = ga_ref[rows, :].astype(F32)
        oa_ref[rows, :] = (_rms(o, onorm) * _silu(gate)).astype(oa_ref.dtype)
        return carry

    lax.fori_loop(0, tb // c, chunk_body, 0)

    cb = B_CHUNK
    w_causal = jnp.where(lax.broadcasted_iota(jnp.int32, (cb, cb), 1)
                         <= lax.broadcasted_iota(jnp.int32, (cb, cb), 0),
                         ws_ref[...], 0.0).astype(BF16)
    bias = bias_ref[...]
    lnw = lnw_ref[...]
    lnb = lnb_ref[...]
    for n in range(tb // cb):
        rows = slice(n * cb, (n + 1) * cb)
        vv = v_ref[rows, :].astype(F32)
        mu = jnp.mean(vv, axis=-1, keepdims=True)
        xc = vv - mu
        vg = xc * lax.rsqrt(jnp.mean(xc * xc, axis=-1, keepdims=True) + EPS) * lnw + lnb
        sv = jnp.dot(w_causal, vg.astype(BF16), preferred_element_type=F32) + bias
        gate = gb_ref[rows, :].astype(F32)
        ob_ref[rows, :] = (u_ref[rows, :].astype(F32) * sv * _silu(gate)).astype(ob_ref.dtype)


def even_mixers(z, lb_logits, a_onorm, ln_w, ln_b, ws, bias, *, layer_j, tb):
    bsz, t, _ = z.shape
    h = A_HEADS
    n_rows = lb_logits.shape[0]
    lbl = lb_logits.reshape(n_rows, h, A_DK).transpose(1, 0, 2)

    def sec(k):
        return pl.BlockSpec((None, tb, LANES), lambda b, j, s, k=k: (b, s, k * h + j))

    per_head = lambda shape: pl.BlockSpec((None,) + shape, lambda b, j, s: (j, 0, 0))
    out_spec = pl.BlockSpec((None, tb, LANES), lambda b, j, s: (b, s, j))
    out_sds = jax.ShapeDtypeStruct((bsz, t, h * LANES), BF16)
    return pl.pallas_call(
        functools.partial(_even_mixer_kernel, layer_j=layer_j),
        grid=(bsz, h, t // tb),
        in_specs=[sec(k) for k in range(7)] + [
            per_head((n_rows, LANES)),
            pl.BlockSpec((1, LANES), lambda b, j, s: (0, 0)),
            per_head((1, LANES)),
            per_head((1, LANES)),
            per_head((B_CHUNK, B_CHUNK)),
            per_head((B_CHUNK, 1)),
        ],
        out_specs=[out_spec, out_spec],
        out_shape=[out_sds, out_sds],
        scratch_shapes=[pltpu.VMEM((A_DK, A_DK), F32)],
        compiler_params=_params("parallel", "parallel", "arbitrary"),
        name="even_mixers",
    )(z, z, z, z, z, z, z, lbl, a_onorm.reshape(1, LANES),
      ln_w.reshape(B_GROUPS, 1, B_DIM), ln_b.reshape(B_GROUPS, 1, B_DIM),
      ws, bias.reshape(B_GROUPS, B_CHUNK, 1))


def _out_proj_kernel(*refs, n_in, with_next):
    a_refs = refs[:n_in]
    w_refs = refs[n_in:2 * n_in]
    x_ref, post_ref = refs[2 * n_in:2 * n_in + 2]
    rest = refs[2 * n_in + 2:]
    y = jnp.dot(a_refs[0][...], w_refs[0][...], preferred_element_type=F32)
    for a_ref, w_ref in zip(a_refs[1:], w_refs[1:]):
        y = y + jnp.dot(a_ref[...], w_ref[...], preferred_element_type=F32)
    x_new = x_ref[...] + _rms(y, post_ref[...])
    if with_next:
        pre_ref, xo_ref, ho_ref = rest
        xo_ref[...] = x_new
        ho_ref[...] = _rms(x_new, pre_ref[...]).astype(BF16)
    else:
        (xo_ref,) = rest
        xo_ref[...] = x_new


def out_proj(acts, weights, x, post_w, next_pre_w, *, tm):
    m, d = x.shape
    n_in = len(acts)
    with_next = next_pre_w is not None
    row = lambda width: pl.BlockSpec((tm, width), lambda i: (i, 0))
    const = lambda shape: pl.BlockSpec(shape, lambda i: (0, 0))
    in_specs = ([row(a.shape[1]) for a in acts] + [const(w.shape) for w in weights]
                + [row(d), const((1, d))])
    args = list(acts) + list(weights) + [x, post_w.reshape(1, d)]
    out_specs = [row(d)]
    out_shape = [jax.ShapeDtypeStruct((m, d), F32)]
    if with_next:
        in_specs.append(const((1, d)))
        args.append(next_pre_w.reshape(1, d))
        out_specs.append(row(d))
        out_shape.append(jax.ShapeDtypeStruct((m, d), BF16))
    return pl.pallas_call(
        functools.partial(_out_proj_kernel, n_in=n_in, with_next=with_next),
        grid=(m // tm,),
        in_specs=in_specs,
        out_specs=out_specs,
        out_shape=out_shape,
        compiler_params=_params("parallel"),
        name="out_proj",
    )(*args)


def _matmul_split_kernel(h_ref, w_ref, *o_refs):
    y = jnp.dot(h_ref[...], w_ref[...], preferred_element_type=F32)
    off = 0
    for o_ref in o_refs:
        width = o_ref.shape[1]
        o_ref[...] = y[:, off:off + width].astype(o_ref.dtype)
        off += width


def matmul_split(h, w, widths, *, tm):
    m, k = h.shape
    return pl.pallas_call(
        _matmul_split_kernel,
        grid=(m // tm,),
        in_specs=[pl.BlockSpec((tm, k), lambda i: (i, 0)),
                  pl.BlockSpec(w.shape, lambda i: (0, 0))],
        out_specs=[pl.BlockSpec((tm, wd), lambda i: (i, 0)) for wd in widths],
        out_shape=[jax.ShapeDtypeStruct((m, wd), BF16) for wd in widths],
        compiler_params=_params("parallel"),
        name="odd_in_proj",
    )(h, w)


def _mla_expand_kernel(c_ref, pe_ref, pos_ref, freq_ref, qnorm_ref, kvnorm_ref, wq_ref, wkv_ref,
                       qn_ref, qpe_ref, kn_ref, v_ref, kpe_ref, *, scale):
    c = c_ref[...].astype(F32)
    cq = _rms(c[:, :C_RANK], qnorm_ref[...]).astype(BF16)
    ckv = _rms(c[:, C_RANK:], kvnorm_ref[...]).astype(BF16)
    ang = pos_ref[...].astype(F32) * freq_ref[...]
    cos = jnp.cos(ang)
    sin = jnp.sin(ang)

    n_nope = C_HEADS * C_NOPE
    n_pe = C_HEADS * C_ROPE
    q = jnp.dot(cq, wq_ref[...], preferred_element_type=F32)
    qn_ref[...] = (q[:, :n_nope] * scale).astype(qn_ref.dtype)
    rep = n_pe // LANES
    q_pe = (q[:, n_nope:n_nope + n_pe] * jnp.tile(cos, (1, rep))
            + q[:, n_nope + n_pe:] * jnp.tile(sin, (1, rep)))
    qpe_ref[...] = (q_pe * scale).astype(qpe_ref.dtype)

    kv = jnp.dot(ckv, wkv_ref[...], preferred_element_type=F32)
    kn_ref[...] = kv[:, :n_nope].astype(kn_ref.dtype)
    v_ref[...] = kv[:, n_nope:].astype(v_ref.dtype)

    pe = pe_ref[...].astype(F32)
    kpe_ref[...] = (pe[:, :LANES] * cos + pe[:, LANES:] * sin).astype(kpe_ref.dtype)


def mla_expand(c, pe, pos, freq, q_norm, kv_norm, wq, wkv, *, tm, scale):
    m = c.shape[0]
    row = lambda width: pl.BlockSpec((tm, width), lambda i: (i, 0))
    const = lambda shape: pl.BlockSpec(shape, lambda i: (0, 0))
    n_nope = C_HEADS * C_NOPE
    widths = [n_nope, C_HEADS * C_ROPE, n_nope, C_HEADS * C_V, LANES]
    return pl.pallas_call(
        functools.partial(_mla_expand_kernel, scale=scale),
        grid=(m // tm,),
        in_specs=[row(c.shape[1]), row(pe.shape[1]), row(1), const((1, LANES)),
                  const((1, C_RANK)), const((1, C_RANK)), const(wq.shape), const(wkv.shape)],
        out_specs=[row(wd) for wd in widths],
        out_shape=[jax.ShapeDtypeStruct((m, wd), BF16) for wd in widths],
        compiler_params=_params("parallel"),
        name="mla_expand",
    )(c, pe, pos, freq, q_norm.reshape(1, C_RANK), kv_norm.reshape(1, C_RANK), wq, wkv)


def _attn_kernel(qn_ref, qpe_ref, kn_ref, kpe_ref, v_ref, g_ref, o_ref, k_scr, *, tq):
    t = qn_ref.shape[0]
    head = pl.program_id(1)
    k_scr[:, :LANES] = kn_ref[...]
    k_scr[:, LANES:] = kpe_ref[...]

    lane = lax.broadcasted_iota(jnp.int32, (1, LANES), 1)
    keep = (lane // C_ROPE) == (head % 2)
    causal = (lax.broadcasted_iota(jnp.int32, (tq, tq), 1)
              <= lax.broadcasted_iota(jnp.int32, (tq, tq), 0))
    neg = jnp.finfo(F32).min

    for qi in range(t // tq):
        r0 = qi * tq
        rows = slice(r0, r0 + tq)
        q_pe = qpe_ref[rows, :]
        q = jnp.concatenate([qn_ref[rows, :], jnp.where(keep, q_pe, jnp.zeros_like(q_pe))], axis=1)
        s_d = lax.dot_general(q, k_scr[rows, :], NT_DIMS, preferred_element_type=F32)
        s_d = jnp.where(causal, s_d, neg)
        m = jnp.max(s_d, axis=-1, keepdims=True)
        if qi > 0:
            s_o = lax.dot_general(q, k_scr[:r0, :], NT_DIMS, preferred_element_type=F32)
            m = jnp.maximum(m, jnp.max(s_o, axis=-1, keepdims=True))
        p_d = jnp.exp(s_d - m)
        l = jnp.sum(p_d, axis=-1, keepdims=True)
        acc = jnp.dot(p_d.astype(BF16), v_ref[rows, :], preferred_element_type=F32)
        if qi > 0:
            p_o = jnp.exp(s_o - m)
            l = l + jnp.sum(p_o, axis=-1, keepdims=True)
            acc = acc + jnp.dot(p_o.astype(BF16), v_ref[:r0, :], preferred_element_type=F32)
        o = acc * (1.0 / l)
        o_ref[rows, :] = (o * _silu(g_ref[rows, :].astype(F32))).astype(o_ref.dtype)


def attention(qn, qpe, kn, kpe, v, gate, *, tq):
    bsz, t, _ = qn.shape
    head = pl.BlockSpec((None, t, LANES), lambda b, h: (b, 0, h))
    pair = pl.BlockSpec((None, t, LANES), lambda b, h: (b, 0, h // 2))
    shared = pl.BlockSpec((None, t, LANES), lambda b, h: (b, 0, 0))
    return pl.pallas_call(
        functools.partial(_attn_kernel, tq=tq),
        grid=(bsz, C_HEADS),
        in_specs=[head, pair, head, shared, head, head],
        out_specs=head,
        out_shape=jax.ShapeDtypeStruct((bsz, t, C_HEADS * C_V), BF16),
        scratch_shapes=[pltpu.VMEM((t, 2 * LANES), BF16)],
        compiler_params=_params("parallel", "parallel"),
        name="mla_attention",
    )(qn, qpe, kn, kpe, v, gate)


def _rot_cols(w):
    lead = w.shape[:-1]
    wr = w.reshape(lead + (-1, 2, C_ROPE // 2))
    return jnp.concatenate([-wr[..., 1:2, :], wr[..., 0:1, :]], axis=-2).reshape(w.shape)


def _odd_weights(w_in, w_qb, w_kvb):
    w_c = w_in[:, :2 * C_RANK]
    w_pe = w_in[:, 2 * C_RANK:2 * C_RANK + C_ROPE]
    w_gate = w_in[:, 2 * C_RANK + C_ROPE:]
    w_pe_rot = _rot_cols(w_pe)
    w_in_all = jnp.concatenate([w_c, w_gate, w_pe, w_pe, w_pe_rot, w_pe_rot], axis=1).astype(BF16)

    qb = w_qb.reshape(C_RANK, C_HEADS, C_NOPE + C_ROPE)
    q_nope = qb[:, :, :C_NOPE].reshape(C_RANK, -1)
    q_pe = qb[:, :, C_NOPE:].reshape(C_RANK, -1)
    wq_all = jnp.concatenate([q_nope, q_pe, _rot_cols(q_pe)], axis=1).astype(BF16)

    kvb = w_kvb.reshape(C_RANK, C_HEADS, C_NOPE + C_V)
    wkv_all = jnp.concatenate([kvb[:, :, :C_NOPE].reshape(C_RANK, -1),
                               kvb[:, :, C_NOPE:].reshape(C_RANK, -1)], axis=1).astype(BF16)
    return w_in_all, wq_all, wkv_all


def kernel(x, positions, norm_pre, norm_post, ev_w_in, ev_lb_logits, ev_a_onorm, ev_b_ln_w,
           ev_b_ln_b, ev_b_ws, ev_b_bias, ev_w_out, od_w_in, od_q_norm, od_w_qb, od_kv_norm,
           od_w_kvb, od_w_out):
    bsz, t, d = x.shape
    m = bsz * t
    xf = x.reshape(m, d)
    tm_big = min(1024, m)
    tm = min(512, m)

    z = rms_matmul(xf, norm_pre[0].reshape(1, d), ev_w_in[0].astype(BF16), tm=tm_big, tn=1024)
    mix_a, mix_b = even_mixers(z.reshape(bsz, t, -1), ev_lb_logits, ev_a_onorm[0], ev_b_ln_w[0],
                               ev_b_ln_b[0], ev_b_ws[0], ev_b_bias[0], layer_j=0, tb=min(512, t))
    a_width = A_HEADS * A_DK
    w_out0 = ev_w_out[0].astype(BF16)
    x1, h1 = out_proj([mix_a.reshape(m, -1), mix_b.reshape(m, -1)],
                      [w_out0[:a_width], w_out0[a_width:]], xf, norm_post[0], norm_pre[1], tm=tm)

    w_in_all, wq_all, wkv_all = _odd_weights(od_w_in[0], od_w_qb[0], od_w_kvb[0])
    c, gate, pe = matmul_split(h1, w_in_all, [2 * C_RANK, C_HEADS * C_V, 2 * LANES], tm=tm)
    inv_freq = ROPE_THETA ** (-jnp.arange(0, C_ROPE, 2, dtype=F32) / C_ROPE)
    freq = jnp.tile(inv_freq, LANES // (C_ROPE // 2)).reshape(1, LANES)
    scale = float((C_NOPE + C_ROPE) ** -0.5)
    qn, qpe, kn, v, kpe = mla_expand(c, pe, positions.reshape(m, 1), freq, od_q_norm[0],
                                     od_kv_norm[0], wq_all, wkv_all, tm=tm, scale=scale)
    r3 = lambda a: a.reshape(bsz, t, -1)
    o = attention(r3(qn), r3(qpe), r3(kn), r3(kpe), r3(v), r3(gate), tq=min(256, t))
    (x2,) = out_proj([o.reshape(m, -1)], [od_w_out[0].astype(BF16)], x1, norm_post[1], None, tm=tm)
    return x2.reshape(bsz, t, d)
```

```python
import functools

import jax
import jax.numpy as jnp
from jax import lax
from jax.experimental import pallas as pl
from jax.experimental.pallas import tpu as pltpu

F32 = jnp.float32
BF16 = jnp.bfloat16

EPS = 1e-6
LANES = 128
VMEM_LIMIT_BYTES = 56 * 1024 * 1024

A_HEADS = 8
A_DK = 128
A_CHUNK = 64
A_SUB = 16
B_GROUPS = 8
B_DIM = 128
B_CHUNK = 128
C_HEADS = 16
C_RANK = 512
C_NOPE = 128
C_ROPE = 64
C_V = 128
ROPE_THETA = 10000.0

NT_DIMS = (((1,), (1,)), ((), ()))
TN_DIMS = (((0,), (0,)), ((), ()))


def _params(*sem):
    return pltpu.CompilerParams(dimension_semantics=sem, vmem_limit_bytes=VMEM_LIMIT_BYTES)


def _rms(x, w):
    ms = jnp.mean(x * x, axis=-1, keepdims=True)
    return x * lax.rsqrt(ms + EPS) * w


def _silu(x):
    return x * (1.0 / (1.0 + jnp.exp(-x)))


def _rms_matmul_kernel(x_ref, g_ref, w_ref, o_ref, h_ref):
    @pl.when(pl.program_id(1) == 0)
    def _():
        h_ref[...] = _rms(x_ref[...], g_ref[...]).astype(BF16)

    o_ref[...] = jnp.dot(h_ref[...], w_ref[...], preferred_element_type=F32).astype(o_ref.dtype)


def rms_matmul(x, g, w, *, tm, tn):
    m, k = x.shape
    n = w.shape[1]
    return pl.pallas_call(
        _rms_matmul_kernel,
        grid=(m // tm, n // tn),
        in_specs=[
            pl.BlockSpec((tm, k), lambda i, j: (i, 0)),
            pl.BlockSpec((1, k), lambda i, j: (0, 0)),
            pl.BlockSpec((k, tn), lambda i, j: (0, j)),
        ],
        out_specs=pl.BlockSpec((tm, tn), lambda i, j: (i, j)),
        out_shape=jax.ShapeDtypeStruct((m, n), BF16),
        scratch_shapes=[pltpu.VMEM((tm, k), BF16)],
        compiler_params=_params("parallel", "arbitrary"),
        name="rms_matmul",
    )(x, g, w)


def _hgrn2_chunk(q, zf, v, lb, tril_f32, tril_mask):
    c, s = A_CHUNK, A_SUB
    m = c // s
    a = jnp.abs(zf)
    t = jnp.exp(-a)
    r = 1.0 / (1.0 + t)
    tr = t * r
    pos = zf >= 0
    sig = jnp.where(pos, r, tr)
    sig_neg = jnp.where(pos, tr, r)
    logf = jnp.log(lb + (1.0 - lb) * sig)
    kk = (1.0 - lb) * sig_neg

    b = jnp.dot(tril_f32, logf, preferred_element_type=F32, precision=lax.Precision.HIGHEST)
    b3 = b.reshape(m, s, A_DK)
    g3 = (b3 - logf.reshape(m, s, A_DK))[:, 0:1, :]
    e3 = b3[:, s - 1:s, :]
    b_last = e3[m - 1]
    lc3 = b3 - g3
    kk3 = kk.reshape(m, s, A_DK)
    q_sub = q.reshape(m, s, A_DK) * jnp.exp(lc3)
    k_diag = kk3 * jnp.exp(-lc3)
    k_end = kk3 * jnp.exp(e3 - b3)

    q_in = (q_sub * jnp.exp(g3)).reshape(c, A_DK).astype(BF16)
    k_in = (k_end * jnp.exp(b_last - e3)).reshape(c, A_DK).astype(BF16)
    vb = v.astype(BF16)
    kv = lax.dot_general(vb, k_in, TN_DIMS, preferred_element_type=F32)

    score_rows = []
    for i in range(m):
        parts = []
        for j in range(i):
            if j == i - 1:
                parts.append(k_end[j])
            else:
                parts.append(k_end[j] * jnp.exp(g3[i] - e3[j]))
        parts.append(k_diag[i])
        if i < m - 1:
            parts.append(jnp.zeros(((m - 1 - i) * s, A_DK), F32))
        k_i = jnp.concatenate(parts, axis=0).astype(BF16)
        score_rows.append(lax.dot_general(q_sub[i].astype(BF16), k_i, NT_DIMS,
                                          preferred_element_type=F32))
    scores = jnp.concatenate(score_rows, axis=0)
    scores = jnp.where(tril_mask, scores, 0.0).astype(BF16)
    o_intra = jnp.dot(scores, vb, preferred_element_type=F32)
    return o_intra, q_in, kv, jnp.exp(b_last)


def _even_mixer_kernel(q_ref, f_ref, i_ref, ga_ref, u_ref, v_ref, gb_ref,
                       lbl_ref, onorm_ref, lnw_ref, lnb_ref, ws_ref, bias_ref,
                       oa_ref, ob_ref, st_ref, *, layer_j):
    tb = q_ref.shape[0]

    @pl.when(pl.program_id(2) == 0)
    def _():
        st_ref[...] = jnp.zeros_like(st_ref)

    logits = lbl_ref[...]
    e = jnp.exp(logits - jnp.max(logits, axis=0, keepdims=True))
    lb = jnp.sum(e[:layer_j + 1], axis=0, keepdims=True) / jnp.sum(e, axis=0, keepdims=True)

    c = A_CHUNK
    tril_mask = (lax.broadcasted_iota(jnp.int32, (c, c), 1)
                 <= lax.broadcasted_iota(jnp.int32, (c, c), 0))
    tril_f32 = jnp.where(tril_mask, 1.0, 0.0).astype(F32)
    onorm = onorm_ref[...]

    st = st_ref[...]
    for n in range(tb // c):
        rows = slice(n * c, (n + 1) * c)
        o, q_in, kv, decay = _hgrn2_chunk(q_ref[rows, :].astype(F32), f_ref[rows, :].astype(F32),
                                          i_ref[rows, :].astype(F32), lb, tril_f32, tril_mask)
        o = o + lax.dot_general(q_in, st.astype(BF16), NT_DIMS, preferred_element_type=F32)
        st = st * decay + kv
        gate = ga_ref[rows, :].astype(F32)
        oa_ref[rows, :] = (_rms(o, onorm) * _silu(gate)).astype(oa_ref.dtype)
    st_ref[...] = st

    cb = B_CHUNK
    w_causal = jnp.where(lax.broadcasted_iota(jnp.int32, (cb, cb), 1)
                         <= lax.broadcasted_iota(jnp.int32, (cb, cb), 0),
                         ws_ref[...], 0.0).astype(BF16)
    bias = bias_ref[...]
    lnw = lnw_ref[...]
    lnb = lnb_ref[...]
    for n in range(tb // cb):
        rows = slice(n * cb, (n + 1) * cb)
        vv = v_ref[rows, :].astype(F32)
        mu = jnp.mean(vv, axis=-1, keepdims=True)
        xc = vv - mu
        vg = xc * lax.rsqrt(jnp.mean(xc * xc, axis=-1, keepdims=True) + EPS) * lnw + lnb
        sv = jnp.dot(w_causal, vg.astype(BF16), preferred_element_type=F32) + bias
        gate = gb_ref[rows, :].astype(F32)
        ob_ref[rows, :] = (u_ref[rows, :].astype(F32) * sv * _silu(gate)).astype(ob_ref.dtype)


def even_mixers(z, lb_logits, a_onorm, ln_w, ln_b, ws, bias, *, layer_j, tb):
    bsz, t, _ = z.shape
    h = A_HEADS
    n_rows = lb_logits.shape[0]
    lbl = lb_logits.reshape(n_rows, h, A_DK).transpose(1, 0, 2)

    def sec(k):
        return pl.BlockSpec((None, tb, LANES), lambda b, j, s, k=k: (b, s, k * h + j))

    per_head = lambda shape: pl.BlockSpec((None,) + shape, lambda b, j, s: (j, 0, 0))
    out_spec = pl.BlockSpec((None, tb, LANES), lambda b, j, s: (b, s, j))
    out_sds = jax.ShapeDtypeStruct((bsz, t, h * LANES), BF16)
    return pl.pallas_call(
        functools.partial(_even_mixer_kernel, layer_j=layer_j),
        grid=(bsz, h, t // tb),
        in_specs=[sec(k) for k in range(7)] + [
            per_head((n_rows, LANES)),
            pl.BlockSpec((1, LANES), lambda b, j, s: (0, 0)),
            per_head((1, LANES)),
            per_head((1, LANES)),
            per_head((B_CHUNK, B_CHUNK)),
            per_head((B_CHUNK, 1)),
        ],
        out_specs=[out_spec, out_spec],
        out_shape=[out_sds, out_sds],
        scratch_shapes=[pltpu.VMEM((A_DK, A_DK), F32)],
        compiler_params=_params("parallel", "parallel", "arbitrary"),
        name="even_mixers",
    )(z, z, z, z, z, z, z, lbl, a_onorm.reshape(1, LANES),
      ln_w.reshape(B_GROUPS, 1, B_DIM), ln_b.reshape(B_GROUPS, 1, B_DIM),
      ws, bias.reshape(B_GROUPS, B_CHUNK, 1))


def _out_proj_kernel(*refs, n_in, with_next):
    a_refs = refs[:n_in]
    w_refs = refs[n_in:2 * n_in]
    x_ref, post_ref = refs[2 * n_in:2 * n_in + 2]
    rest = refs[2 * n_in + 2:]
    y = jnp.dot(a_refs[0][...], w_refs[0][...], preferred_element_type=F32)
    for a_ref, w_ref in zip(a_refs[1:], w_refs[1:]):
        y = y + jnp.dot(a_ref[...], w_ref[...], preferred_element_type=F32)
    x_new = x_ref[...] + _rms(y, post_ref[...])
    if with_next:
        pre_ref, xo_ref, ho_ref = rest
        xo_ref[...] = x_new
        ho_ref[...] = _rms(x_new, pre_ref[...]).astype(BF16)
    else:
        (xo_ref,) = rest
        xo_ref[...] = x_new


def out_proj(acts, weights, x, post_w, next_pre_w, *, tm):
    m, d = x.shape
    n_in = len(acts)
    with_next = next_pre_w is not None
    row = lambda width: pl.BlockSpec((tm, width), lambda i: (i, 0))
    const = lambda shape: pl.BlockSpec(shape, lambda i: (0, 0))
    in_specs = ([row(a.shape[1]) for a in acts] + [const(w.shape) for w in weights]
                + [row(d), const((1, d))])
    args = list(acts) + list(weights) + [x, post_w.reshape(1, d)]
    out_specs = [row(d)]
    out_shape = [jax.ShapeDtypeStruct((m, d), F32)]
    if with_next:
        in_specs.append(const((1, d)))
        args.append(next_pre_w.reshape(1, d))
        out_specs.append(row(d))
        out_shape.append(jax.ShapeDtypeStruct((m, d), BF16))
    return pl.pallas_call(
        functools.partial(_out_proj_kernel, n_in=n_in, with_next=with_next),
        grid=(m // tm,),
        in_specs=in_specs,
        out_specs=out_specs,
        out_shape=out_shape,
        compiler_params=_params("parallel"),
        name="out_proj",
    )(*args)


def _matmul_split_kernel(h_ref, w_ref, *o_refs):
    y = jnp.dot(h_ref[...], w_ref[...], preferred_element_type=F32)
    off = 0
    for o_ref in o_refs:
        width = o_ref.shape[1]
        o_ref[...] = y[:, off:off + width].astype(o_ref.dtype)
        off += width


def matmul_split(h, w, widths, *, tm):
    m, k = h.shape
    return pl.pallas_call(
        _matmul_split_kernel,
        grid=(m // tm,),
        in_specs=[pl.BlockSpec((tm, k), lambda i: (i, 0)),
                  pl.BlockSpec(w.shape, lambda i: (0, 0))],
        out_specs=[pl.BlockSpec((tm, wd), lambda i: (i, 0)) for wd in widths],
        out_shape=[jax.ShapeDtypeStruct((m, wd), BF16) for wd in widths],
        compiler_params=_params("parallel"),
        name="odd_in_proj",
    )(h, w)


def _mla_expand_kernel(c_ref, pe_ref, pos_ref, freq_ref, qnorm_ref, kvnorm_ref, wq_ref, wkv_ref,
                       qn_ref, qpe_ref, kn_ref, v_ref, kpe_ref, *, scale):
    c = c_ref[...].astype(F32)
    cq = _rms(c[:, :C_RANK], qnorm_ref[...]).astype(BF16)
    ckv = _rms(c[:, C_RANK:], kvnorm_ref[...]).astype(BF16)
    ang = pos_ref[...].astype(F32) * freq_ref[...]
    cos = jnp.cos(ang)
    sin = jnp.sin(ang)

    n_nope = C_HEADS * C_NOPE
    n_pe = C_HEADS * C_ROPE
    q = jnp.dot(cq, wq_ref[...], preferred_element_type=F32)
    qn_ref[...] = (q[:, :n_nope] * scale).astype(qn_ref.dtype)
    rep = n_pe // LANES
    q_pe = (q[:, n_nope:n_nope + n_pe] * jnp.tile(cos, (1, rep))
            + q[:, n_nope + n_pe:] * jnp.tile(sin, (1, rep)))
    qpe_ref[...] = (q_pe * scale).astype(qpe_ref.dtype)

    kv = jnp.dot(ckv, wkv_ref[...], preferred_element_type=F32)
    kn_ref[...] = kv[:, :n_nope].astype(kn_ref.dtype)
    v_ref[...] = kv[:, n_nope:].astype(v_ref.dtype)

    pe = pe_ref[...].astype(F32)
    kpe_ref[...] = (pe[:, :LANES] * cos + pe[:, LANES:] * sin).astype(kpe_ref.dtype)


def mla_expand(c, pe, pos, freq, q_norm, kv_norm, wq, wkv, *, tm, scale):
    m = c.shape[0]
    row = lambda width: pl.BlockSpec((tm, width), lambda i: (i, 0))
    const = lambda shape: pl.BlockSpec(shape, lambda i: (0, 0))
    n_nope = C_HEADS * C_NOPE
    widths = [n_nope, C_HEADS * C_ROPE, n_nope, C_HEADS * C_V, LANES]
    return pl.pallas_call(
        functools.partial(_mla_expand_kernel, scale=scale),
        grid=(m // tm,),
        in_specs=[row(c.shape[1]), row(pe.shape[1]), row(1), const((1, LANES)),
                  const((1, C_RANK)), const((1, C_RANK)), const(wq.shape), const(wkv.shape)],
        out_specs=[row(wd) for wd in widths],
        out_shape=[jax.ShapeDtypeStruct((m, wd), BF16) for wd in widths],
        compiler_params=_params("parallel"),
        name="mla_expand",
    )(c, pe, pos, freq, q_norm.reshape(1, C_RANK), kv_norm.reshape(1, C_RANK), wq, wkv)


def _attn_kernel(qn_ref, qpe_ref, kn_ref, kpe_ref, v_ref, g_ref, o_ref, k_scr, *, tq):
    t = qn_ref.shape[0]
    head = pl.program_id(1)
    k_scr[:, :LANES] = kn_ref[...]
    k_scr[:, LANES:] = kpe_ref[...]

    lane = lax.broadcasted_iota(jnp.int32, (1, LANES), 1)
    keep = (lane // C_ROPE) == (head % 2)
    causal = (lax.broadcasted_iota(jnp.int32, (tq, tq), 1)
              <= lax.broadcasted_iota(jnp.int32, (tq, tq), 0))
    neg = jnp.finfo(F32).min

    for qi in range(t // tq):
        r0 = qi * tq
        rows = slice(r0, r0 + tq)
        q_pe = qpe_ref[rows, :]
        q = jnp.concatenate([qn_ref[rows, :], jnp.where(keep, q_pe, jnp.zeros_like(q_pe))], axis=1)
        s_d = lax.dot_general(q, k_scr[rows, :], NT_DIMS, preferred_element_type=F32)
        s_d = jnp.where(causal, s_d, neg)
        m = jnp.max(s_d, axis=-1, keepdims=True)
        if qi > 0:
            s_o = lax.dot_general(q, k_scr[:r0, :], NT_DIMS, preferred_element_type=F32)
            m = jnp.maximum(m, jnp.max(s_o, axis=-1, keepdims=True))
        p_d = jnp.exp(s_d - m)
        l = jnp.sum(p_d, axis=-1, keepdims=True)
        acc = jnp.dot(p_d.astype(BF16), v_ref[rows, :], preferred_element_type=F32)
        if qi > 0:
            p_o = jnp.exp(s_o - m)
            l = l + jnp.sum(p_o, axis=-1, keepdims=True)
            acc = acc + jnp.dot(p_o.astype(BF16), v_ref[:r0, :], preferred_element_type=F32)
        o = acc * (1.0 / l)
        o_ref[rows, :] = (o * _silu(g_ref[rows, :].astype(F32))).astype(o_ref.dtype)


def attention(qn, qpe, kn, kpe, v, gate, *, tq):
    bsz, t, _ = qn.shape
    head = pl.BlockSpec((None, t, LANES), lambda b, h: (b, 0, h))
    pair = pl.BlockSpec((None, t, LANES), lambda b, h: (b, 0, h // 2))
    shared = pl.BlockSpec((None, t, LANES), lambda b, h: (b, 0, 0))
    return pl.pallas_call(
        functools.partial(_attn_kernel, tq=tq),
        grid=(bsz, C_HEADS),
        in_specs=[head, pair, head, shared, head, head],
        out_specs=head,
        out_shape=jax.ShapeDtypeStruct((bsz, t, C_HEADS * C_V), BF16),
        scratch_shapes=[pltpu.VMEM((t, 2 * LANES), BF16)],
        compiler_params=_params("parallel", "parallel"),
        name="mla_attention",
    )(qn, qpe, kn, kpe, v, gate)


def _rot_cols(w):
    lead = w.shape[:-1]
    wr = w.reshape(lead + (-1, 2, C_ROPE // 2))
    return jnp.concatenate([-wr[..., 1:2, :], wr[..., 0:1, :]], axis=-2).reshape(w.shape)


def _odd_weights(w_in, w_qb, w_kvb):
    w_c = w_in[:, :2 * C_RANK]
    w_pe = w_in[:, 2 * C_RANK:2 * C_RANK + C_ROPE]
    w_gate = w_in[:, 2 * C_RANK + C_ROPE:]
    w_pe_rot = _rot_cols(w_pe)
    w_in_all = jnp.concatenate([w_c, w_gate, w_pe, w_pe, w_pe_rot, w_pe_rot], axis=1).astype(BF16)

    qb = w_qb.reshape(C_RANK, C_HEADS, C_NOPE + C_ROPE)
    q_nope = qb[:, :, :C_NOPE].reshape(C_RANK, -1)
    q_pe = qb[:, :, C_NOPE:].reshape(C_RANK, -1)
    wq_all = jnp.concatenate([q_nope, q_pe, _rot_cols(q_pe)], axis=1).astype(BF16)

    kvb = w_kvb.reshape(C_RANK, C_HEADS, C_NOPE + C_V)
    wkv_all = jnp.concatenate([kvb[:, :, :C_NOPE].reshape(C_RANK, -1),
                               kvb[:, :, C_NOPE:].reshape(C_RANK, -1)], axis=1).astype(BF16)
    return w_in_all, wq_all, wkv_all


def kernel(x, positions, norm_pre, norm_post, ev_w_in, ev_lb_logits, ev_a_onorm, ev_b_ln_w,
           ev_b_ln_b, ev_b_ws, ev_b_bias, ev_w_out, od_w_in, od_q_norm, od_w_qb, od_kv_norm,
           od_w_kvb, od_w_out):
    bsz, t, d = x.shape
    m = bsz * t
    xf = x.reshape(m, d)
    tm_big = min(1024, m)
    tm = min(512, m)

    z = rms_matmul(xf, norm_pre[0].reshape(1, d), ev_w_in[0].astype(BF16), tm=tm_big, tn=1024)
    mix_a, mix_b = even_mixers(z.reshape(bsz, t, -1), ev_lb_logits, ev_a_onorm[0], ev_b_ln_w[0],
                               ev_b_ln_b[0], ev_b_ws[0], ev_b_bias[0], layer_j=0, tb=min(512, t))
    a_width = A_HEADS * A_DK
    w_out0 = ev_w_out[0].astype(BF16)
    x1, h1 = out_proj([mix_a.reshape(m, -1), mix_b.reshape(m, -1)],
                      [w_out0[:a_width], w_out0[a_width:]], xf, norm_post[0], norm_pre[1], tm=tm)

    w_in_all, wq_all, wkv_all = _odd_weights(od_w_in[0], od_w_qb[0], od_w_kvb[0])
    c, gate, pe = matmul_split(h1, w_in_all, [2 * C_RANK, C_HEADS * C_V, 2 * LANES], tm=tm)
    inv_freq = ROPE_THETA ** (-jnp.arange(0, C_ROPE, 2, dtype=F32) / C_ROPE)
    freq = jnp.tile(inv_freq, LANES // (C_ROPE // 2)).reshape(1, LANES)
    scale = float((C_NOPE + C_ROPE) ** -0.5)
    qn, qpe, kn, v, kpe = mla_expand(c, pe, positions.reshape(m, 1), freq, od_q_norm[0],
                                     od_kv_norm[0], wq_all, wkv_all, tm=tm, scale=scale)
    r3 = lambda a: a.reshape(bsz, t, -1)
    o = attention(r3(qn), r3(qpe), r3(kn), r3(kpe), r3(v), r3(gate), tq=min(256, t))
    (x2,) = out_proj([o.reshape(m, -1)], [od_w_out[0].astype(BF16)], x1, norm_post[1], None, tm=tm)
    return x2.reshape(bsz, t, d)
```

```python
import functools

import jax
import jax.numpy as jnp
from jax import lax
from jax.experimental import pallas as pl
from jax.experimental.pallas import tpu as pltpu

F32 = jnp.float32
BF16 = jnp.bfloat16

EPS = 1e-6
LANES = 128
VMEM_LIMIT_BYTES = 56 * 1024 * 1024

A_HEADS = 8
A_DK = 128
A_CHUNK = 64
A_SUB = 16
B_GROUPS = 8
B_DIM = 128
B_CHUNK = 128
C_HEADS = 16
C_RANK = 512
C_NOPE = 128
C_ROPE = 64
C_V = 128
ROPE_THETA = 10000.0
LOG2E = 1.4426950408889634

NT_DIMS = (((1,), (1,)), ((), ()))
TN_DIMS = (((0,), (0,)), ((), ()))


def _params(*sem):
    return pltpu.CompilerParams(dimension_semantics=sem, vmem_limit_bytes=VMEM_LIMIT_BYTES)


def _rms(x, w):
    ms = jnp.mean(x * x, axis=-1, keepdims=True)
    return x * lax.rsqrt(ms + EPS) * w


def _silu(x):
    return x * (1.0 / (1.0 + jnp.exp(-x)))


def _rms_matmul_kernel(x_ref, g_ref, w_ref, o_ref, h_ref):
    @pl.when(pl.program_id(1) == 0)
    def _():
        h_ref[...] = _rms(x_ref[...], g_ref[...]).astype(BF16)

    o_ref[...] = jnp.dot(h_ref[...], w_ref[...], preferred_element_type=F32).astype(o_ref.dtype)


def rms_matmul(x, g, w, *, tm, tn):
    m, k = x.shape
    n = w.shape[1]
    return pl.pallas_call(
        _rms_matmul_kernel,
        grid=(m // tm, n // tn),
        in_specs=[
            pl.BlockSpec((tm, k), lambda i, j: (i, 0)),
            pl.BlockSpec((1, k), lambda i, j: (0, 0)),
            pl.BlockSpec((k, tn), lambda i, j: (0, j)),
        ],
        out_specs=pl.BlockSpec((tm, tn), lambda i, j: (i, j)),
        out_shape=jax.ShapeDtypeStruct((m, n), BF16),
        scratch_shapes=[pltpu.VMEM((tm, k), BF16)],
        compiler_params=_params("parallel", "arbitrary"),
        name="rms_matmul",
    )(x, g, w)


def _hgrn2_chunk(q, zf, v, lb, tril_f32, tril_mask):
    c, s = A_CHUNK, A_SUB
    m = c // s
    a = jnp.abs(zf)
    t = jnp.exp(-a)
    r = 1.0 / (1.0 + t)
    tr = t * r
    pos = zf >= 0
    sig = jnp.where(pos, r, tr)
    sig_neg = jnp.where(pos, tr, r)
    logf = jnp.log(lb + (1.0 - lb) * sig)
    kk = (1.0 - lb) * sig_neg

    b = jnp.dot(tril_f32, logf, preferred_element_type=F32, precision=lax.Precision.HIGHEST)
    b3 = b.reshape(m, s, A_DK)
    g3 = (b3 - logf.reshape(m, s, A_DK))[:, 0:1, :]
    e3 = b3[:, s - 1:s, :]
    b_last = e3[m - 1]
    lc3 = b3 - g3
    kk3 = kk.reshape(m, s, A_DK)
    q_sub = q.reshape(m, s, A_DK) * jnp.exp(lc3)
    k_diag = kk3 * jnp.exp(-lc3)
    k_end = kk3 * jnp.exp(e3 - b3)

    q_in = (q_sub * jnp.exp(g3)).reshape(c, A_DK).astype(BF16)
    k_in = (k_end * jnp.exp(b_last - e3)).reshape(c, A_DK).astype(BF16)
    vb = v.astype(BF16)
    kv = lax.dot_general(vb, k_in, TN_DIMS, preferred_element_type=F32)

    score_rows = []
    for i in range(m):
        parts = []
        for j in range(i):
            if j == i - 1:
                parts.append(k_end[j])
            else:
                parts.append(k_end[j] * jnp.exp(g3[i] - e3[j]))
        parts.append(k_diag[i])
        if i < m - 1:
            parts.append(jnp.zeros(((m - 1 - i) * s, A_DK), F32))
        k_i = jnp.concatenate(parts, axis=0).astype(BF16)
        score_rows.append(lax.dot_general(q_sub[i].astype(BF16), k_i, NT_DIMS,
                                          preferred_element_type=F32))
    scores = jnp.concatenate(score_rows, axis=0)
    scores = jnp.where(tril_mask, scores, 0.0).astype(BF16)
    o_intra = jnp.dot(scores, vb, preferred_element_type=F32)
    return o_intra, q_in, kv, jnp.exp(b_last)


def _even_mixer_kernel(q_ref, f_ref, i_ref, ga_ref, u_ref, v_ref, gb_ref,
                       lbl_ref, onorm_ref, lnw_ref, lnb_ref, ws_ref, bias_ref,
                       oa_ref, ob_ref, st_ref, *, layer_j):
    tb = q_ref.shape[0]

    @pl.when(pl.program_id(2) == 0)
    def _():
        st_ref[...] = jnp.zeros_like(st_ref)

    logits = lbl_ref[...]
    e = jnp.exp(logits - jnp.max(logits, axis=0, keepdims=True))
    lb = jnp.sum(e[:layer_j + 1], axis=0, keepdims=True) / jnp.sum(e, axis=0, keepdims=True)

    c = A_CHUNK
    tril_mask = (lax.broadcasted_iota(jnp.int32, (c, c), 1)
                 <= lax.broadcasted_iota(jnp.int32, (c, c), 0))
    tril_f32 = jnp.where(tril_mask, 1.0, 0.0).astype(F32)
    onorm = onorm_ref[...]

    st = st_ref[...]
    for n in range(tb // c):
        rows = slice(n * c, (n + 1) * c)
        o, q_in, kv, decay = _hgrn2_chunk(q_ref[rows, :].astype(F32), f_ref[rows, :].astype(F32),
                                          i_ref[rows, :].astype(F32), lb, tril_f32, tril_mask)
        o = o + lax.dot_general(q_in, st.astype(BF16), NT_DIMS, preferred_element_type=F32)
        st = st * decay + kv
        gate = ga_ref[rows, :].astype(F32)
        oa_ref[rows, :] = (_rms(o, onorm) * _silu(gate)).astype(oa_ref.dtype)
    st_ref[...] = st

    cb = B_CHUNK
    w_causal = jnp.where(lax.broadcasted_iota(jnp.int32, (cb, cb), 1)
                         <= lax.broadcasted_iota(jnp.int32, (cb, cb), 0),
                         ws_ref[...], 0.0).astype(BF16)
    bias = bias_ref[...]
    lnw = lnw_ref[...]
    lnb = lnb_ref[...]
    for n in range(tb // cb):
        rows = slice(n * cb, (n + 1) * cb)
        vv = v_ref[rows, :].astype(F32)
        mu = jnp.mean(vv, axis=-1, keepdims=True)
        xc = vv - mu
        vg = xc * lax.rsqrt(jnp.mean(xc * xc, axis=-1, keepdims=True) + EPS) * lnw + lnb
        sv = jnp.dot(w_causal, vg.astype(BF16), preferred_element_type=F32) + bias
        gate = gb_ref[rows, :].astype(F32)
        ob_ref[rows, :] = (u_ref[rows, :].astype(F32) * sv * _silu(gate)).astype(ob_ref.dtype)


def even_mixers(z, lb_logits, a_onorm, ln_w, ln_b, ws, bias, *, layer_j, tb):
    bsz, t, _ = z.shape
    h = A_HEADS
    n_rows = lb_logits.shape[0]
    lbl = lb_logits.reshape(n_rows, h, A_DK).transpose(1, 0, 2)

    def sec(k):
        return pl.BlockSpec((None, tb, LANES), lambda b, j, s, k=k: (b, s, k * h + j))

    per_head = lambda shape: pl.BlockSpec((None,) + shape, lambda b, j, s: (j, 0, 0))
    out_spec = pl.BlockSpec((None, tb, LANES), lambda b, j, s: (b, s, j))
    out_sds = jax.ShapeDtypeStruct((bsz, t, h * LANES), BF16)
    return pl.pallas_call(
        functools.partial(_even_mixer_kernel, layer_j=layer_j),
        grid=(bsz, h, t // tb),
        in_specs=[sec(k) for k in range(7)] + [
            per_head((n_rows, LANES)),
            pl.BlockSpec((1, LANES), lambda b, j, s: (0, 0)),
            per_head((1, LANES)),
            per_head((1, LANES)),
            per_head((B_CHUNK, B_CHUNK)),
            per_head((B_CHUNK, 1)),
        ],
        out_specs=[out_spec, out_spec],
        out_shape=[out_sds, out_sds],
        scratch_shapes=[pltpu.VMEM((A_DK, A_DK), F32)],
        compiler_params=_params("parallel", "parallel", "arbitrary"),
        name="even_mixers",
    )(z, z, z, z, z, z, z, lbl, a_onorm.reshape(1, LANES),
      ln_w.reshape(B_GROUPS, 1, B_DIM), ln_b.reshape(B_GROUPS, 1, B_DIM),
      ws, bias.reshape(B_GROUPS, B_CHUNK, 1))


def _out_proj_kernel(*refs, n_in, with_next):
    a_refs = refs[:n_in]
    w_refs = refs[n_in:2 * n_in]
    x_ref, post_ref = refs[2 * n_in:2 * n_in + 2]
    rest = refs[2 * n_in + 2:]
    y = jnp.dot(a_refs[0][...], w_refs[0][...], preferred_element_type=F32)
    for a_ref, w_ref in zip(a_refs[1:], w_refs[1:]):
        y = y + jnp.dot(a_ref[...], w_ref[...], preferred_element_type=F32)
    x_new = x_ref[...] + _rms(y, post_ref[...])
    if with_next:
        pre_ref, xo_ref, ho_ref = rest
        xo_ref[...] = x_new
        ho_ref[...] = _rms(x_new, pre_ref[...]).astype(BF16)
    else:
        (xo_ref,) = rest
        xo_ref[...] = x_new


def out_proj(acts, weights, x, post_w, next_pre_w, *, tm):
    m, d = x.shape
    n_in = len(acts)
    with_next = next_pre_w is not None
    row = lambda width: pl.BlockSpec((tm, width), lambda i: (i, 0))
    const = lambda shape: pl.BlockSpec(shape, lambda i: (0, 0))
    in_specs = ([row(a.shape[1]) for a in acts] + [const(w.shape) for w in weights]
                + [row(d), const((1, d))])
    args = list(acts) + list(weights) + [x, post_w.reshape(1, d)]
    out_specs = [row(d)]
    out_shape = [jax.ShapeDtypeStruct((m, d), F32)]
    if with_next:
        in_specs.append(const((1, d)))
        args.append(next_pre_w.reshape(1, d))
        out_specs.append(row(d))
        out_shape.append(jax.ShapeDtypeStruct((m, d), BF16))
    return pl.pallas_call(
        functools.partial(_out_proj_kernel, n_in=n_in, with_next=with_next),
        grid=(m // tm,),
        in_specs=in_specs,
        out_specs=out_specs,
        out_shape=out_shape,
        compiler_params=_params("parallel"),
        name="out_proj",
    )(*args)


def _matmul_split_kernel(h_ref, w_ref, *o_refs):
    y = jnp.dot(h_ref[...], w_ref[...], preferred_element_type=F32)
    off = 0
    for o_ref in o_refs:
        width = o_ref.shape[1]
        o_ref[...] = y[:, off:off + width].astype(o_ref.dtype)
        off += width


def matmul_split(h, w, widths, *, tm):
    m, k = h.shape
    return pl.pallas_call(
        _matmul_split_kernel,
        grid=(m // tm,),
        in_specs=[pl.BlockSpec((tm, k), lambda i: (i, 0)),
                  pl.BlockSpec(w.shape, lambda i: (0, 0))],
        out_specs=[pl.BlockSpec((tm, wd), lambda i: (i, 0)) for wd in widths],
        out_shape=[jax.ShapeDtypeStruct((m, wd), BF16) for wd in widths],
        compiler_params=_params("parallel"),
        name="odd_in_proj",
    )(h, w)


def _mla_expand_kernel(c_ref, pe_ref, pos_ref, freq_ref, qnorm_ref, kvnorm_ref, wq_ref, wk_ref,
                       wvt_ref, qn_ref, qpe_ref, kn_ref, vt_ref, kpe_ref, *, scale):
    c = c_ref[...].astype(F32)
    cq = _rms(c[:, :C_RANK], qnorm_ref[...]).astype(BF16)
    ckv = _rms(c[:, C_RANK:], kvnorm_ref[...]).astype(BF16)
    ang = pos_ref[...].astype(F32) * freq_ref[...]
    cos = jnp.cos(ang)
    sin = jnp.sin(ang)

    n_nope = C_HEADS * C_NOPE
    n_pe = C_HEADS * C_ROPE
    q = jnp.dot(cq, wq_ref[...], preferred_element_type=F32)
    qn_ref[...] = (q[:, :n_nope] * scale).astype(qn_ref.dtype)
    rep = n_pe // LANES
    q_pe = (q[:, n_nope:n_nope + n_pe] * jnp.tile(cos, (1, rep))
            + q[:, n_nope + n_pe:] * jnp.tile(sin, (1, rep)))
    qpe_ref[...] = (q_pe * scale).astype(qpe_ref.dtype)

    kn_ref[...] = jnp.dot(ckv, wk_ref[...], preferred_element_type=F32).astype(kn_ref.dtype)
    vt_ref[...] = lax.dot_general(wvt_ref[...], ckv, NT_DIMS,
                                  preferred_element_type=F32).astype(vt_ref.dtype)

    pe = pe_ref[...].astype(F32)
    kpe_ref[...] = (pe[:, :LANES] * cos + pe[:, LANES:] * sin).astype(kpe_ref.dtype)


def mla_expand(c, pe, pos, freq, q_norm, kv_norm, wq, wk, wvt, *, bsz, tm, scale):
    m = c.shape[0]
    t = m // bsz
    nb = t // tm
    row = lambda width: pl.BlockSpec((tm, width), lambda i: (i, 0))
    const = lambda shape: pl.BlockSpec(shape, lambda i: (0, 0))
    n_nope = C_HEADS * C_NOPE
    n_v = C_HEADS * C_V
    widths = [n_nope, C_HEADS * C_ROPE, n_nope, None, LANES]
    out_specs = [row(wd) if wd else pl.BlockSpec((None, n_v, tm), lambda i: (i // nb, 0, i % nb))
                 for wd in widths]
    out_shape = [jax.ShapeDtypeStruct((m, wd) if wd else (bsz, n_v, t), BF16) for wd in widths]
    return pl.pallas_call(
        functools.partial(_mla_expand_kernel, scale=scale),
        grid=(m // tm,),
        in_specs=[row(c.shape[1]), row(pe.shape[1]), row(1), const((1, LANES)),
                  const((1, C_RANK)), const((1, C_RANK)), const(wq.shape), const(wk.shape),
                  const(wvt.shape)],
        out_specs=out_specs,
        out_shape=out_shape,
        compiler_params=_params("parallel"),
        name="mla_expand",
    )(c, pe, pos, freq, q_norm.reshape(1, C_RANK), kv_norm.reshape(1, C_RANK), wq, wk, wvt)


def _attn_kernel(qn_ref, qpe_ref, kn_ref, kpe_ref, vt_ref, g_ref, o_ref, k_scr, *, tq):
    t = qn_ref.shape[0]
    head = pl.program_id(1)
    k_scr[:, :LANES] = kn_ref[...]
    k_scr[:, LANES:] = kpe_ref[...]

    lane = lax.broadcasted_iota(jnp.int32, (1, LANES), 1)
    keep = (lane // C_ROPE) == (head % 2)
    causal = (lax.broadcasted_iota(jnp.int32, (tq, tq), 0)
              <= lax.broadcasted_iota(jnp.int32, (tq, tq), 1))
    neg = jnp.finfo(F32).min

    def scores(qi):
        r0 = qi * tq
        end = r0 + tq
        q_pe = qpe_ref[r0:end, :]
        q = jnp.concatenate([qn_ref[r0:end, :], jnp.where(keep, q_pe, jnp.zeros_like(q_pe))], axis=1)
        return lax.dot_general(k_scr[:end, :], q, NT_DIMS, preferred_element_type=F32)

    nq = t // tq
    s_next = scores(0)
    for qi in range(nq):
        r0 = qi * tq
        end = r0 + tq
        s = s_next
        if qi + 1 < nq:
            s_next = scores(qi + 1)
        s_d = jnp.where(causal, s[r0:end], neg)
        m = jnp.max(s_d, axis=0, keepdims=True)
        if qi > 0:
            m = jnp.maximum(m, jnp.max(s[:r0], axis=0, keepdims=True))
        p_d = jnp.exp2(s_d - m)
        l = jnp.sum(p_d, axis=0, keepdims=True)
        acc = jnp.dot(vt_ref[:, r0:end], p_d.astype(BF16), preferred_element_type=F32)
        if qi > 0:
            p_o = jnp.exp2(s[:r0] - m)
            l = l + jnp.sum(p_o, axis=0, keepdims=True)
            acc = acc + jnp.dot(vt_ref[:, :r0], p_o.astype(BF16), preferred_element_type=F32)
        o = (acc * (1.0 / l)).T
        o_ref[r0:end, :] = (o * _silu(g_ref[r0:end, :].astype(F32))).astype(o_ref.dtype)


def attention(qn, qpe, kn, kpe, vt, gate, *, tq):
    bsz, t, _ = qn.shape
    head = pl.BlockSpec((None, t, LANES), lambda b, h: (b, 0, h))
    pair = pl.BlockSpec((None, t, LANES), lambda b, h: (b, 0, h // 2))
    shared = pl.BlockSpec((None, t, LANES), lambda b, h: (b, 0, 0))
    head_t = pl.BlockSpec((None, C_V, t), lambda b, h: (b, h, 0))
    return pl.pallas_call(
        functools.partial(_attn_kernel, tq=tq),
        grid=(bsz, C_HEADS),
        in_specs=[head, pair, head, shared, head_t, head],
        out_specs=head,
        out_shape=jax.ShapeDtypeStruct((bsz, t, C_HEADS * C_V), BF16),
        scratch_shapes=[pltpu.VMEM((t, 2 * LANES), BF16)],
        compiler_params=_params("parallel", "parallel"),
        name="mla_attention",
    )(qn, qpe, kn, kpe, vt, gate)


def _rot_cols(w):
    lead = w.shape[:-1]
    wr = w.reshape(lead + (-1, 2, C_ROPE // 2))
    return jnp.concatenate([-wr[..., 1:2, :], wr[..., 0:1, :]], axis=-2).reshape(w.shape)


def _odd_weights(w_in, w_qb, w_kvb):
    w_c = w_in[:, :2 * C_RANK]
    w_pe = w_in[:, 2 * C_RANK:2 * C_RANK + C_ROPE]
    w_gate = w_in[:, 2 * C_RANK + C_ROPE:]
    w_pe_rot = _rot_cols(w_pe)
    w_in_all = jnp.concatenate([w_c, w_gate, w_pe, w_pe, w_pe_rot, w_pe_rot], axis=1).astype(BF16)

    qb = w_qb.reshape(C_RANK, C_HEADS, C_NOPE + C_ROPE)
    q_nope = qb[:, :, :C_NOPE].reshape(C_RANK, -1)
    q_pe = qb[:, :, C_NOPE:].reshape(C_RANK, -1)
    wq_all = jnp.concatenate([q_nope, q_pe, _rot_cols(q_pe)], axis=1).astype(BF16)

    kvb = w_kvb.reshape(C_RANK, C_HEADS, C_NOPE + C_V)
    wk = kvb[:, :, :C_NOPE].reshape(C_RANK, -1).astype(BF16)
    wvt = kvb[:, :, C_NOPE:].reshape(C_RANK, -1).T.astype(BF16)
    return w_in_all, wq_all, wk, wvt


def kernel(x, positions, norm_pre, norm_post, ev_w_in, ev_lb_logits, ev_a_onorm, ev_b_ln_w,
           ev_b_ln_b, ev_b_ws, ev_b_bias, ev_w_out, od_w_in, od_q_norm, od_w_qb, od_kv_norm,
           od_w_kvb, od_w_out):
    bsz, t, d = x.shape
    m = bsz * t
    xf = x.reshape(m, d)
    tm_big = min(1024, m)
    tm = min(512, m)

    z = rms_matmul(xf, norm_pre[0].reshape(1, d), ev_w_in[0].astype(BF16), tm=tm_big, tn=1024)
    mix_a, mix_b = even_mixers(z.reshape(bsz, t, -1), ev_lb_logits, ev_a_onorm[0], ev_b_ln_w[0],
                               ev_b_ln_b[0], ev_b_ws[0], ev_b_bias[0], layer_j=0, tb=min(512, t))
    a_width = A_HEADS * A_DK
    w_out0 = ev_w_out[0].astype(BF16)
    x1, h1 = out_proj([mix_a.reshape(m, -1), mix_b.reshape(m, -1)],
                      [w_out0[:a_width], w_out0[a_width:]], xf, norm_post[0], norm_pre[1], tm=tm)

    w_in_all, wq_all, wk, wvt = _odd_weights(od_w_in[0], od_w_qb[0], od_w_kvb[0])
    c, gate, pe = matmul_split(h1, w_in_all, [2 * C_RANK, C_HEADS * C_V, 2 * LANES], tm=tm)
    inv_freq = ROPE_THETA ** (-jnp.arange(0, C_ROPE, 2, dtype=F32) / C_ROPE)
    freq = jnp.tile(inv_freq, LANES // (C_ROPE // 2)).reshape(1, LANES)
    scale = float((C_NOPE + C_ROPE) ** -0.5 * LOG2E)
    qn, qpe, kn, vt, kpe = mla_expand(c, pe, positions.reshape(m, 1), freq, od_q_norm[0],
                                      od_kv_norm[0], wq_all, wk, wvt, bsz=bsz, tm=min(tm, t),
                                      scale=scale)
    r3 = lambda a: a.reshape(bsz, t, -1)
    o = attention(r3(qn), r3(qpe), r3(kn), r3(kpe), vt, r3(gate), tq=min(256, t))
    (x2,) = out_proj([o.reshape(m, -1)], [od_w_out[0].astype(BF16)], x1, norm_post[1], None, tm=tm)
    return x2.reshape(bsz, t, d)
```

```python
import functools

import jax
import jax.numpy as jnp
from jax import lax
from jax.experimental import pallas as pl
from jax.experimental.pallas import tpu as pltpu

F32 = jnp.float32
BF16 = jnp.bfloat16

EPS = 1e-6
LANES = 128
VMEM_LIMIT_BYTES = 56 * 1024 * 1024

A_HEADS = 8
A_DK = 128
A_CHUNK = 64
A_SUB = 16
B_GROUPS = 8
B_DIM = 128
B_CHUNK = 128
C_HEADS = 16
C_RANK = 512
C_NOPE = 128
C_ROPE = 64
C_V = 128
ROPE_THETA = 10000.0
LOG2E = 1.4426950408889634

NT_DIMS = (((1,), (1,)), ((), ()))
TN_DIMS = (((0,), (0,)), ((), ()))


def _params(*sem):
    return pltpu.CompilerParams(dimension_semantics=sem, vmem_limit_bytes=VMEM_LIMIT_BYTES)


def _rms(x, w):
    ms = jnp.mean(x * x, axis=-1, keepdims=True)
    return x * lax.rsqrt(ms + EPS) * w


def _silu(x):
    return x * (1.0 / (1.0 + jnp.exp(-x)))


def _rms_matmul_kernel(x_ref, g_ref, w_ref, o_ref, h_ref):
    @pl.when(pl.program_id(1) == 0)
    def _():
        h_ref[...] = _rms(x_ref[...], g_ref[...]).astype(BF16)

    o_ref[...] = jnp.dot(h_ref[...], w_ref[...], preferred_element_type=F32).astype(o_ref.dtype)


def rms_matmul(x, g, w, *, tm, tn):
    m, k = x.shape
    n = w.shape[1]
    return pl.pallas_call(
        _rms_matmul_kernel,
        grid=(m // tm, n // tn),
        in_specs=[
            pl.BlockSpec((tm, k), lambda i, j: (i, 0)),
            pl.BlockSpec((1, k), lambda i, j: (0, 0)),
            pl.BlockSpec((k, tn), lambda i, j: (0, j)),
        ],
        out_specs=pl.BlockSpec((tm, tn), lambda i, j: (i, j)),
        out_shape=jax.ShapeDtypeStruct((m, n), BF16),
        scratch_shapes=[pltpu.VMEM((tm, k), BF16)],
        compiler_params=_params("parallel", "arbitrary"),
        name="rms_matmul",
    )(x, g, w)


def _hgrn2_block(q, zf, v, lb, st):
    c, s = A_CHUNK, A_SUB
    m = c // s
    tb = q.shape[0]
    nch = tb // c
    tril_mask = (lax.broadcasted_iota(jnp.int32, (c, c), 1)
                 <= lax.broadcasted_iota(jnp.int32, (c, c), 0))
    tril = jnp.where(tril_mask, 1.0, 0.0).astype(BF16)

    a = jnp.abs(zf)
    t = jnp.exp(-a)
    r = 1.0 / (1.0 + t)
    tr = t * r
    pos = zf >= 0
    sig = jnp.where(pos, r, tr)
    sig_neg = jnp.where(pos, tr, r)
    logf = jnp.log(lb + (1.0 - lb) * sig)
    kk = (1.0 - lb) * sig_neg

    hi = logf.astype(BF16)
    r1 = logf - hi.astype(F32)
    mid = r1.astype(BF16)
    lo = (r1 - mid.astype(F32)).astype(BF16)
    terms = jnp.concatenate([hi, mid, lo], axis=1)
    b_chunks = []
    for n in range(nch):
        y = jnp.dot(tril, terms[n * c:(n + 1) * c], preferred_element_type=F32)
        b_chunks.append((y[:, :A_DK] + y[:, A_DK:2 * A_DK]) + y[:, 2 * A_DK:])
    b = jnp.concatenate(b_chunks, axis=0)

    shape4 = (nch, m, s, A_DK)
    b4 = b.reshape(shape4)
    g4 = (b4 - logf.reshape(shape4))[:, :, 0:1, :]
    e4 = b4[:, :, s - 1:s, :]
    bl4 = e4[:, m - 1:m]
    lc4 = b4 - g4
    kk4 = kk.reshape(shape4)
    q_sub = q.reshape(shape4) * jnp.exp(lc4)
    k_diag = (kk4 * jnp.exp(-lc4)).astype(BF16)
    k_end = kk4 * jnp.exp(e4 - b4)
    q_in = (q_sub * jnp.exp(g4)).astype(BF16)
    k_in = (k_end * jnp.exp(bl4 - e4)).astype(BF16)
    q_sub = q_sub.astype(BF16)
    decay = jnp.exp(bl4)
    k_cross = {(i, j): (k_end[:, j] * jnp.exp(g4[:, i] - e4[:, j])).astype(BF16)
               for i in range(m) for j in range(i - 1)}
    k_end = k_end.astype(BF16)
    vb = v.astype(BF16)
    zeros = jnp.zeros((s, A_DK), BF16)

    o_intra, kv = [], []
    for n in range(nch):
        score_rows = []
        for i in range(m):
            parts = [k_end[n, j] if j == i - 1 else k_cross[(i, j)][n] for j in range(i)]
            parts.append(k_diag[n, i])
            parts.extend([zeros] * (m - 1 - i))
            k_i = jnp.concatenate(parts, axis=0)
            score_rows.append(lax.dot_general(q_sub[n, i], k_i, NT_DIMS,
                                              preferred_element_type=F32))
        scores = jnp.concatenate(score_rows, axis=0)
        scores = jnp.where(tril_mask, scores, 0.0).astype(BF16)
        vb_n = vb[n * c:(n + 1) * c]
        o_intra.append(jnp.dot(scores, vb_n, preferred_element_type=F32))
        kv.append(lax.dot_general(vb_n, k_in[n].reshape(c, A_DK), TN_DIMS,
                                  preferred_element_type=F32))

    states = []
    for n in range(nch):
        states.append(st.astype(BF16))
        st = st * decay[n, 0] + kv[n]
    outs = [o_intra[n] + lax.dot_general(q_in[n].reshape(c, A_DK), states[n], NT_DIMS,
                                         preferred_element_type=F32) for n in range(nch)]
    return jnp.concatenate(outs, axis=0), st


def _even_mixer_kernel(q_ref, f_ref, i_ref, ga_ref, u_ref, v_ref, gb_ref,
                       lbl_ref, onorm_ref, lnw_ref, lnb_ref, ws_ref, bias_ref,
                       oa_ref, ob_ref, st_ref, *, layer_j):
    tb = q_ref.shape[0]

    @pl.when(pl.program_id(2) == 0)
    def _():
        st_ref[...] = jnp.zeros_like(st_ref)

    logits = lbl_ref[...]
    e = jnp.exp(logits - jnp.max(logits, axis=0, keepdims=True))
    lb = jnp.sum(e[:layer_j + 1], axis=0, keepdims=True) / jnp.sum(e, axis=0, keepdims=True)

    o, st = _hgrn2_block(q_ref[...].astype(F32), f_ref[...].astype(F32), i_ref[...].astype(F32),
                         lb, st_ref[...])
    st_ref[...] = st
    oa_ref[...] = (_rms(o, onorm_ref[...]) * _silu(ga_ref[...].astype(F32))).astype(oa_ref.dtype)

    cb = B_CHUNK
    w_causal = jnp.where(lax.broadcasted_iota(jnp.int32, (cb, cb), 1)
                         <= lax.broadcasted_iota(jnp.int32, (cb, cb), 0),
                         ws_ref[...], 0.0).astype(BF16)
    vv = v_ref[...].astype(F32)
    mu = jnp.mean(vv, axis=-1, keepdims=True)
    xc = vv - mu
    vg = (xc * lax.rsqrt(jnp.mean(xc * xc, axis=-1, keepdims=True) + EPS) * lnw_ref[...]
          + lnb_ref[...]).astype(BF16)
    sv = jnp.concatenate([jnp.dot(w_causal, vg[n * cb:(n + 1) * cb], preferred_element_type=F32)
                          for n in range(tb // cb)], axis=0)
    sv = (sv.reshape(tb // cb, cb, B_DIM) + bias_ref[...]).reshape(tb, B_DIM)
    ob_ref[...] = (u_ref[...].astype(F32) * sv * _silu(gb_ref[...].astype(F32))).astype(ob_ref.dtype)


def even_mixers(z, lb_logits, a_onorm, ln_w, ln_b, ws, bias, *, layer_j, tb):
    bsz, t, _ = z.shape
    h = A_HEADS
    n_rows = lb_logits.shape[0]
    lbl = lb_logits.reshape(n_rows, h, A_DK).transpose(1, 0, 2)

    def sec(k):
        return pl.BlockSpec((None, tb, LANES), lambda b, j, s, k=k: (b, s, k * h + j))

    per_head = lambda shape: pl.BlockSpec((None,) + shape, lambda b, j, s: (j, 0, 0))
    out_spec = pl.BlockSpec((None, tb, LANES), lambda b, j, s: (b, s, j))
    out_sds = jax.ShapeDtypeStruct((bsz, t, h * LANES), BF16)
    return pl.pallas_call(
        functools.partial(_even_mixer_kernel, layer_j=layer_j),
        grid=(bsz, h, t // tb),
        in_specs=[sec(k) for k in range(7)] + [
            per_head((n_rows, LANES)),
            pl.BlockSpec((1, LANES), lambda b, j, s: (0, 0)),
            per_head((1, LANES)),
            per_head((1, LANES)),
            per_head((B_CHUNK, B_CHUNK)),
            per_head((B_CHUNK, 1)),
        ],
        out_specs=[out_spec, out_spec],
        out_shape=[out_sds, out_sds],
        scratch_shapes=[pltpu.VMEM((A_DK, A_DK), F32)],
        compiler_params=_params("parallel", "parallel", "arbitrary"),
        name="even_mixers",
    )(z, z, z, z, z, z, z, lbl, a_onorm.reshape(1, LANES),
      ln_w.reshape(B_GROUPS, 1, B_DIM), ln_b.reshape(B_GROUPS, 1, B_DIM),
      ws, bias.reshape(B_GROUPS, B_CHUNK, 1))


def _out_proj_kernel(*refs, n_in, with_next):
    a_refs = refs[:n_in]
    w_refs = refs[n_in:2 * n_in]
    x_ref, post_ref = refs[2 * n_in:2 * n_in + 2]
    rest = refs[2 * n_in + 2:]
    y = jnp.dot(a_refs[0][...], w_refs[0][...], preferred_element_type=F32)
    for a_ref, w_ref in zip(a_refs[1:], w_refs[1:]):
        y = y + jnp.dot(a_ref[...], w_ref[...], preferred_element_type=F32)
    x_new = x_ref[...] + _rms(y, post_ref[...])
    if with_next:
        pre_ref, xo_ref, ho_ref = rest
        xo_ref[...] = x_new
        ho_ref[...] = _rms(x_new, pre_ref[...]).astype(BF16)
    else:
        (xo_ref,) = rest
        xo_ref[...] = x_new


def out_proj(acts, weights, x, post_w, next_pre_w, *, tm):
    m, d = x.shape
    n_in = len(acts)
    with_next = next_pre_w is not None
    row = lambda width: pl.BlockSpec((tm, width), lambda i: (i, 0))
    const = lambda shape: pl.BlockSpec(shape, lambda i: (0, 0))
    in_specs = ([row(a.shape[1]) for a in acts] + [const(w.shape) for w in weights]
                + [row(d), const((1, d))])
    args = list(acts) + list(weights) + [x, post_w.reshape(1, d)]
    out_specs = [row(d)]
    out_shape = [jax.ShapeDtypeStruct((m, d), F32)]
    if with_next:
        in_specs.append(const((1, d)))
        args.append(next_pre_w.reshape(1, d))
        out_specs.append(row(d))
        out_shape.append(jax.ShapeDtypeStruct((m, d), BF16))
    return pl.pallas_call(
        functools.partial(_out_proj_kernel, n_in=n_in, with_next=with_next),
        grid=(m // tm,),
        in_specs=in_specs,
        out_specs=out_specs,
        out_shape=out_shape,
        compiler_params=_params("parallel"),
        name="out_proj",
    )(*args)


def _matmul_split_kernel(h_ref, w_ref, *o_refs):
    y = jnp.dot(h_ref[...], w_ref[...], preferred_element_type=F32)
    off = 0
    for o_ref in o_refs:
        width = o_ref.shape[1]
        o_ref[...] = y[:, off:off + width].astype(o_ref.dtype)
        off += width


def matmul_split(h, w, widths, *, tm):
    m, k = h.shape
    return pl.pallas_call(
        _matmul_split_kernel,
        grid=(m // tm,),
        in_specs=[pl.BlockSpec((tm, k), lambda i: (i, 0)),
                  pl.BlockSpec(w.shape, lambda i: (0, 0))],
        out_specs=[pl.BlockSpec((tm, wd), lambda i: (i, 0)) for wd in widths],
        out_shape=[jax.ShapeDtypeStruct((m, wd), BF16) for wd in widths],
        compiler_params=_params("parallel"),
        name="odd_in_proj",
    )(h, w)


def _mla_expand_kernel(c_ref, pe_ref, pos_ref, freq_ref, qnorm_ref, kvnorm_ref, wq_ref, wk_ref,
                       wvt_ref, qn_ref, qpe_ref, kn_ref, vt_ref, kpe_ref, *, scale):
    c = c_ref[...].astype(F32)
    cq = _rms(c[:, :C_RANK], qnorm_ref[...]).astype(BF16)
    ckv = _rms(c[:, C_RANK:], kvnorm_ref[...]).astype(BF16)
    ang = pos_ref[...].astype(F32) * freq_ref[...]
    cos = jnp.cos(ang)
    sin = jnp.sin(ang)

    n_nope = C_HEADS * C_NOPE
    n_pe = C_HEADS * C_ROPE
    q = jnp.dot(cq, wq_ref[...], preferred_element_type=F32)
    qn_ref[...] = (q[:, :n_nope] * scale).astype(qn_ref.dtype)
    rep = n_pe // LANES
    q_pe = (q[:, n_nope:n_nope + n_pe] * jnp.tile(cos, (1, rep))
            + q[:, n_nope + n_pe:] * jnp.tile(sin, (1, rep)))
    qpe_ref[...] = (q_pe * scale).astype(qpe_ref.dtype)

    kn_ref[...] = jnp.dot(ckv, wk_ref[...], preferred_element_type=F32).astype(kn_ref.dtype)
    vt_ref[...] = lax.dot_general(wvt_ref[...], ckv, NT_DIMS,
                                  preferred_element_type=F32).astype(vt_ref.dtype)

    pe = pe_ref[...].astype(F32)
    kpe_ref[...] = (pe[:, :LANES] * cos + pe[:, LANES:] * sin).astype(kpe_ref.dtype)


def mla_expand(c, pe, pos, freq, q_norm, kv_norm, wq, wk, wvt, *, bsz, tm, scale):
    m = c.shape[0]
    t = m // bsz
    nb = t // tm
    row = lambda width: pl.BlockSpec((tm, width), lambda i: (i, 0))
    const = lambda shape: pl.BlockSpec(shape, lambda i: (0, 0))
    n_nope = C_HEADS * C_NOPE
    n_v = C_HEADS * C_V
    widths = [n_nope, C_HEADS * C_ROPE, n_nope, None, LANES]
    out_specs = [row(wd) if wd else pl.BlockSpec((None, n_v, tm), lambda i: (i // nb, 0, i % nb))
                 for wd in widths]
    out_shape = [jax.ShapeDtypeStruct((m, wd) if wd else (bsz, n_v, t), BF16) for wd in widths]
    return pl.pallas_call(
        functools.partial(_mla_expand_kernel, scale=scale),
        grid=(m // tm,),
        in_specs=[row(c.shape[1]), row(pe.shape[1]), row(1), const((1, LANES)),
                  const((1, C_RANK)), const((1, C_RANK)), const(wq.shape), const(wk.shape),
                  const(wvt.shape)],
        out_specs=out_specs,
        out_shape=out_shape,
        compiler_params=_params("parallel"),
        name="mla_expand",
    )(c, pe, pos, freq, q_norm.reshape(1, C_RANK), kv_norm.reshape(1, C_RANK), wq, wk, wvt)


def _attn_kernel(qn_ref, qpe_ref, kn_ref, kpe_ref, vt_ref, g_ref, o_ref, k_scr, *, tq):
    t = qn_ref.shape[0]
    head = pl.program_id(1)
    k_scr[:, :LANES] = kn_ref[...]
    k_scr[:, LANES:] = kpe_ref[...]

    lane = lax.broadcasted_iota(jnp.int32, (1, LANES), 1)
    keep = (lane // C_ROPE) == (head % 2)
    causal = (lax.broadcasted_iota(jnp.int32, (tq, tq), 0)
              <= lax.broadcasted_iota(jnp.int32, (tq, tq), 1))
    neg = jnp.finfo(F32).min

    def scores(qi):
        r0 = qi * tq
        end = r0 + tq
        q_pe = qpe_ref[r0:end, :]
        q = jnp.concatenate([qn_ref[r0:end, :], jnp.where(keep, q_pe, jnp.zeros_like(q_pe))], axis=1)
        return lax.dot_general(k_scr[:end, :], q, NT_DIMS, preferred_element_type=F32)

    nq = t // tq
    s_next = scores(0)
    for qi in range(nq):
        r0 = qi * tq
        end = r0 + tq
        s = s_next
        if qi + 1 < nq:
            s_next = scores(qi + 1)
        s_d = jnp.where(causal, s[r0:end], neg)
        m = jnp.max(s_d, axis=0, keepdims=True)
        if qi > 0:
            m = jnp.maximum(m, jnp.max(s[:r0], axis=0, keepdims=True))
        p_d = jnp.exp2(s_d - m)
        l = jnp.sum(p_d, axis=0, keepdims=True)
        acc = jnp.dot(vt_ref[:, r0:end], p_d.astype(BF16), preferred_element_type=F32)
        if qi > 0:
            p_o = jnp.exp2(s[:r0] - m)
            l = l + jnp.sum(p_o, axis=0, keepdims=True)
            acc = acc + jnp.dot(vt_ref[:, :r0], p_o.astype(BF16), preferred_element_type=F32)
        o = (acc * (1.0 / l)).T
        o_ref[r0:end, :] = (o * _silu(g_ref[r0:end, :].astype(F32))).astype(o_ref.dtype)


def attention(qn, qpe, kn, kpe, vt, gate, *, tq):
    bsz, t, _ = qn.shape
    head = pl.BlockSpec((None, t, LANES), lambda b, h: (b, 0, h))
    pair = pl.BlockSpec((None, t, LANES), lambda b, h: (b, 0, h // 2))
    shared = pl.BlockSpec((None, t, LANES), lambda b, h: (b, 0, 0))
    head_t = pl.BlockSpec((None, C_V, t), lambda b, h: (b, h, 0))
    return pl.pallas_call(
        functools.partial(_attn_kernel, tq=tq),
        grid=(bsz, C_HEADS),
        in_specs=[head, pair, head, shared, head_t, head],
        out_specs=head,
        out_shape=jax.ShapeDtypeStruct((bsz, t, C_HEADS * C_V), BF16),
        scratch_shapes=[pltpu.VMEM((t, 2 * LANES), BF16)],
        compiler_params=_params("parallel", "parallel"),
        name="mla_attention",
    )(qn, qpe, kn, kpe, vt, gate)


def _rot_cols(w):
    lead = w.shape[:-1]
    wr = w.reshape(lead + (-1, 2, C_ROPE // 2))
    return jnp.concatenate([-wr[..., 1:2, :], wr[..., 0:1, :]], axis=-2).reshape(w.shape)


def _odd_weights(w_in, w_qb, w_kvb):
    w_c = w_in[:, :2 * C_RANK]
    w_pe = w_in[:, 2 * C_RANK:2 * C_RANK + C_ROPE]
    w_gate = w_in[:, 2 * C_RANK + C_ROPE:]
    w_pe_rot = _rot_cols(w_pe)
    w_in_all = jnp.concatenate([w_c, w_gate, w_pe, w_pe, w_pe_rot, w_pe_rot], axis=1).astype(BF16)

    qb = w_qb.reshape(C_RANK, C_HEADS, C_NOPE + C_ROPE)
    q_nope = qb[:, :, :C_NOPE].reshape(C_RANK, -1)
    q_pe = qb[:, :, C_NOPE:].reshape(C_RANK, -1)
    wq_all = jnp.concatenate([q_nope, q_pe, _rot_cols(q_pe)], axis=1).astype(BF16)

    kvb = w_kvb.reshape(C_RANK, C_HEADS, C_NOPE + C_V)
    wk = kvb[:, :, :C_NOPE].reshape(C_RANK, -1).astype(BF16)
    wvt = kvb[:, :, C_NOPE:].reshape(C_RANK, -1).T.astype(BF16)
    return w_in_all, wq_all, wk, wvt


def kernel(x, positions, norm_pre, norm_post, ev_w_in, ev_lb_logits, ev_a_onorm, ev_b_ln_w,
           ev_b_ln_b, ev_b_ws, ev_b_bias, ev_w_out, od_w_in, od_q_norm, od_w_qb, od_kv_norm,
           od_w_kvb, od_w_out):
    bsz, t, d = x.shape
    m = bsz * t
    xf = x.reshape(m, d)
    tm_big = min(1024, m)
    tm = min(512, m)

    z = rms_matmul(xf, norm_pre[0].reshape(1, d), ev_w_in[0].astype(BF16), tm=tm_big, tn=1024)
    mix_a, mix_b = even_mixers(z.reshape(bsz, t, -1), ev_lb_logits, ev_a_onorm[0], ev_b_ln_w[0],
                               ev_b_ln_b[0], ev_b_ws[0], ev_b_bias[0], layer_j=0, tb=min(512, t))
    a_width = A_HEADS * A_DK
    w_out0 = ev_w_out[0].astype(BF16)
    x1, h1 = out_proj([mix_a.reshape(m, -1), mix_b.reshape(m, -1)],
                      [w_out0[:a_width], w_out0[a_width:]], xf, norm_post[0], norm_pre[1], tm=tm)

    w_in_all, wq_all, wk, wvt = _odd_weights(od_w_in[0], od_w_qb[0], od_w_kvb[0])
    c, gate, pe = matmul_split(h1, w_in_all, [2 * C_RANK, C_HEADS * C_V, 2 * LANES], tm=tm)
    inv_freq = ROPE_THETA ** (-jnp.arange(0, C_ROPE, 2, dtype=F32) / C_ROPE)
    freq = jnp.tile(inv_freq, LANES // (C_ROPE // 2)).reshape(1, LANES)
    scale = float((C_NOPE + C_ROPE) ** -0.5 * LOG2E)
    qn, qpe, kn, vt, kpe = mla_expand(c, pe, positions.reshape(m, 1), freq, od_q_norm[0],
                                      od_kv_norm[0], wq_all, wk, wvt, bsz=bsz, tm=min(tm, t),
                                      scale=scale)
    r3 = lambda a: a.reshape(bsz, t, -1)
    o = attention(r3(qn), r3(qpe), r3(kn), r3(kpe), vt, r3(gate), tq=min(256, t))
    (x2,) = out_proj([o.reshape(m, -1)], [od_w_out[0].astype(BF16)], x1, norm_post[1], None, tm=tm)
    return x2.reshape(bsz, t, d)
```

```python
import functools

import jax
import jax.numpy as jnp
from jax import lax
from jax.experimental import pallas as pl
from jax.experimental.pallas import tpu as pltpu

F32 = jnp.float32
BF16 = jnp.bfloat16

EPS = 1e-6
LANES = 128
VMEM_LIMIT_BYTES = 56 * 1024 * 1024

A_HEADS = 8
A_DK = 128
A_CHUNK = 64
A_SUB = 16
B_GROUPS = 8
B_DIM = 128
B_CHUNK = 128
C_HEADS = 16
C_RANK = 512
C_NOPE = 128
C_ROPE = 64
C_V = 128
ROPE_THETA = 10000.0
LOG2E = 1.4426950408889634

NT_DIMS = (((1,), (1,)), ((), ()))
TN_DIMS = (((0,), (0,)), ((), ()))


def _params(*sem):
    return pltpu.CompilerParams(dimension_semantics=sem, vmem_limit_bytes=VMEM_LIMIT_BYTES)


def _rms(x, w):
    ms = jnp.mean(x * x, axis=-1, keepdims=True)
    return x * lax.rsqrt(ms + EPS) * w


def _silu(x):
    return x * (1.0 / (1.0 + jnp.exp(-x)))


def _rms_matmul_kernel(x_ref, g_ref, w_ref, o_ref, h_ref):
    @pl.when(pl.program_id(1) == 0)
    def _():
        h_ref[...] = _rms(x_ref[...], g_ref[...]).astype(BF16)

    o_ref[...] = jnp.dot(h_ref[...], w_ref[...], preferred_element_type=F32).astype(o_ref.dtype)


def rms_matmul(x, g, w, *, tm, tn):
    m, k = x.shape
    n = w.shape[1]
    return pl.pallas_call(
        _rms_matmul_kernel,
        grid=(m // tm, n // tn),
        in_specs=[
            pl.BlockSpec((tm, k), lambda i, j: (i, 0)),
            pl.BlockSpec((1, k), lambda i, j: (0, 0)),
            pl.BlockSpec((k, tn), lambda i, j: (0, j)),
        ],
        out_specs=pl.BlockSpec((tm, tn), lambda i, j: (i, j)),
        out_shape=jax.ShapeDtypeStruct((m, n), BF16),
        scratch_shapes=[pltpu.VMEM((tm, k), BF16)],
        compiler_params=_params("parallel", "arbitrary"),
        name="rms_matmul",
    )(x, g, w)


def _hgrn2_block(q, zf, v, lb, st):
    c, s = A_CHUNK, A_SUB
    m = c // s
    tb = q.shape[0]
    nch = tb // c
    tril_mask = (lax.broadcasted_iota(jnp.int32, (c, c), 1)
                 <= lax.broadcasted_iota(jnp.int32, (c, c), 0))
    tril = jnp.where(tril_mask, 1.0, 0.0).astype(BF16)

    a = jnp.abs(zf)
    t = jnp.exp(-a)
    r = 1.0 / (1.0 + t)
    tr = t * r
    pos = zf >= 0
    sig = jnp.where(pos, r, tr)
    sig_neg = jnp.where(pos, tr, r)
    logf = jnp.log(lb + (1.0 - lb) * sig)
    kk = (1.0 - lb) * sig_neg

    hi = logf.astype(BF16)
    r1 = logf - hi.astype(F32)
    mid = r1.astype(BF16)
    lo = (r1 - mid.astype(F32)).astype(BF16)
    terms = jnp.concatenate([hi, mid, lo], axis=1)
    b_chunks = []
    for n in range(nch):
        y = jnp.dot(tril, terms[n * c:(n + 1) * c], preferred_element_type=F32)
        b_chunks.append((y[:, :A_DK] + y[:, A_DK:2 * A_DK]) + y[:, 2 * A_DK:])
    b = jnp.concatenate(b_chunks, axis=0)

    shape4 = (nch, m, s, A_DK)
    b4 = b.reshape(shape4)
    g4 = (b4 - logf.reshape(shape4))[:, :, 0:1, :]
    e4 = b4[:, :, s - 1:s, :]
    bl4 = e4[:, m - 1:m]
    lc4 = b4 - g4
    kk4 = kk.reshape(shape4)
    q_sub = q.reshape(shape4) * jnp.exp(lc4)
    k_diag = (kk4 * jnp.exp(-lc4)).astype(BF16)
    k_end = kk4 * jnp.exp(e4 - b4)
    q_in = (q_sub * jnp.exp(g4)).astype(BF16)
    k_in = (k_end * jnp.exp(bl4 - e4)).astype(BF16)
    q_sub = q_sub.astype(BF16)
    decay = jnp.exp(bl4)
    k_cross = {(i, j): (k_end[:, j] * jnp.exp(g4[:, i] - e4[:, j])).astype(BF16)
               for i in range(m) for j in range(i - 1)}
    k_end = k_end.astype(BF16)
    vb = v.astype(BF16)
    zeros = jnp.zeros((s, A_DK), BF16)

    o_intra, kv = [], []
    for n in range(nch):
        score_rows = []
        for i in range(m):
            parts = [k_end[n, j] if j == i - 1 else k_cross[(i, j)][n] for j in range(i)]
            parts.append(k_diag[n, i])
            parts.extend([zeros] * (m - 1 - i))
            k_i = jnp.concatenate(parts, axis=0)
            score_rows.append(lax.dot_general(q_sub[n, i], k_i, NT_DIMS,
                                              preferred_element_type=F32))
        scores = jnp.concatenate(score_rows, axis=0)
        scores = jnp.where(tril_mask, scores, 0.0).astype(BF16)
        vb_n = vb[n * c:(n + 1) * c]
        o_intra.append(jnp.dot(scores, vb_n, preferred_element_type=F32))
        kv.append(lax.dot_general(vb_n, k_in[n].reshape(c, A_DK), TN_DIMS,
                                  preferred_element_type=F32))

    states = []
    for n in range(nch):
        states.append(st.astype(BF16))
        st = st * decay[n, 0] + kv[n]
    outs = [o_intra[n] + lax.dot_general(q_in[n].reshape(c, A_DK), states[n], NT_DIMS,
                                         preferred_element_type=F32) for n in range(nch)]
    return jnp.concatenate(outs, axis=0), st


def _even_mixer_kernel(q_ref, f_ref, i_ref, ga_ref, u_ref, v_ref, gb_ref,
                       lbl_ref, onorm_ref, lnw_ref, lnb_ref, ws_ref, bias_ref,
                       oa_ref, ob_ref, st_ref, *, layer_j):
    tb = q_ref.shape[0]

    @pl.when(pl.program_id(2) == 0)
    def _():
        st_ref[...] = jnp.zeros_like(st_ref)

    logits = lbl_ref[...]
    e = jnp.exp(logits - jnp.max(logits, axis=0, keepdims=True))
    lb = jnp.sum(e[:layer_j + 1], axis=0, keepdims=True) / jnp.sum(e, axis=0, keepdims=True)

    o, st = _hgrn2_block(q_ref[...].astype(F32), f_ref[...].astype(F32), i_ref[...].astype(F32),
                         lb, st_ref[...])
    st_ref[...] = st
    oa_ref[...] = (_rms(o, onorm_ref[...]) * _silu(ga_ref[...].astype(F32))).astype(oa_ref.dtype)

    cb = B_CHUNK
    w_causal = jnp.where(lax.broadcasted_iota(jnp.int32, (cb, cb), 1)
                         <= lax.broadcasted_iota(jnp.int32, (cb, cb), 0),
                         ws_ref[...], 0.0).astype(BF16)
    vv = v_ref[...].astype(F32)
    mu = jnp.mean(vv, axis=-1, keepdims=True)
    xc = vv - mu
    vg = (xc * lax.rsqrt(jnp.mean(xc * xc, axis=-1, keepdims=True) + EPS) * lnw_ref[...]
          + lnb_ref[...]).astype(BF16)
    sv = jnp.concatenate([jnp.dot(w_causal, vg[n * cb:(n + 1) * cb], preferred_element_type=F32)
                          for n in range(tb // cb)], axis=0)
    sv = (sv.reshape(tb // cb, cb, B_DIM) + bias_ref[...]).reshape(tb, B_DIM)
    ob_ref[...] = (u_ref[...].astype(F32) * sv * _silu(gb_ref[...].astype(F32))).astype(ob_ref.dtype)


def even_mixers(z, lb_logits, a_onorm, ln_w, ln_b, ws, bias, *, layer_j, tb):
    bsz, t, _ = z.shape
    h = A_HEADS
    n_rows = lb_logits.shape[0]
    lbl = lb_logits.reshape(n_rows, h, A_DK).transpose(1, 0, 2)

    def sec(k):
        return pl.BlockSpec((None, tb, LANES), lambda b, j, s, k=k: (b, s, k * h + j))

    per_head = lambda shape: pl.BlockSpec((None,) + shape, lambda b, j, s: (j, 0, 0))
    out_spec = pl.BlockSpec((None, tb, LANES), lambda b, j, s: (b, s, j))
    out_sds = jax.ShapeDtypeStruct((bsz, t, h * LANES), BF16)
    return pl.pallas_call(
        functools.partial(_even_mixer_kernel, layer_j=layer_j),
        grid=(bsz, h, t // tb),
        in_specs=[sec(k) for k in range(7)] + [
            per_head((n_rows, LANES)),
            pl.BlockSpec((1, LANES), lambda b, j, s: (0, 0)),
            per_head((1, LANES)),
            per_head((1, LANES)),
            per_head((B_CHUNK, B_CHUNK)),
            per_head((B_CHUNK, 1)),
        ],
        out_specs=[out_spec, out_spec],
        out_shape=[out_sds, out_sds],
        scratch_shapes=[pltpu.VMEM((A_DK, A_DK), F32)],
        compiler_params=_params("parallel", "parallel", "arbitrary"),
        name="even_mixers",
    )(z, z, z, z, z, z, z, lbl, a_onorm.reshape(1, LANES),
      ln_w.reshape(B_GROUPS, 1, B_DIM), ln_b.reshape(B_GROUPS, 1, B_DIM),
      ws, bias.reshape(B_GROUPS, B_CHUNK, 1))


def _out_proj_kernel(*refs, n_in, with_next):
    a_refs = refs[:n_in]
    w_refs = refs[n_in:2 * n_in]
    x_ref, post_ref = refs[2 * n_in:2 * n_in + 2]
    rest = refs[2 * n_in + 2:]
    if with_next:
        pre_ref, xo_ref, ho_ref = rest
    else:
        (xo_ref,) = rest
    tm = x_ref.shape[0]
    n_sub = 2 if tm % 32 == 0 else 1
    sub = tm // n_sub

    def project(k):
        rows = slice(k * sub, (k + 1) * sub)
        y = jnp.dot(a_refs[0][rows, :], w_refs[0][...], preferred_element_type=F32)
        for a_ref, w_ref in zip(a_refs[1:], w_refs[1:]):
            y = y + jnp.dot(a_ref[rows, :], w_ref[...], preferred_element_type=F32)
        return y

    def finish(k, y):
        rows = slice(k * sub, (k + 1) * sub)
        x_new = x_ref[rows, :] + _rms(y, post_ref[...])
        xo_ref[rows, :] = x_new
        if with_next:
            ho_ref[rows, :] = _rms(x_new, pre_ref[...]).astype(BF16)

    ys = [project(0)]
    for k in range(n_sub):
        if k + 1 < n_sub:
            ys.append(project(k + 1))
        finish(k, ys[k])


def out_proj(acts, w, x, post_w, next_pre_w, *, tm):
    m, d = x.shape
    n_in = len(acts)
    ka = acts[0].shape[1]
    assert all(a.shape[1] == ka for a in acts) and w.shape == (n_in * ka, d)
    with_next = next_pre_w is not None
    row = lambda width: pl.BlockSpec((tm, width), lambda i: (i, 0))
    const = lambda shape: pl.BlockSpec(shape, lambda i: (0, 0))
    in_specs = ([row(ka) for _ in acts]
                + [pl.BlockSpec((ka, d), lambda i, r=r: (r, 0)) for r in range(n_in)]
                + [row(d), const((1, d))])
    args = list(acts) + [w] * n_in + [x, post_w.reshape(1, d)]
    out_specs = [row(d)]
    out_shape = [jax.ShapeDtypeStruct((m, d), F32)]
    if with_next:
        in_specs.append(const((1, d)))
        args.append(next_pre_w.reshape(1, d))
        out_specs.append(row(d))
        out_shape.append(jax.ShapeDtypeStruct((m, d), BF16))
    return pl.pallas_call(
        functools.partial(_out_proj_kernel, n_in=n_in, with_next=with_next),
        grid=(m // tm,),
        in_specs=in_specs,
        out_specs=out_specs,
        out_shape=out_shape,
        compiler_params=_params("parallel"),
        name="out_proj",
    )(*args)


def _multi_matmul_kernel(h_ref, *refs):
    n = len(refs) // 2
    h = h_ref[...]
    for w_ref, o_ref in zip(refs[:n], refs[n:]):
        o_ref[...] = jnp.dot(h, w_ref[...], preferred_element_type=F32).astype(o_ref.dtype)


def multi_matmul(h, weights, *, tm):
    m, k = h.shape
    return pl.pallas_call(
        _multi_matmul_kernel,
        grid=(m // tm,),
        in_specs=[pl.BlockSpec((tm, k), lambda i: (i, 0))]
                 + [pl.BlockSpec(w.shape, lambda i: (0, 0)) for w in weights],
        out_specs=[pl.BlockSpec((tm, w.shape[1]), lambda i: (i, 0)) for w in weights],
        out_shape=[jax.ShapeDtypeStruct((m, w.shape[1]), BF16) for w in weights],
        compiler_params=_params("parallel"),
        name="odd_in_proj",
    )(h, *weights)


def _mla_expand_kernel(c_ref, pe_ref, pos_ref, freq_ref, qnorm_ref, kvnorm_ref, wq_ref, wk_ref,
                       wvt_ref, qn_ref, qpe_ref, kn_ref, vt_ref, kpe_ref, *, scale):
    c = c_ref[...].astype(F32)
    cq = _rms(c[:, :C_RANK], qnorm_ref[...]).astype(BF16)
    ckv = _rms(c[:, C_RANK:], kvnorm_ref[...]).astype(BF16)

    n_nope = C_HEADS * C_NOPE
    n_pe = C_HEADS * C_ROPE
    q = jnp.dot(cq, wq_ref[...], preferred_element_type=F32)
    kn_ref[...] = jnp.dot(ckv, wk_ref[...], preferred_element_type=F32).astype(kn_ref.dtype)
    vt_ref[...] = lax.dot_general(wvt_ref[...], ckv, NT_DIMS,
                                  preferred_element_type=F32).astype(vt_ref.dtype)

    ang = pos_ref[...].astype(F32) * freq_ref[...]
    cos = jnp.cos(ang)
    sin = jnp.sin(ang)
    qn_ref[...] = (q[:, :n_nope] * scale).astype(qn_ref.dtype)
    rep = n_pe // LANES
    q_pe = (q[:, n_nope:n_nope + n_pe] * jnp.tile(cos, (1, rep))
            + q[:, n_nope + n_pe:] * jnp.tile(sin, (1, rep)))
    qpe_ref[...] = (q_pe * scale).astype(qpe_ref.dtype)

    pe = pe_ref[...].astype(F32)
    kpe_ref[...] = (pe[:, :LANES] * cos + pe[:, LANES:] * sin).astype(kpe_ref.dtype)


def mla_expand(c, pe, pos, freq, q_norm, kv_norm, wq, wk, wvt, *, bsz, tm, scale):
    m = c.shape[0]
    t = m // bsz
    nb = t // tm
    row = lambda width: pl.BlockSpec((tm, width), lambda i: (i, 0))
    const = lambda shape: pl.BlockSpec(shape, lambda i: (0, 0))
    n_nope = C_HEADS * C_NOPE
    n_v = C_HEADS * C_V
    widths = [n_nope, C_HEADS * C_ROPE, n_nope, None, LANES]
    out_specs = [row(wd) if wd else pl.BlockSpec((None, n_v, tm), lambda i: (i // nb, 0, i % nb))
                 for wd in widths]
    out_shape = [jax.ShapeDtypeStruct((m, wd) if wd else (bsz, n_v, t), BF16) for wd in widths]
    return pl.pallas_call(
        functools.partial(_mla_expand_kernel, scale=scale),
        grid=(m // tm,),
        in_specs=[row(c.shape[1]), row(pe.shape[1]), row(1), const((1, LANES)),
                  const((1, C_RANK)), const((1, C_RANK)), const(wq.shape), const(wk.shape),
                  const(wvt.shape)],
        out_specs=out_specs,
        out_shape=out_shape,
        compiler_params=_params("parallel"),
        name="mla_expand",
    )(c, pe, pos, freq, q_norm.reshape(1, C_RANK), kv_norm.reshape(1, C_RANK), wq, wk, wvt)


def _attn_kernel(qn_ref, qpe_ref, kn_ref, kpe_ref, vt_ref, g_ref, o_ref, k_scr, *, tq):
    t = qn_ref.shape[0]
    head = pl.program_id(1)
    k_scr[:, :LANES] = kn_ref[...]
    k_scr[:, LANES:] = kpe_ref[...]

    lane = lax.broadcasted_iota(jnp.int32, (1, LANES), 1)
    keep = (lane // C_ROPE) == (head % 2)
    causal = (lax.broadcasted_iota(jnp.int32, (tq, tq), 0)
              <= lax.broadcasted_iota(jnp.int32, (tq, tq), 1))
    neg = jnp.finfo(F32).min

    def scores(qi):
        r0 = qi * tq
        end = r0 + tq
        q_pe = qpe_ref[r0:end, :]
        q = jnp.concatenate([qn_ref[r0:end, :], jnp.where(keep, q_pe, jnp.zeros_like(q_pe))], axis=1)
        return lax.dot_general(k_scr[:end, :], q, NT_DIMS, preferred_element_type=F32)

    nq = t // tq
    s_next = scores(0)
    for qi in range(nq):
        r0 = qi * tq
        end = r0 + tq
        s = s_next
        if qi + 1 < nq:
            s_next = scores(qi + 1)
        s_d = jnp.where(causal, s[r0:end], neg)
        m = jnp.max(s_d, axis=0, keepdims=True)
        if qi > 0:
            m = jnp.maximum(m, jnp.max(s[:r0], axis=0, keepdims=True))
        p_d = jnp.exp2(s_d - m)
        l = jnp.sum(p_d, axis=0, keepdims=True)
        acc = jnp.dot(vt_ref[:, r0:end], p_d.astype(BF16), preferred_element_type=F32)
        if qi > 0:
            p_o = jnp.exp2(s[:r0] - m)
            l = l + jnp.sum(p_o, axis=0, keepdims=True)
            acc = acc + jnp.dot(vt_ref[:, :r0], p_o.astype(BF16), preferred_element_type=F32)
        o = (acc * (1.0 / l)).T
        o_ref[r0:end, :] = (o * _silu(g_ref[r0:end, :].astype(F32))).astype(o_ref.dtype)


def attention(qn, qpe, kn, kpe, vt, gate, *, tq):
    bsz, t, _ = qn.shape
    head = pl.BlockSpec((None, t, LANES), lambda b, h: (b, 0, h))
    pair = pl.BlockSpec((None, t, LANES), lambda b, h: (b, 0, h // 2))
    shared = pl.BlockSpec((None, t, LANES), lambda b, h: (b, 0, 0))
    head_t = pl.BlockSpec((None, C_V, t), lambda b, h: (b, h, 0))
    return pl.pallas_call(
        functools.partial(_attn_kernel, tq=tq),
        grid=(bsz, C_HEADS),
        in_specs=[head, pair, head, shared, head_t, head],
        out_specs=head,
        out_shape=jax.ShapeDtypeStruct((bsz, t, C_HEADS * C_V), BF16),
        scratch_shapes=[pltpu.VMEM((t, 2 * LANES), BF16)],
        compiler_params=_params("parallel", "parallel"),
        name="mla_attention",
    )(qn, qpe, kn, kpe, vt, gate)


def _odd_weights(w_in, w_qb, w_kvb):
    half = C_ROPE // 2
    w_c = w_in[:, :2 * C_RANK].astype(BF16)
    w_gate = w_in[:, 2 * C_RANK + C_ROPE:].astype(BF16)
    w_pe = w_in[:, 2 * C_RANK:2 * C_RANK + C_ROPE].astype(BF16)
    w_pe_rot = jnp.concatenate([-w_pe[:, half:], w_pe[:, :half]], axis=1)
    w_pe4 = jnp.concatenate([w_pe, w_pe, w_pe_rot, w_pe_rot], axis=1)

    w_qb = w_qb.astype(BF16)
    dq = C_NOPE + C_ROPE
    nope, pe, pe_rot = [], [], []
    for h in range(C_HEADS):
        nope.append(w_qb[:, h * dq:h * dq + C_NOPE])
        lo = w_qb[:, h * dq + C_NOPE:h * dq + C_NOPE + half]
        hi = w_qb[:, h * dq + C_NOPE + half:(h + 1) * dq]
        pe += [lo, hi]
        pe_rot += [-hi, lo]
    wq_all = jnp.concatenate(nope + pe + pe_rot, axis=1)

    kvb = w_kvb.astype(BF16).reshape(C_RANK, C_HEADS, C_NOPE + C_V)
    wk = kvb[:, :, :C_NOPE].reshape(C_RANK, -1)
    wvt = kvb[:, :, C_NOPE:].reshape(C_RANK, -1).T
    return (w_c, w_gate, w_pe4), wq_all, wk, wvt


def kernel(x, positions, norm_pre, norm_post, ev_w_in, ev_lb_logits, ev_a_onorm, ev_b_ln_w,
           ev_b_ln_b, ev_b_ws, ev_b_bias, ev_w_out, od_w_in, od_q_norm, od_w_qb, od_kv_norm,
           od_w_kvb, od_w_out):
    bsz, t, d = x.shape
    m = bsz * t
    xf = x.reshape(m, d)
    tm_big = min(1024, m)
    tm = min(512, m)

    z = rms_matmul(xf, norm_pre[0].reshape(1, d), ev_w_in[0].astype(BF16), tm=tm_big, tn=1024)
    mix_a, mix_b = even_mixers(z.reshape(bsz, t, -1), ev_lb_logits, ev_a_onorm[0], ev_b_ln_w[0],
                               ev_b_ln_b[0], ev_b_ws[0], ev_b_bias[0], layer_j=0, tb=min(512, t))
    x1, h1 = out_proj([mix_a.reshape(m, -1), mix_b.reshape(m, -1)], ev_w_out[0].astype(BF16),
                      xf, norm_post[0], norm_pre[1], tm=tm)

    w_in_parts, wq_all, wk, wvt = _odd_weights(od_w_in[0], od_w_qb[0], od_w_kvb[0])
    c, gate, pe = multi_matmul(h1, w_in_parts, tm=tm)
    inv_freq = ROPE_THETA ** (-jnp.arange(0, C_ROPE, 2, dtype=F32) / C_ROPE)
    freq = jnp.tile(inv_freq, LANES // (C_ROPE // 2)).reshape(1, LANES)
    scale = float((C_NOPE + C_ROPE) ** -0.5 * LOG2E)
    qn, qpe, kn, vt, kpe = mla_expand(c, pe, positions.reshape(m, 1), freq, od_q_norm[0],
                                      od_kv_norm[0], wq_all, wk, wvt, bsz=bsz, tm=min(tm, t),
                                      scale=scale)
    r3 = lambda a: a.reshape(bsz, t, -1)
    o = attention(r3(qn), r3(qpe), r3(kn), r3(kpe), vt, r3(gate), tq=min(256, t))
    (x2,) = out_proj([o.reshape(m, -1)], od_w_out[0].astype(BF16), x1, norm_post[1], None, tm=tm)
    return x2.reshape(bsz, t, d)
```

```python
import functools

import jax
import jax.numpy as jnp
from jax import lax
from jax.experimental import pallas as pl
from jax.experimental.pallas import tpu as pltpu

F32 = jnp.float32
BF16 = jnp.bfloat16

EPS = 1e-6
LANES = 128
VMEM_LIMIT_BYTES = 56 * 1024 * 1024

A_HEADS = 8
A_DK = 128
A_CHUNK = 64
A_SUB = 16
B_GROUPS = 8
B_DIM = 128
B_CHUNK = 128
C_HEADS = 16
C_RANK = 512
C_NOPE = 128
C_ROPE = 64
C_V = 128
ROPE_THETA = 10000.0
LOG2E = 1.4426950408889634

NT_DIMS = (((1,), (1,)), ((), ()))
TN_DIMS = (((0,), (0,)), ((), ()))


def _params(*sem):
    return pltpu.CompilerParams(dimension_semantics=sem, vmem_limit_bytes=VMEM_LIMIT_BYTES)


def _rms(x, w):
    ms = jnp.mean(x * x, axis=-1, keepdims=True)
    return x * lax.rsqrt(ms + EPS) * w


def _silu(x):
    return x * (1.0 / (1.0 + jnp.exp(-x)))


def _rms_matmul_kernel(x_ref, g_ref, w_ref, o_ref, h_ref):
    @pl.when(pl.program_id(1) == 0)
    def _():
        h_ref[...] = _rms(x_ref[...], g_ref[...]).astype(BF16)

    o_ref[...] = jnp.dot(h_ref[...], w_ref[...], preferred_element_type=F32).astype(o_ref.dtype)


def rms_matmul(x, g, w, *, tm, tn):
    m, k = x.shape
    n = w.shape[1]
    return pl.pallas_call(
        _rms_matmul_kernel,
        grid=(m // tm, n // tn),
        in_specs=[
            pl.BlockSpec((tm, k), lambda i, j: (i, 0)),
            pl.BlockSpec((1, k), lambda i, j: (0, 0)),
            pl.BlockSpec((k, tn), lambda i, j: (0, j)),
        ],
        out_specs=pl.BlockSpec((tm, tn), lambda i, j: (i, j)),
        out_shape=jax.ShapeDtypeStruct((m, n), BF16),
        scratch_shapes=[pltpu.VMEM((tm, k), BF16)],
        compiler_params=_params("parallel", "arbitrary"),
        name="rms_matmul",
    )(x, g, w)


def _hgrn2_block(q, zf, v, lb, st):
    c, s = A_CHUNK, A_SUB
    m = c // s
    tb = q.shape[0]
    nch = tb // c
    tril_mask = (lax.broadcasted_iota(jnp.int32, (c, c), 1)
                 <= lax.broadcasted_iota(jnp.int32, (c, c), 0))
    tril = jnp.where(tril_mask, 1.0, 0.0).astype(BF16)

    a = jnp.abs(zf)
    t = jnp.exp(-a)
    r = 1.0 / (1.0 + t)
    tr = t * r
    pos = zf >= 0
    sig = jnp.where(pos, r, tr)
    sig_neg = jnp.where(pos, tr, r)
    logf = jnp.log(lb + (1.0 - lb) * sig)
    kk = (1.0 - lb) * sig_neg

    hi = logf.astype(BF16)
    r1 = logf - hi.astype(F32)
    mid = r1.astype(BF16)
    lo = (r1 - mid.astype(F32)).astype(BF16)
    terms = jnp.concatenate([hi, mid, lo], axis=1)
    b_chunks = []
    for n in range(nch):
        y = jnp.dot(tril, terms[n * c:(n + 1) * c], preferred_element_type=F32)
        b_chunks.append((y[:, :A_DK] + y[:, A_DK:2 * A_DK]) + y[:, 2 * A_DK:])
    b = jnp.concatenate(b_chunks, axis=0)

    shape4 = (nch, m, s, A_DK)
    b4 = b.reshape(shape4)
    g4 = (b4 - logf.reshape(shape4))[:, :, 0:1, :]
    e4 = b4[:, :, s - 1:s, :]
    bl4 = e4[:, m - 1:m]
    lc4 = b4 - g4
    kk4 = kk.reshape(shape4)
    q_sub = q.reshape(shape4) * jnp.exp(lc4)
    k_diag = (kk4 * jnp.exp(-lc4)).astype(BF16)
    k_end = kk4 * jnp.exp(e4 - b4)
    q_in = (q_sub * jnp.exp(g4)).astype(BF16)
    k_in = (k_end * jnp.exp(bl4 - e4)).astype(BF16)
    q_sub = q_sub.astype(BF16)
    decay = jnp.exp(bl4)
    k_cross = {(i, j): (k_end[:, j] * jnp.exp(g4[:, i] - e4[:, j])).astype(BF16)
               for i in range(m) for j in range(i - 1)}
    k_end = k_end.astype(BF16)
    vb = v.astype(BF16)
    zeros = jnp.zeros((s, A_DK), BF16)

    o_intra, kv = [], []
    for n in range(nch):
        score_rows = []
        for i in range(m):
            parts = [k_end[n, j] if j == i - 1 else k_cross[(i, j)][n] for j in range(i)]
            parts.append(k_diag[n, i])
            parts.extend([zeros] * (m - 1 - i))
            k_i = jnp.concatenate(parts, axis=0)
            score_rows.append(lax.dot_general(q_sub[n, i], k_i, NT_DIMS,
                                              preferred_element_type=F32))
        scores = jnp.concatenate(score_rows, axis=0)
        scores = jnp.where(tril_mask, scores, 0.0).astype(BF16)
        vb_n = vb[n * c:(n + 1) * c]
        o_intra.append(jnp.dot(scores, vb_n, preferred_element_type=F32))
        kv.append(lax.dot_general(vb_n, k_in[n].reshape(c, A_DK), TN_DIMS,
                                  preferred_element_type=F32))

    states = []
    for n in range(nch):
        states.append(st.astype(BF16))
        st = st * decay[n, 0] + kv[n]
    outs = [o_intra[n] + lax.dot_general(q_in[n].reshape(c, A_DK), states[n], NT_DIMS,
                                         preferred_element_type=F32) for n in range(nch)]
    return jnp.concatenate(outs, axis=0), st


def _even_mixer_kernel(q_ref, f_ref, i_ref, ga_ref, u_ref, v_ref, gb_ref,
                       lbl_ref, onorm_ref, lnw_ref, lnb_ref, ws_ref, bias_ref,
                       oa_ref, ob_ref, st_ref, *, layer_j):
    tb = q_ref.shape[0]

    @pl.when(pl.program_id(2) == 0)
    def _():
        st_ref[...] = jnp.zeros_like(st_ref)

    logits = lbl_ref[...]
    e = jnp.exp(logits - jnp.max(logits, axis=0, keepdims=True))
    lb = jnp.sum(e[:layer_j + 1], axis=0, keepdims=True) / jnp.sum(e, axis=0, keepdims=True)

    o, st = _hgrn2_block(q_ref[...].astype(F32), f_ref[...].astype(F32), i_ref[...].astype(F32),
                         lb, st_ref[...])
    st_ref[...] = st
    oa_ref[...] = (_rms(o, onorm_ref[...]) * _silu(ga_ref[...].astype(F32))).astype(oa_ref.dtype)

    cb = B_CHUNK
    w_causal = jnp.where(lax.broadcasted_iota(jnp.int32, (cb, cb), 1)
                         <= lax.broadcasted_iota(jnp.int32, (cb, cb), 0),
                         ws_ref[...], 0.0).astype(BF16)
    vv = v_ref[...].astype(F32)
    mu = jnp.mean(vv, axis=-1, keepdims=True)
    xc = vv - mu
    vg = (xc * lax.rsqrt(jnp.mean(xc * xc, axis=-1, keepdims=True) + EPS) * lnw_ref[...]
          + lnb_ref[...]).astype(BF16)
    sv = jnp.concatenate([jnp.dot(w_causal, vg[n * cb:(n + 1) * cb], preferred_element_type=F32)
                          for n in range(tb // cb)], axis=0)
    sv = (sv.reshape(tb // cb, cb, B_DIM) + bias_ref[...]).reshape(tb, B_DIM)
    ob_ref[...] = (u_ref[...].astype(F32) * sv * _silu(gb_ref[...].astype(F32))).astype(ob_ref.dtype)


def even_mixers(z, lb_logits, a_onorm, ln_w, ln_b, ws, bias, *, layer_j, tb):
    bsz, t, _ = z.shape
    h = A_HEADS
    n_rows = lb_logits.shape[0]
    lbl = lb_logits.reshape(n_rows, h, A_DK).transpose(1, 0, 2)

    def sec(k):
        return pl.BlockSpec((None, tb, LANES), lambda b, j, s, k=k: (b, s, k * h + j))

    per_head = lambda shape: pl.BlockSpec((None,) + shape, lambda b, j, s: (j, 0, 0))
    out_spec = pl.BlockSpec((None, tb, LANES), lambda b, j, s: (b, s, j))
    out_sds = jax.ShapeDtypeStruct((bsz, t, h * LANES), BF16)
    return pl.pallas_call(
        functools.partial(_even_mixer_kernel, layer_j=layer_j),
        grid=(bsz, h, t // tb),
        in_specs=[sec(k) for k in range(7)] + [
            per_head((n_rows, LANES)),
            pl.BlockSpec((1, LANES), lambda b, j, s: (0, 0)),
            per_head((1, LANES)),
            per_head((1, LANES)),
            per_head((B_CHUNK, B_CHUNK)),
            per_head((B_CHUNK, 1)),
        ],
        out_specs=[out_spec, out_spec],
        out_shape=[out_sds, out_sds],
        scratch_shapes=[pltpu.VMEM((A_DK, A_DK), F32)],
        compiler_params=_params("parallel", "parallel", "arbitrary"),
        name="even_mixers",
    )(z, z, z, z, z, z, z, lbl, a_onorm.reshape(1, LANES),
      ln_w.reshape(B_GROUPS, 1, B_DIM), ln_b.reshape(B_GROUPS, 1, B_DIM),
      ws, bias.reshape(B_GROUPS, B_CHUNK, 1))


def _out_proj_kernel(*refs, n_in, with_next):
    a_refs = refs[:n_in]
    w_refs = refs[n_in:2 * n_in]
    x_ref, post_ref = refs[2 * n_in:2 * n_in + 2]
    rest = refs[2 * n_in + 2:]
    if with_next:
        pre_ref, xo_ref, ho_ref = rest
    else:
        (xo_ref,) = rest
    tm = x_ref.shape[0]
    n_sub = 2 if tm % 32 == 0 else 1
    sub = tm // n_sub

    def project(k):
        rows = slice(k * sub, (k + 1) * sub)
        y = jnp.dot(a_refs[0][rows, :], w_refs[0][...], preferred_element_type=F32)
        for a_ref, w_ref in zip(a_refs[1:], w_refs[1:]):
            y = y + jnp.dot(a_ref[rows, :], w_ref[...], preferred_element_type=F32)
        return y

    def finish(k, y):
        rows = slice(k * sub, (k + 1) * sub)
        x_new = x_ref[rows, :] + _rms(y, post_ref[...])
        xo_ref[rows, :] = x_new
        if with_next:
            ho_ref[rows, :] = _rms(x_new, pre_ref[...]).astype(BF16)

    ys = [project(0)]
    for k in range(n_sub):
        if k + 1 < n_sub:
            ys.append(project(k + 1))
        finish(k, ys[k])


def out_proj(acts, w, x, post_w, next_pre_w, *, tm):
    m, d = x.shape
    n_in = len(acts)
    ka = acts[0].shape[1]
    assert all(a.shape[1] == ka for a in acts) and w.shape == (n_in * ka, d)
    with_next = next_pre_w is not None
    row = lambda width: pl.BlockSpec((tm, width), lambda i: (i, 0))
    const = lambda shape: pl.BlockSpec(shape, lambda i: (0, 0))
    in_specs = ([row(ka) for _ in acts]
                + [pl.BlockSpec((ka, d), lambda i, r=r: (r, 0)) for r in range(n_in)]
                + [row(d), const((1, d))])
    args = list(acts) + [w] * n_in + [x, post_w.reshape(1, d)]
    out_specs = [row(d)]
    out_shape = [jax.ShapeDtypeStruct((m, d), F32)]
    if with_next:
        in_specs.append(const((1, d)))
        args.append(next_pre_w.reshape(1, d))
        out_specs.append(row(d))
        out_shape.append(jax.ShapeDtypeStruct((m, d), BF16))
    return pl.pallas_call(
        functools.partial(_out_proj_kernel, n_in=n_in, with_next=with_next),
        grid=(m // tm,),
        in_specs=in_specs,
        out_specs=out_specs,
        out_shape=out_shape,
        compiler_params=_params("parallel"),
        name="out_proj",
    )(*args)


def _lane_iota():
    return lax.broadcasted_iota(jnp.int32, (1, LANES), 1)


def _join_halves(left, right):
    half = LANES // 2
    return pltpu.roll(jnp.where(_lane_iota() < half, right, left), half, axis=1)


def _odd_in_proj_kernel(h_ref, w_ref, c_ref, pe_ref, gate_ref):
    y = jnp.dot(h_ref[...], w_ref[...], preferred_element_type=F32)
    n_c = c_ref.shape[1]
    c_ref[...] = y[:, :n_c].astype(c_ref.dtype)
    pe_ref[...] = y[:, n_c:n_c + LANES].astype(pe_ref.dtype)
    for j in range(gate_ref.shape[1] // LANES):
        lo = n_c + j * LANES
        tile = _join_halves(y[:, lo:lo + LANES], y[:, lo + LANES:lo + 2 * LANES])
        gate_ref[:, j * LANES:(j + 1) * LANES] = tile.astype(gate_ref.dtype)


def odd_in_proj(h, w, *, tm):
    m, k = h.shape
    widths = [2 * C_RANK, LANES, C_HEADS * C_V]
    return pl.pallas_call(
        _odd_in_proj_kernel,
        grid=(m // tm,),
        in_specs=[pl.BlockSpec((tm, k), lambda i: (i, 0)),
                  pl.BlockSpec(w.shape, lambda i: (0, 0))],
        out_specs=[pl.BlockSpec((tm, wd), lambda i: (i, 0)) for wd in widths],
        out_shape=[jax.ShapeDtypeStruct((m, wd), BF16) for wd in widths],
        compiler_params=_params("parallel"),
        name="odd_in_proj",
    )(h, w)


def _mla_expand_kernel(c_ref, pe_ref, pos_ref, freq_ref, qnorm_ref, kvnorm_ref, wq_ref, wk_ref,
                       wvt_ref, qn_ref, qpe_ref, kn_ref, vt_ref, kpe_ref, *, scale):
    c = c_ref[...].astype(F32)
    cq = _rms(c[:, :C_RANK], qnorm_ref[...]).astype(BF16)
    ckv = _rms(c[:, C_RANK:], kvnorm_ref[...]).astype(BF16)

    q = jnp.dot(cq, wq_ref[...], preferred_element_type=F32)
    kn_ref[...] = jnp.dot(ckv, wk_ref[...], preferred_element_type=F32).astype(kn_ref.dtype)
    vt_ref[...] = lax.dot_general(wvt_ref[...], ckv, NT_DIMS,
                                  preferred_element_type=F32).astype(vt_ref.dtype)

    ang = pos_ref[...].astype(F32) * freq_ref[...]
    cos = jnp.cos(ang)
    sin = jnp.sin(ang)
    lane = _lane_iota()
    first_half = lane < C_ROPE
    low_rot = (lane % C_ROPE) < C_ROPE // 2

    def rope(x):
        rot = jnp.where(low_rot, -pltpu.roll(x, LANES - C_ROPE // 2, axis=1),
                        pltpu.roll(x, C_ROPE // 2, axis=1))
        return x * cos + rot * sin

    for p in range(C_HEADS // 2):
        a, b, c3 = (q[:, (3 * p + k) * LANES:(3 * p + k + 1) * LANES] for k in range(3))
        qn_ref[:, 2 * p * LANES:(2 * p + 1) * LANES] = (a * scale).astype(qn_ref.dtype)
        qn_ref[:, (2 * p + 1) * LANES:(2 * p + 2) * LANES] = (
            _join_halves(b, c3) * scale).astype(qn_ref.dtype)
        pe_pair = jnp.where(first_half, b, c3)
        qpe_ref[:, p * LANES:(p + 1) * LANES] = (rope(pe_pair) * scale).astype(qpe_ref.dtype)

    k_pe = rope(pe_ref[...].astype(F32))
    kpe_ref[...] = jnp.where(first_half, k_pe,
                             pltpu.roll(k_pe, C_ROPE, axis=1)).astype(kpe_ref.dtype)


def mla_expand(c, pe, pos, freq, q_norm, kv_norm, wq, wk, wvt, *, bsz, tm, scale):
    m = c.shape[0]
    t = m // bsz
    nb = t // tm
    row = lambda width: pl.BlockSpec((tm, width), lambda i: (i, 0))
    const = lambda shape: pl.BlockSpec(shape, lambda i: (0, 0))
    n_nope = C_HEADS * C_NOPE
    n_v = C_HEADS * C_V
    widths = [n_nope, C_HEADS * C_ROPE, n_nope, None, LANES]
    out_specs = [row(wd) if wd else pl.BlockSpec((None, n_v, tm), lambda i: (i // nb, 0, i % nb))
                 for wd in widths]
    out_shape = [jax.ShapeDtypeStruct((m, wd) if wd else (bsz, n_v, t), BF16) for wd in widths]
    return pl.pallas_call(
        functools.partial(_mla_expand_kernel, scale=scale),
        grid=(m // tm,),
        in_specs=[row(c.shape[1]), row(pe.shape[1]), row(1), const((1, LANES)),
                  const((1, C_RANK)), const((1, C_RANK)), const(wq.shape), const(wk.shape),
                  const(wvt.shape)],
        out_specs=out_specs,
        out_shape=out_shape,
        compiler_params=_params("parallel"),
        name="mla_expand",
    )(c, pe, pos, freq, q_norm.reshape(1, C_RANK), kv_norm.reshape(1, C_RANK), wq, wk, wvt)


def _attn_kernel(qn_ref, qpe_ref, kn_ref, kpe_ref, vt_ref, g_ref, o_ref, k_scr, *, tq):
    t = qn_ref.shape[0]
    head = pl.program_id(1)
    k_scr[:, :LANES] = kn_ref[...]
    k_scr[:, LANES:] = kpe_ref[...]

    lane = lax.broadcasted_iota(jnp.int32, (1, LANES), 1)
    keep = (lane // C_ROPE) == (head % 2)
    causal = (lax.broadcasted_iota(jnp.int32, (tq, tq), 0)
              <= lax.broadcasted_iota(jnp.int32, (tq, tq), 1))
    neg = jnp.finfo(F32).min

    def scores(qi):
        r0 = qi * tq
        end = r0 + tq
        q_pe = qpe_ref[r0:end, :]
        q = jnp.concatenate([qn_ref[r0:end, :], jnp.where(keep, q_pe, jnp.zeros_like(q_pe))], axis=1)
        return lax.dot_general(k_scr[:end, :], q, NT_DIMS, preferred_element_type=F32)

    nq = t // tq
    s_next = scores(0)
    for qi in range(nq):
        r0 = qi * tq
        end = r0 + tq
        s = s_next
        if qi + 1 < nq:
            s_next = scores(qi + 1)
        s_d = jnp.where(causal, s[r0:end], neg)
        m = jnp.max(s_d, axis=0, keepdims=True)
        if qi > 0:
            m = jnp.maximum(m, jnp.max(s[:r0], axis=0, keepdims=True))
        p_d = jnp.exp2(s_d - m)
        l = jnp.sum(p_d, axis=0, keepdims=True)
        acc = jnp.dot(vt_ref[:, r0:end], p_d.astype(BF16), preferred_element_type=F32)
        if qi > 0:
            p_o = jnp.exp2(s[:r0] - m)
            l = l + jnp.sum(p_o, axis=0, keepdims=True)
            acc = acc + jnp.dot(vt_ref[:, :r0], p_o.astype(BF16), preferred_element_type=F32)
        o = (acc * (1.0 / l)).T
        o_ref[r0:end, :] = (o * _silu(g_ref[r0:end, :].astype(F32))).astype(o_ref.dtype)


def attention(qn, qpe, kn, kpe, vt, gate, *, tq):
    bsz, t, _ = qn.shape
    head = pl.BlockSpec((None, t, LANES), lambda b, h: (b, 0, h))
    pair = pl.BlockSpec((None, t, LANES), lambda b, h: (b, 0, h // 2))
    shared = pl.BlockSpec((None, t, LANES), lambda b, h: (b, 0, 0))
    head_t = pl.BlockSpec((None, C_V, t), lambda b, h: (b, h, 0))
    return pl.pallas_call(
        functools.partial(_attn_kernel, tq=tq),
        grid=(bsz, C_HEADS),
        in_specs=[head, pair, head, shared, head_t, head],
        out_specs=head,
        out_shape=jax.ShapeDtypeStruct((bsz, t, C_HEADS * C_V), BF16),
        scratch_shapes=[pltpu.VMEM((t, 2 * LANES), BF16)],
        compiler_params=_params("parallel", "parallel"),
        name="mla_attention",
    )(qn, qpe, kn, kpe, vt, gate)


def _odd_weights(w_in, w_qb, w_kvb):
    n_in = w_in.shape[1]
    n_pad = -(-(n_in + LANES // 2) // (2 * LANES)) * (2 * LANES) - n_in
    w_in_b = jnp.pad(w_in, ((0, 0), (0, n_pad))).astype(BF16)
    kvb = w_kvb.astype(BF16).reshape(C_RANK, C_HEADS, C_NOPE + C_V)
    wk = kvb[:, :, :C_NOPE].reshape(C_RANK, -1)
    wvt = kvb[:, :, C_NOPE:].reshape(C_RANK, -1).T
    return w_in_b, w_qb.astype(BF16), wk, wvt


def kernel(x, positions, norm_pre, norm_post, ev_w_in, ev_lb_logits, ev_a_onorm, ev_b_ln_w,
           ev_b_ln_b, ev_b_ws, ev_b_bias, ev_w_out, od_w_in, od_q_norm, od_w_qb, od_kv_norm,
           od_w_kvb, od_w_out):
    bsz, t, d = x.shape
    m = bsz * t
    xf = x.reshape(m, d)
    tm_big = min(1024, m)
    tm = min(512, m)

    z = rms_matmul(xf, norm_pre[0].reshape(1, d), ev_w_in[0].astype(BF16), tm=tm_big, tn=1024)
    mix_a, mix_b = even_mixers(z.reshape(bsz, t, -1), ev_lb_logits, ev_a_onorm[0], ev_b_ln_w[0],
                               ev_b_ln_b[0], ev_b_ws[0], ev_b_bias[0], layer_j=0, tb=min(512, t))
    x1, h1 = out_proj([mix_a.reshape(m, -1), mix_b.reshape(m, -1)], ev_w_out[0].astype(BF16),
                      xf, norm_post[0], norm_pre[1], tm=tm)

    w_in_b, wq_all, wk, wvt = _odd_weights(od_w_in[0], od_w_qb[0], od_w_kvb[0])
    c, pe, gate = odd_in_proj(h1, w_in_b, tm=tm)
    inv_freq = ROPE_THETA ** (-jnp.arange(0, C_ROPE, 2, dtype=F32) / C_ROPE)
    freq = jnp.tile(inv_freq, LANES // (C_ROPE // 2)).reshape(1, LANES)
    scale = float((C_NOPE + C_ROPE) ** -0.5 * LOG2E)
    qn, qpe, kn, vt, kpe = mla_expand(c, pe, positions.reshape(m, 1), freq, od_q_norm[0],
                                      od_kv_norm[0], wq_all, wk, wvt, bsz=bsz, tm=min(tm, t),
                                      scale=scale)
    r3 = lambda a: a.reshape(bsz, t, -1)
    o = attention(r3(qn), r3(qpe), r3(kn), r3(kpe), vt, r3(gate), tq=min(256, t))
    (x2,) = out_proj([o.reshape(m, -1)], od_w_out[0].astype(BF16), x1, norm_post[1], None, tm=tm)
    return x2.reshape(bsz, t, d)
```

```python
import functools

import jax
import jax.numpy as jnp
from jax import lax
from jax.experimental import pallas as pl
from jax.experimental.pallas import tpu as pltpu

F32 = jnp.float32
BF16 = jnp.bfloat16

EPS = 1e-6
LANES = 128
VMEM_LIMIT_BYTES = 56 * 1024 * 1024

A_HEADS = 8
A_DK = 128
A_CHUNK = 64
A_SUB = 16
B_GROUPS = 8
B_DIM = 128
B_CHUNK = 128
C_HEADS = 16
C_RANK = 512
C_NOPE = 128
C_ROPE = 64
C_V = 128
ROPE_THETA = 10000.0
LOG2E = 1.4426950408889634
QK_SPLIT = 2

NT_DIMS = (((1,), (1,)), ((), ()))
TN_DIMS = (((0,), (0,)), ((), ()))


def _params(*sem):
    return pltpu.CompilerParams(dimension_semantics=sem, vmem_limit_bytes=VMEM_LIMIT_BYTES)


def _rms(x, w):
    ms = jnp.mean(x * x, axis=-1, keepdims=True)
    return x * lax.rsqrt(ms + EPS) * w


def _silu(x):
    return x * (1.0 / (1.0 + jnp.exp(-x)))


def _rms_matmul_kernel(x_ref, g_ref, w_ref, o_ref, h_ref):
    @pl.when(pl.program_id(1) == 0)
    def _():
        h_ref[...] = _rms(x_ref[...], g_ref[...]).astype(BF16)

    o_ref[...] = jnp.dot(h_ref[...], w_ref[...], preferred_element_type=F32).astype(o_ref.dtype)


def rms_matmul(x, g, w, *, tm, tn):
    m, k = x.shape
    n = w.shape[1]
    return pl.pallas_call(
        _rms_matmul_kernel,
        grid=(m // tm, n // tn),
        in_specs=[
            pl.BlockSpec((tm, k), lambda i, j: (i, 0)),
            pl.BlockSpec((1, k), lambda i, j: (0, 0)),
            pl.BlockSpec((k, tn), lambda i, j: (0, j)),
        ],
        out_specs=pl.BlockSpec((tm, tn), lambda i, j: (i, j)),
        out_shape=jax.ShapeDtypeStruct((m, n), BF16),
        scratch_shapes=[pltpu.VMEM((tm, k), BF16)],
        compiler_params=_params("parallel", "arbitrary"),
        name="rms_matmul",
    )(x, g, w)


def _hgrn2_block(q, zf, v, lb, st):
    c, s = A_CHUNK, A_SUB
    m = c // s
    tb = q.shape[0]
    nch = tb // c
    tril_mask = (lax.broadcasted_iota(jnp.int32, (c, c), 1)
                 <= lax.broadcasted_iota(jnp.int32, (c, c), 0))
    tril = jnp.where(tril_mask, 1.0, 0.0).astype(BF16)

    a = jnp.abs(zf)
    t = jnp.exp(-a)
    r = 1.0 / (1.0 + t)
    tr = t * r
    pos = zf >= 0
    sig = jnp.where(pos, r, tr)
    sig_neg = jnp.where(pos, tr, r)
    logf = jnp.log(lb + (1.0 - lb) * sig)
    kk = (1.0 - lb) * sig_neg

    hi = logf.astype(BF16)
    r1 = logf - hi.astype(F32)
    mid = r1.astype(BF16)
    lo = (r1 - mid.astype(F32)).astype(BF16)
    terms = jnp.concatenate([hi, mid, lo], axis=1)
    b_chunks = []
    for n in range(nch):
        y = jnp.dot(tril, terms[n * c:(n + 1) * c], preferred_element_type=F32)
        b_chunks.append((y[:, :A_DK] + y[:, A_DK:2 * A_DK]) + y[:, 2 * A_DK:])
    b = jnp.concatenate(b_chunks, axis=0)

    shape4 = (nch, m, s, A_DK)
    b4 = b.reshape(shape4)
    g4 = (b4 - logf.reshape(shape4))[:, :, 0:1, :]
    e4 = b4[:, :, s - 1:s, :]
    bl4 = e4[:, m - 1:m]
    lc4 = b4 - g4
    kk4 = kk.reshape(shape4)
    q_sub = q.reshape(shape4) * jnp.exp(lc4)
    k_diag = (kk4 * jnp.exp(-lc4)).astype(BF16)
    k_end = kk4 * jnp.exp(e4 - b4)
    q_in = (q_sub * jnp.exp(g4)).astype(BF16)
    k_in = (k_end * jnp.exp(bl4 - e4)).astype(BF16)
    q_sub = q_sub.astype(BF16)
    decay = jnp.exp(bl4)
    k_cross = {(i, j): (k_end[:, j] * jnp.exp(g4[:, i] - e4[:, j])).astype(BF16)
               for i in range(m) for j in range(i - 1)}
    k_end = k_end.astype(BF16)
    vb = v.astype(BF16)
    zeros = jnp.zeros((s, A_DK), BF16)

    o_intra, kv = [], []
    for n in range(nch):
        score_rows = []
        for i in range(m):
            parts = [k_end[n, j] if j == i - 1 else k_cross[(i, j)][n] for j in range(i)]
            parts.append(k_diag[n, i])
            parts.extend([zeros] * (m - 1 - i))
            k_i = jnp.concatenate(parts, axis=0)
            score_rows.append(lax.dot_general(q_sub[n, i], k_i, NT_DIMS,
                                              preferred_element_type=F32))
        scores = jnp.concatenate(score_rows, axis=0)
        scores = jnp.where(tril_mask, scores, 0.0).astype(BF16)
        vb_n = vb[n * c:(n + 1) * c]
        o_intra.append(jnp.dot(scores, vb_n, preferred_element_type=F32))
        kv.append(lax.dot_general(vb_n, k_in[n].reshape(c, A_DK), TN_DIMS,
                                  preferred_element_type=F32))

    states = []
    for n in range(nch):
        states.append(st.astype(BF16))
        st = st * decay[n, 0] + kv[n]
    outs = [o_intra[n] + lax.dot_general(q_in[n].reshape(c, A_DK), states[n], NT_DIMS,
                                         preferred_element_type=F32) for n in range(nch)]
    return jnp.concatenate(outs, axis=0), st


def _even_mixer_kernel(q_ref, f_ref, i_ref, ga_ref, u_ref, v_ref, gb_ref,
                       lbl_ref, onorm_ref, lnw_ref, lnb_ref, ws_ref, bias_ref,
                       oa_ref, ob_ref, st_ref, *, layer_j):
    tb = q_ref.shape[0]

    @pl.when(pl.program_id(2) == 0)
    def _():
        st_ref[...] = jnp.zeros_like(st_ref)

    logits = lbl_ref[...]
    e = jnp.exp(logits - jnp.max(logits, axis=0, keepdims=True))
    lb = jnp.sum(e[:layer_j + 1], axis=0, keepdims=True) / jnp.sum(e, axis=0, keepdims=True)

    o, st = _hgrn2_block(q_ref[...].astype(F32), f_ref[...].astype(F32), i_ref[...].astype(F32),
                         lb, st_ref[...])
    st_ref[...] = st
    oa_ref[...] = (_rms(o, onorm_ref[...]) * _silu(ga_ref[...].astype(F32))).astype(oa_ref.dtype)

    cb = B_CHUNK
    w_causal = jnp.where(lax.broadcasted_iota(jnp.int32, (cb, cb), 1)
                         <= lax.broadcasted_iota(jnp.int32, (cb, cb), 0),
                         ws_ref[...], 0.0).astype(BF16)
    vv = v_ref[...].astype(F32)
    mu = jnp.mean(vv, axis=-1, keepdims=True)
    xc = vv - mu
    vg = (xc * lax.rsqrt(jnp.mean(xc * xc, axis=-1, keepdims=True) + EPS) * lnw_ref[...]
          + lnb_ref[...]).astype(BF16)
    sv = jnp.concatenate([jnp.dot(w_causal, vg[n * cb:(n + 1) * cb], preferred_element_type=F32)
                          for n in range(tb // cb)], axis=0)
    sv = (sv.reshape(tb // cb, cb, B_DIM) + bias_ref[...]).reshape(tb, B_DIM)
    ob_ref[...] = (u_ref[...].astype(F32) * sv * _silu(gb_ref[...].astype(F32))).astype(ob_ref.dtype)


def even_mixers(z, lb_logits, a_onorm, ln_w, ln_b, ws, bias, *, layer_j, tb):
    bsz, t, _ = z.shape
    h = A_HEADS
    n_rows = lb_logits.shape[0]
    lbl = lb_logits.reshape(n_rows, h, A_DK).transpose(1, 0, 2)

    def sec(k):
        return pl.BlockSpec((None, tb, LANES), lambda b, j, s, k=k: (b, s, k * h + j))

    per_head = lambda shape: pl.BlockSpec((None,) + shape, lambda b, j, s: (j, 0, 0))
    out_spec = pl.BlockSpec((None, tb, LANES), lambda b, j, s: (b, s, j))
    out_sds = jax.ShapeDtypeStruct((bsz, t, h * LANES), BF16)
    return pl.pallas_call(
        functools.partial(_even_mixer_kernel, layer_j=layer_j),
        grid=(bsz, h, t // tb),
        in_specs=[sec(k) for k in range(7)] + [
            per_head((n_rows, LANES)),
            pl.BlockSpec((1, LANES), lambda b, j, s: (0, 0)),
            per_head((1, LANES)),
            per_head((1, LANES)),
            per_head((B_CHUNK, B_CHUNK)),
            per_head((B_CHUNK, 1)),
        ],
        out_specs=[out_spec, out_spec],
        out_shape=[out_sds, out_sds],
        scratch_shapes=[pltpu.VMEM((A_DK, A_DK), F32)],
        compiler_params=_params("parallel", "parallel", "arbitrary"),
        name="even_mixers",
    )(z, z, z, z, z, z, z, lbl, a_onorm.reshape(1, LANES),
      ln_w.reshape(B_GROUPS, 1, B_DIM), ln_b.reshape(B_GROUPS, 1, B_DIM),
      ws, bias.reshape(B_GROUPS, B_CHUNK, 1))


def _out_proj_kernel(*refs, n_in, with_next):
    a_refs = refs[:n_in]
    w_refs = refs[n_in:2 * n_in]
    x_ref, post_ref = refs[2 * n_in:2 * n_in + 2]
    rest = refs[2 * n_in + 2:]
    if with_next:
        pre_ref, xo_ref, ho_ref = rest
    else:
        (xo_ref,) = rest
    tm = x_ref.shape[0]
    n_sub = 2 if tm % 32 == 0 else 1
    sub = tm // n_sub

    def project(k):
        rows = slice(k * sub, (k + 1) * sub)
        y = jnp.dot(a_refs[0][rows, :], w_refs[0][...], preferred_element_type=F32)
        for a_ref, w_ref in zip(a_refs[1:], w_refs[1:]):
            y = y + jnp.dot(a_ref[rows, :], w_ref[...], preferred_element_type=F32)
        return y

    def finish(k, y):
        rows = slice(k * sub, (k + 1) * sub)
        x_new = x_ref[rows, :] + _rms(y, post_ref[...])
        xo_ref[rows, :] = x_new
        if with_next:
            ho_ref[rows, :] = _rms(x_new, pre_ref[...]).astype(BF16)

    ys = [project(0)]
    for k in range(n_sub):
        if k + 1 < n_sub:
            ys.append(project(k + 1))
        finish(k, ys[k])


def out_proj(acts, w, x, post_w, next_pre_w, *, tm):
    m, d = x.shape
    n_in = len(acts)
    ka = acts[0].shape[1]
    assert all(a.shape[1] == ka for a in acts) and w.shape == (n_in * ka, d)
    with_next = next_pre_w is not None
    row = lambda width: pl.BlockSpec((tm, width), lambda i: (i, 0))
    const = lambda shape: pl.BlockSpec(shape, lambda i: (0, 0))
    in_specs = ([row(ka) for _ in acts]
                + [pl.BlockSpec((ka, d), lambda i, r=r: (r, 0)) for r in range(n_in)]
                + [row(d), const((1, d))])
    args = list(acts) + [w] * n_in + [x, post_w.reshape(1, d)]
    out_specs = [row(d)]
    out_shape = [jax.ShapeDtypeStruct((m, d), F32)]
    if with_next:
        in_specs.append(const((1, d)))
        args.append(next_pre_w.reshape(1, d))
        out_specs.append(row(d))
        out_shape.append(jax.ShapeDtypeStruct((m, d), BF16))
    return pl.pallas_call(
        functools.partial(_out_proj_kernel, n_in=n_in, with_next=with_next),
        grid=(m // tm,),
        in_specs=in_specs,
        out_specs=out_specs,
        out_shape=out_shape,
        compiler_params=_params("parallel"),
        name="out_proj",
    )(*args)


def _lane_iota():
    return lax.broadcasted_iota(jnp.int32, (1, LANES), 1)


def _join_halves(left, right):
    half = LANES // 2
    return pltpu.roll(jnp.where(_lane_iota() < half, right, left), half, axis=1)


def _odd_in_proj_kernel(h_ref, w_ref, c_ref, pe_ref, gate_ref):
    y = jnp.dot(h_ref[...], w_ref[...], preferred_element_type=F32)
    n_c = c_ref.shape[1]
    c_ref[...] = y[:, :n_c].astype(c_ref.dtype)
    pe_ref[...] = y[:, n_c:n_c + LANES].astype(pe_ref.dtype)
    for j in range(gate_ref.shape[1] // LANES):
        lo = n_c + j * LANES
        tile = _join_halves(y[:, lo:lo + LANES], y[:, lo + LANES:lo + 2 * LANES])
        gate_ref[:, j * LANES:(j + 1) * LANES] = tile.astype(gate_ref.dtype)


def odd_in_proj(h, w, *, tm):
    m, k = h.shape
    widths = [2 * C_RANK, LANES, C_HEADS * C_V]
    return pl.pallas_call(
        _odd_in_proj_kernel,
        grid=(m // tm,),
        in_specs=[pl.BlockSpec((tm, k), lambda i: (i, 0)),
                  pl.BlockSpec(w.shape, lambda i: (0, 0))],
        out_specs=[pl.BlockSpec((tm, wd), lambda i: (i, 0)) for wd in widths],
        out_shape=[jax.ShapeDtypeStruct((m, wd), BF16) for wd in widths],
        compiler_params=_params("parallel"),
        name="odd_in_proj",
    )(h, w)


def _mla_expand_kernel(c_ref, pe_ref, pos_ref, freq_ref, qnorm_ref, kvnorm_ref, wq_ref, wk_ref,
                       wvt_ref, qn_ref, qpe_ref, kn_ref, vt_ref, kpe_ref, *, scale):
    c = c_ref[...].astype(F32)
    cq = _rms(c[:, :C_RANK], qnorm_ref[...]).astype(BF16)
    ckv = _rms(c[:, C_RANK:], kvnorm_ref[...]).astype(BF16)

    q = jnp.dot(cq, wq_ref[...], preferred_element_type=F32)
    kn_ref[...] = jnp.dot(ckv, wk_ref[...], preferred_element_type=F32).astype(kn_ref.dtype)
    vt_ref[...] = lax.dot_general(wvt_ref[...], ckv, NT_DIMS,
                                  preferred_element_type=F32).astype(vt_ref.dtype)

    ang = pos_ref[...].astype(F32) * freq_ref[...]
    cos = jnp.cos(ang)
    sin = jnp.sin(ang)
    lane = _lane_iota()
    first_half = lane < C_ROPE
    low_rot = (lane % C_ROPE) < C_ROPE // 2

    def rope(x):
        rot = jnp.where(low_rot, -pltpu.roll(x, LANES - C_ROPE // 2, axis=1),
                        pltpu.roll(x, C_ROPE // 2, axis=1))
        return x * cos + rot * sin

    for p in range(C_HEADS // 2):
        a, b, c3 = (q[:, (3 * p + k) * LANES:(3 * p + k + 1) * LANES] for k in range(3))
        qn_ref[:, 2 * p * LANES:(2 * p + 1) * LANES] = (a * scale).astype(qn_ref.dtype)
        qn_ref[:, (2 * p + 1) * LANES:(2 * p + 2) * LANES] = (
            _join_halves(b, c3) * scale).astype(qn_ref.dtype)
        pe_pair = jnp.where(first_half, b, c3)
        qpe_ref[:, p * LANES:(p + 1) * LANES] = (rope(pe_pair) * scale).astype(qpe_ref.dtype)

    k_pe = rope(pe_ref[...].astype(F32))
    kpe_ref[...] = jnp.where(first_half, k_pe,
                             pltpu.roll(k_pe, C_ROPE, axis=1)).astype(kpe_ref.dtype)


def mla_expand(c, pe, pos, freq, q_norm, kv_norm, wq, wk, wvt, *, bsz, tm, scale):
    m = c.shape[0]
    t = m // bsz
    nb = t // tm
    row = lambda width: pl.BlockSpec((tm, width), lambda i: (i, 0))
    const = lambda shape: pl.BlockSpec(shape, lambda i: (0, 0))
    n_nope = C_HEADS * C_NOPE
    n_v = C_HEADS * C_V
    widths = [n_nope, C_HEADS * C_ROPE, n_nope, None, LANES]
    out_specs = [row(wd) if wd else pl.BlockSpec((None, n_v, tm), lambda i: (i // nb, 0, i % nb))
                 for wd in widths]
    out_shape = [jax.ShapeDtypeStruct((m, wd) if wd else (bsz, n_v, t), BF16) for wd in widths]
    return pl.pallas_call(
        functools.partial(_mla_expand_kernel, scale=scale),
        grid=(m // tm,),
        in_specs=[row(c.shape[1]), row(pe.shape[1]), row(1), const((1, LANES)),
                  const((1, C_RANK)), const((1, C_RANK)), const(wq.shape), const(wk.shape),
                  const(wvt.shape)],
        out_specs=out_specs,
        out_shape=out_shape,
        compiler_params=_params("parallel"),
        name="mla_expand",
    )(c, pe, pos, freq, q_norm.reshape(1, C_RANK), kv_norm.reshape(1, C_RANK), wq, wk, wvt)


def _attn_kernel(qn_ref, qpe_ref, kn_ref, kpe_ref, vt_ref, g_ref, o_ref, k_scr, *, tq):
    t = qn_ref.shape[0]
    head = pl.program_id(1)
    k_scr[:, :LANES] = kn_ref[...]
    k_scr[:, LANES:] = kpe_ref[...]

    lane = lax.broadcasted_iota(jnp.int32, (1, LANES), 1)
    keep = (lane // C_ROPE) == (head % 2)
    causal = (lax.broadcasted_iota(jnp.int32, (tq, tq), 0)
              <= lax.broadcasted_iota(jnp.int32, (tq, tq), 1))
    neg = jnp.finfo(F32).min

    def scores(qi):
        r0 = qi * tq
        end = r0 + tq
        q_pe = qpe_ref[r0:end, :]
        q = jnp.concatenate([qn_ref[r0:end, :], jnp.where(keep, q_pe, jnp.zeros_like(q_pe))], axis=1)
        part = end // QK_SPLIT
        return jnp.concatenate(
            [lax.dot_general(k_scr[j * part:(j + 1) * part, :], q, NT_DIMS,
                             preferred_element_type=F32) for j in range(QK_SPLIT)],
            axis=0)

    def col_max(qi, s):
        r0 = qi * tq
        s_d = jnp.where(causal, s[r0:], neg)
        m = jnp.max(s_d, axis=0, keepdims=True)
        if qi > 0:
            m = jnp.maximum(m, jnp.max(s[:r0], axis=0, keepdims=True))
        return s_d, m

    def probs(qi, s, s_d, m):
        r0 = qi * tq
        p_d = jnp.exp2(s_d - m)
        l = jnp.sum(p_d, axis=0, keepdims=True)
        p_o = None
        if qi > 0:
            p_o = jnp.exp2(s[:r0] - m)
            l = l + jnp.sum(p_o, axis=0, keepdims=True)
            p_o = p_o.astype(BF16)
        return p_d.astype(BF16), p_o, l

    def values(qi, p_d, p_o, l):
        r0 = qi * tq
        end = r0 + tq
        acc = jnp.dot(vt_ref[:, r0:end], p_d, preferred_element_type=F32)
        if qi > 0:
            acc = acc + jnp.dot(vt_ref[:, :r0], p_o, preferred_element_type=F32)
        o = (acc * (1.0 / l)).T
        o_ref[r0:end, :] = (o * _silu(g_ref[r0:end, :].astype(F32))).astype(o_ref.dtype)

    nq = t // tq
    s = scores(0)
    s_d, m = col_max(0, s)
    for qi in range(nq):
        if qi + 1 < nq:
            s_next = scores(qi + 1)
        p_d, p_o, l = probs(qi, s, s_d, m)
        values(qi, p_d, p_o, l)
        if qi + 1 < nq:
            s = s_next
            s_d, m = col_max(qi + 1, s)


def attention(qn, qpe, kn, kpe, vt, gate, *, tq):
    bsz, t, _ = qn.shape
    head = pl.BlockSpec((None, t, LANES), lambda b, h: (b, 0, h))
    pair = pl.BlockSpec((None, t, LANES), lambda b, h: (b, 0, h // 2))
    shared = pl.BlockSpec((None, t, LANES), lambda b, h: (b, 0, 0))
    head_t = pl.BlockSpec((None, C_V, t), lambda b, h: (b, h, 0))
    return pl.pallas_call(
        functools.partial(_attn_kernel, tq=tq),
        grid=(bsz, C_HEADS),
        in_specs=[head, pair, head, shared, head_t, head],
        out_specs=head,
        out_shape=jax.ShapeDtypeStruct((bsz, t, C_HEADS * C_V), BF16),
        scratch_shapes=[pltpu.VMEM((t, 2 * LANES), BF16)],
        compiler_params=_params("parallel", "parallel"),
        name="mla_attention",
    )(qn, qpe, kn, kpe, vt, gate)


def _odd_weights(w_in, w_qb, w_kvb):
    n_in = w_in.shape[1]
    n_pad = -(-(n_in + LANES // 2) // (2 * LANES)) * (2 * LANES) - n_in
    w_in_b = jnp.pad(w_in, ((0, 0), (0, n_pad))).astype(BF16)
    kvb = w_kvb.astype(BF16).reshape(C_RANK, C_HEADS, C_NOPE + C_V)
    wk = kvb[:, :, :C_NOPE].reshape(C_RANK, -1)
    wvt = kvb[:, :, C_NOPE:].reshape(C_RANK, -1).T
    return w_in_b, w_qb.astype(BF16), wk, wvt


def kernel(x, positions, norm_pre, norm_post, ev_w_in, ev_lb_logits, ev_a_onorm, ev_b_ln_w,
           ev_b_ln_b, ev_b_ws, ev_b_bias, ev_w_out, od_w_in, od_q_norm, od_w_qb, od_kv_norm,
           od_w_kvb, od_w_out):
    bsz, t, d = x.shape
    m = bsz * t
    xf = x.reshape(m, d)
    tm_big = min(1024, m)
    tm = min(512, m)

    z = rms_matmul(xf, norm_pre[0].reshape(1, d), ev_w_in[0].astype(BF16), tm=tm_big, tn=1792)
    mix_a, mix_b = even_mixers(z.reshape(bsz, t, -1), ev_lb_logits, ev_a_onorm[0], ev_b_ln_w[0],
                               ev_b_ln_b[0], ev_b_ws[0], ev_b_bias[0], layer_j=0, tb=min(512, t))
    x1, h1 = out_proj([mix_a.reshape(m, -1), mix_b.reshape(m, -1)], ev_w_out[0].astype(BF16),
                      xf, norm_post[0], norm_pre[1], tm=tm)

    w_in_b, wq_all, wk, wvt = _odd_weights(od_w_in[0], od_w_qb[0], od_w_kvb[0])
    c, pe, gate = odd_in_proj(h1, w_in_b, tm=tm)
    inv_freq = ROPE_THETA ** (-jnp.arange(0, C_ROPE, 2, dtype=F32) / C_ROPE)
    freq = jnp.tile(inv_freq, LANES // (C_ROPE // 2)).reshape(1, LANES)
    scale = float((C_NOPE + C_ROPE) ** -0.5 * LOG2E)
    qn, qpe, kn, vt, kpe = mla_expand(c, pe, positions.reshape(m, 1), freq, od_q_norm[0],
                                      od_kv_norm[0], wq_all, wk, wvt, bsz=bsz, tm=min(tm, t),
                                      scale=scale)
    r3 = lambda a: a.reshape(bsz, t, -1)
    o = attention(r3(qn), r3(qpe), r3(kn), r3(kpe), vt, r3(gate), tq=min(256, t))
    (x2,) = out_proj([o.reshape(m, -1)], od_w_out[0].astype(BF16), x1, norm_post[1], None, tm=tm)
    return x2.reshape(bsz, t, d)
```

```python
import functools

import jax
import jax.numpy as jnp
from jax import lax
from jax.experimental import pallas as pl
from jax.experimental.pallas import tpu as pltpu

F32 = jnp.float32
BF16 = jnp.bfloat16

EPS = 1e-6
LANES = 128
VMEM_LIMIT_BYTES = 56 * 1024 * 1024

A_HEADS = 8
A_DK = 128
A_CHUNK = 64
A_SUB = 16
B_GROUPS = 8
B_DIM = 128
B_CHUNK = 128
C_HEADS = 16
C_RANK = 512
C_NOPE = 128
C_ROPE = 64
C_V = 128
ROPE_THETA = 10000.0
LOG2E = 1.4426950408889634
QK_SPLIT = 1

NT_DIMS = (((1,), (1,)), ((), ()))
TN_DIMS = (((0,), (0,)), ((), ()))


def _params(*sem):
    return pltpu.CompilerParams(dimension_semantics=sem, vmem_limit_bytes=VMEM_LIMIT_BYTES)


def _rms(x, w):
    ms = jnp.mean(x * x, axis=-1, keepdims=True)
    return x * lax.rsqrt(ms + EPS) * w


def _silu(x):
    return x * (1.0 / (1.0 + jnp.exp(-x)))


def _rms_matmul_kernel(x_ref, g_ref, w_ref, o_ref, h_ref):
    @pl.when(pl.program_id(1) == 0)
    def _():
        h_ref[...] = _rms(x_ref[...], g_ref[...]).astype(BF16)

    o_ref[...] = jnp.dot(h_ref[...], w_ref[...], preferred_element_type=F32).astype(o_ref.dtype)


def rms_matmul(x, g, w, *, tm, tn):
    m, k = x.shape
    n = w.shape[1]
    return pl.pallas_call(
        _rms_matmul_kernel,
        grid=(m // tm, n // tn),
        in_specs=[
            pl.BlockSpec((tm, k), lambda i, j: (i, 0)),
            pl.BlockSpec((1, k), lambda i, j: (0, 0)),
            pl.BlockSpec((k, tn), lambda i, j: (0, j)),
        ],
        out_specs=pl.BlockSpec((tm, tn), lambda i, j: (i, j)),
        out_shape=jax.ShapeDtypeStruct((m, n), BF16),
        scratch_shapes=[pltpu.VMEM((tm, k), BF16)],
        compiler_params=_params("parallel", "arbitrary"),
        name="rms_matmul",
    )(x, g, w)


def _hgrn2_block(q, zf, v, lb, st):
    c, s = A_CHUNK, A_SUB
    m = c // s
    tb = q.shape[0]
    nch = tb // c
    tril_mask = (lax.broadcasted_iota(jnp.int32, (c, c), 1)
                 <= lax.broadcasted_iota(jnp.int32, (c, c), 0))
    tril = jnp.where(tril_mask, 1.0, 0.0).astype(BF16)

    a = jnp.abs(zf)
    t = jnp.exp(-a)
    r = 1.0 / (1.0 + t)
    tr = t * r
    pos = zf >= 0
    sig = jnp.where(pos, r, tr)
    sig_neg = jnp.where(pos, tr, r)
    logf = jnp.log(lb + (1.0 - lb) * sig)
    kk = (1.0 - lb) * sig_neg

    hi = logf.astype(BF16)
    r1 = logf - hi.astype(F32)
    mid = r1.astype(BF16)
    lo = (r1 - mid.astype(F32)).astype(BF16)
    terms = jnp.concatenate([hi, mid, lo], axis=1)
    b_chunks = []
    for n in range(nch):
        y = jnp.dot(tril, terms[n * c:(n + 1) * c], preferred_element_type=F32)
        b_chunks.append((y[:, :A_DK] + y[:, A_DK:2 * A_DK]) + y[:, 2 * A_DK:])
    b = jnp.concatenate(b_chunks, axis=0)

    shape4 = (nch, m, s, A_DK)
    b4 = b.reshape(shape4)
    g4 = (b4 - logf.reshape(shape4))[:, :, 0:1, :]
    e4 = b4[:, :, s - 1:s, :]
    bl4 = e4[:, m - 1:m]
    lc4 = b4 - g4
    kk4 = kk.reshape(shape4)
    q_sub = q.reshape(shape4) * jnp.exp(lc4)
    k_diag = (kk4 * jnp.exp(-lc4)).astype(BF16)
    k_end = kk4 * jnp.exp(e4 - b4)
    q_in = (q_sub * jnp.exp(g4)).astype(BF16)
    k_in = (k_end * jnp.exp(bl4 - e4)).astype(BF16)
    q_sub = q_sub.astype(BF16)
    decay = jnp.exp(bl4)
    k_cross = {(i, j): (k_end[:, j] * jnp.exp(g4[:, i] - e4[:, j])).astype(BF16)
               for i in range(m) for j in range(i - 1)}
    k_end = k_end.astype(BF16)
    vb = v.astype(BF16)
    zeros = jnp.zeros((s, A_DK), BF16)

    o_intra, kv = [], []
    for n in range(nch):
        score_rows = []
        for i in range(m):
            parts = [k_end[n, j] if j == i - 1 else k_cross[(i, j)][n] for j in range(i)]
            parts.append(k_diag[n, i])
            parts.extend([zeros] * (m - 1 - i))
            k_i = jnp.concatenate(parts, axis=0)
            score_rows.append(lax.dot_general(q_sub[n, i], k_i, NT_DIMS,
                                              preferred_element_type=F32))
        scores = jnp.concatenate(score_rows, axis=0)
        scores = jnp.where(tril_mask, scores, 0.0).astype(BF16)
        vb_n = vb[n * c:(n + 1) * c]
        o_intra.append(jnp.dot(scores, vb_n, preferred_element_type=F32))
        kv.append(lax.dot_general(vb_n, k_in[n].reshape(c, A_DK), TN_DIMS,
                                  preferred_element_type=F32))

    states = []
    for n in range(nch):
        states.append(st.astype(BF16))
        st = st * decay[n, 0] + kv[n]
    outs = [o_intra[n] + lax.dot_general(q_in[n].reshape(c, A_DK), states[n], NT_DIMS,
                                         preferred_element_type=F32) for n in range(nch)]
    return jnp.concatenate(outs, axis=0), st


def _even_mixer_kernel(q_ref, f_ref, i_ref, ga_ref, u_ref, v_ref, gb_ref,
                       lbl_ref, onorm_ref, lnw_ref, lnb_ref, ws_ref, bias_ref,
                       oa_ref, ob_ref, st_ref, *, layer_j):
    tb = q_ref.shape[0]

    @pl.when(pl.program_id(2) == 0)
    def _():
        st_ref[...] = jnp.zeros_like(st_ref)

    logits = lbl_ref[...]
    e = jnp.exp(logits - jnp.max(logits, axis=0, keepdims=True))
    lb = jnp.sum(e[:layer_j + 1], axis=0, keepdims=True) / jnp.sum(e, axis=0, keepdims=True)

    o, st = _hgrn2_block(q_ref[...].astype(F32), f_ref[...].astype(F32), i_ref[...].astype(F32),
                         lb, st_ref[...])
    st_ref[...] = st
    oa_ref[...] = (_rms(o, onorm_ref[...]) * _silu(ga_ref[...].astype(F32))).astype(oa_ref.dtype)

    cb = B_CHUNK
    w_causal = jnp.where(lax.broadcasted_iota(jnp.int32, (cb, cb), 1)
                         <= lax.broadcasted_iota(jnp.int32, (cb, cb), 0),
                         ws_ref[...], 0.0).astype(BF16)
    vv = v_ref[...].astype(F32)
    mu = jnp.mean(vv, axis=-1, keepdims=True)
    xc = vv - mu
    vg = (xc * lax.rsqrt(jnp.mean(xc * xc, axis=-1, keepdims=True) + EPS) * lnw_ref[...]
          + lnb_ref[...]).astype(BF16)
    sv = jnp.concatenate([jnp.dot(w_causal, vg[n * cb:(n + 1) * cb], preferred_element_type=F32)
                          for n in range(tb // cb)], axis=0)
    sv = (sv.reshape(tb // cb, cb, B_DIM) + bias_ref[...]).reshape(tb, B_DIM)
    ob_ref[...] = (u_ref[...].astype(F32) * sv * _silu(gb_ref[...].astype(F32))).astype(ob_ref.dtype)


def even_mixers(z, lb_logits, a_onorm, ln_w, ln_b, ws, bias, *, layer_j, tb):
    bsz, t, _ = z.shape
    h = A_HEADS
    n_rows = lb_logits.shape[0]
    lbl = lb_logits.reshape(n_rows, h, A_DK).transpose(1, 0, 2)

    def sec(k):
        return pl.BlockSpec((None, tb, LANES), lambda b, j, s, k=k: (b, s, k * h + j))

    per_head = lambda shape: pl.BlockSpec((None,) + shape, lambda b, j, s: (j, 0, 0))
    out_spec = pl.BlockSpec((None, tb, LANES), lambda b, j, s: (b, s, j))
    out_sds = jax.ShapeDtypeStruct((bsz, t, h * LANES), BF16)
    return pl.pallas_call(
        functools.partial(_even_mixer_kernel, layer_j=layer_j),
        grid=(bsz, h, t // tb),
        in_specs=[sec(k) for k in range(7)] + [
            per_head((n_rows, LANES)),
            pl.BlockSpec((1, LANES), lambda b, j, s: (0, 0)),
            per_head((1, LANES)),
            per_head((1, LANES)),
            per_head((B_CHUNK, B_CHUNK)),
            per_head((B_CHUNK, 1)),
        ],
        out_specs=[out_spec, out_spec],
        out_shape=[out_sds, out_sds],
        scratch_shapes=[pltpu.VMEM((A_DK, A_DK), F32)],
        compiler_params=_params("parallel", "parallel", "arbitrary"),
        name="even_mixers",
    )(z, z, z, z, z, z, z, lbl, a_onorm.reshape(1, LANES),
      ln_w.reshape(B_GROUPS, 1, B_DIM), ln_b.reshape(B_GROUPS, 1, B_DIM),
      ws, bias.reshape(B_GROUPS, B_CHUNK, 1))


def _out_proj_kernel(*refs, n_in, with_next):
    a_refs = refs[:n_in]
    w_refs = refs[n_in:2 * n_in]
    x_ref, post_ref = refs[2 * n_in:2 * n_in + 2]
    rest = refs[2 * n_in + 2:]
    if with_next:
        pre_ref, xo_ref, ho_ref = rest
    else:
        (xo_ref,) = rest
    tm = x_ref.shape[0]
    n_sub = 2 if tm % 32 == 0 else 1
    sub = tm // n_sub

    def project(k):
        rows = slice(k * sub, (k + 1) * sub)
        y = jnp.dot(a_refs[0][rows, :], w_refs[0][...], preferred_element_type=F32)
        for a_ref, w_ref in zip(a_refs[1:], w_refs[1:]):
            y = y + jnp.dot(a_ref[rows, :], w_ref[...], preferred_element_type=F32)
        return y

    def finish(k, y):
        rows = slice(k * sub, (k + 1) * sub)
        x_new = x_ref[rows, :] + _rms(y, post_ref[...])
        xo_ref[rows, :] = x_new
        if with_next:
            ho_ref[rows, :] = _rms(x_new, pre_ref[...]).astype(BF16)

    ys = [project(0)]
    for k in range(n_sub):
        if k + 1 < n_sub:
            ys.append(project(k + 1))
        finish(k, ys[k])


def out_proj(acts, w, x, post_w, next_pre_w, *, tm):
    m, d = x.shape
    n_in = len(acts)
    ka = acts[0].shape[1]
    assert all(a.shape[1] == ka for a in acts) and w.shape == (n_in * ka, d)
    with_next = next_pre_w is not None
    row = lambda width: pl.BlockSpec((tm, width), lambda i: (i, 0))
    const = lambda shape: pl.BlockSpec(shape, lambda i: (0, 0))
    in_specs = ([row(ka) for _ in acts]
                + [pl.BlockSpec((ka, d), lambda i, r=r: (r, 0)) for r in range(n_in)]
                + [row(d), const((1, d))])
    args = list(acts) + [w] * n_in + [x, post_w.reshape(1, d)]
    out_specs = [row(d)]
    out_shape = [jax.ShapeDtypeStruct((m, d), F32)]
    if with_next:
        in_specs.append(const((1, d)))
        args.append(next_pre_w.reshape(1, d))
        out_specs.append(row(d))
        out_shape.append(jax.ShapeDtypeStruct((m, d), BF16))
    return pl.pallas_call(
        functools.partial(_out_proj_kernel, n_in=n_in, with_next=with_next),
        grid=(m // tm,),
        in_specs=in_specs,
        out_specs=out_specs,
        out_shape=out_shape,
        compiler_params=_params("parallel"),
        name="out_proj",
    )(*args)


def _lane_iota():
    return lax.broadcasted_iota(jnp.int32, (1, LANES), 1)


def _join_halves(left, right):
    half = LANES // 2
    return pltpu.roll(jnp.where(_lane_iota() < half, right, left), half, axis=1)


def _odd_in_proj_kernel(h_ref, w_ref, c_ref, pe_ref, gate_ref):
    y = jnp.dot(h_ref[...], w_ref[...], preferred_element_type=F32)
    n_c = c_ref.shape[1]
    c_ref[...] = y[:, :n_c].astype(c_ref.dtype)
    pe_ref[...] = y[:, n_c:n_c + LANES].astype(pe_ref.dtype)
    for j in range(gate_ref.shape[1] // LANES):
        lo = n_c + j * LANES
        tile = _join_halves(y[:, lo:lo + LANES], y[:, lo + LANES:lo + 2 * LANES])
        gate_ref[:, j * LANES:(j + 1) * LANES] = tile.astype(gate_ref.dtype)


def odd_in_proj(h, w, *, tm):
    m, k = h.shape
    widths = [2 * C_RANK, LANES, C_HEADS * C_V]
    return pl.pallas_call(
        _odd_in_proj_kernel,
        grid=(m // tm,),
        in_specs=[pl.BlockSpec((tm, k), lambda i: (i, 0)),
                  pl.BlockSpec(w.shape, lambda i: (0, 0))],
        out_specs=[pl.BlockSpec((tm, wd), lambda i: (i, 0)) for wd in widths],
        out_shape=[jax.ShapeDtypeStruct((m, wd), BF16) for wd in widths],
        compiler_params=_params("parallel"),
        name="odd_in_proj",
    )(h, w)


def _mla_expand_kernel(c_ref, pe_ref, pos_ref, freq_ref, qnorm_ref, kvnorm_ref, wq_ref, wk_ref,
                       wvt_ref, qn_ref, qpe_ref, kn_ref, vt_ref, kpe_ref, *, scale):
    c = c_ref[...].astype(F32)
    cq = _rms(c[:, :C_RANK], qnorm_ref[...]).astype(BF16)
    ckv = _rms(c[:, C_RANK:], kvnorm_ref[...]).astype(BF16)

    q = jnp.dot(cq, wq_ref[...], preferred_element_type=F32)
    kn_ref[...] = jnp.dot(ckv, wk_ref[...], preferred_element_type=F32).astype(kn_ref.dtype)
    vt_ref[...] = lax.dot_general(wvt_ref[...], ckv, NT_DIMS,
                                  preferred_element_type=F32).astype(vt_ref.dtype)

    ang = pos_ref[...].astype(F32) * freq_ref[...]
    cos = jnp.cos(ang)
    sin = jnp.sin(ang)
    lane = _lane_iota()
    first_half = lane < C_ROPE
    low_rot = (lane % C_ROPE) < C_ROPE // 2

    def rope(x):
        rot = jnp.where(low_rot, -pltpu.roll(x, LANES - C_ROPE // 2, axis=1),
                        pltpu.roll(x, C_ROPE // 2, axis=1))
        return x * cos + rot * sin

    for p in range(C_HEADS // 2):
        a, b, c3 = (q[:, (3 * p + k) * LANES:(3 * p + k + 1) * LANES] for k in range(3))
        qn_ref[:, 2 * p * LANES:(2 * p + 1) * LANES] = (a * scale).astype(qn_ref.dtype)
        qn_ref[:, (2 * p + 1) * LANES:(2 * p + 2) * LANES] = (
            _join_halves(b, c3) * scale).astype(qn_ref.dtype)
        pe_pair = jnp.where(first_half, b, c3)
        qpe_ref[:, p * LANES:(p + 1) * LANES] = (rope(pe_pair) * scale).astype(qpe_ref.dtype)

    k_pe = rope(pe_ref[...].astype(F32))
    kpe_ref[...] = jnp.where(first_half, k_pe,
                             pltpu.roll(k_pe, C_ROPE, axis=1)).astype(kpe_ref.dtype)


def mla_expand(c, pe, pos, freq, q_norm, kv_norm, wq, wk, wvt, *, bsz, tm, scale):
    m = c.shape[0]
    t = m // bsz
    nb = t // tm
    row = lambda width: pl.BlockSpec((tm, width), lambda i: (i, 0))
    const = lambda shape: pl.BlockSpec(shape, lambda i: (0, 0))
    n_nope = C_HEADS * C_NOPE
    n_v = C_HEADS * C_V
    widths = [n_nope, C_HEADS * C_ROPE, n_nope, None, LANES]
    out_specs = [row(wd) if wd else pl.BlockSpec((None, n_v, tm), lambda i: (i // nb, 0, i % nb))
                 for wd in widths]
    out_shape = [jax.ShapeDtypeStruct((m, wd) if wd else (bsz, n_v, t), BF16) for wd in widths]
    return pl.pallas_call(
        functools.partial(_mla_expand_kernel, scale=scale),
        grid=(m // tm,),
        in_specs=[row(c.shape[1]), row(pe.shape[1]), row(1), const((1, LANES)),
                  const((1, C_RANK)), const((1, C_RANK)), const(wq.shape), const(wk.shape),
                  const(wvt.shape)],
        out_specs=out_specs,
        out_shape=out_shape,
        compiler_params=_params("parallel"),
        name="mla_expand",
    )(c, pe, pos, freq, q_norm.reshape(1, C_RANK), kv_norm.reshape(1, C_RANK), wq, wk, wvt)


def _attn_kernel(qn_ref, qpe_ref, kn_ref, kpe_ref, vt_ref, g_ref, o_ref, k_scr, *, tq):
    t = qn_ref.shape[0]
    heads = (0, 1)
    for hh in heads:
        k_scr[hh, :, :LANES] = kn_ref[:, hh * LANES:(hh + 1) * LANES]
        k_scr[hh, :, LANES:] = kpe_ref[...]

    lane = lax.broadcasted_iota(jnp.int32, (1, LANES), 1)
    causal = (lax.broadcasted_iota(jnp.int32, (tq, tq), 0)
              <= lax.broadcasted_iota(jnp.int32, (tq, tq), 1))
    neg = jnp.finfo(F32).min

    def scores(qi, hh):
        r0 = qi * tq
        end = r0 + tq
        q_pe = qpe_ref[r0:end, :]
        q_pe = jnp.where((lane // C_ROPE) == hh, q_pe, jnp.zeros_like(q_pe))
        q = jnp.concatenate([qn_ref[r0:end, hh * LANES:(hh + 1) * LANES], q_pe], axis=1)
        part = end // QK_SPLIT
        return jnp.concatenate(
            [lax.dot_general(k_scr[hh, j * part:(j + 1) * part, :], q, NT_DIMS,
                             preferred_element_type=F32) for j in range(QK_SPLIT)],
            axis=0)

    def col_max(qi, s):
        r0 = qi * tq
        s_d = jnp.where(causal, s[r0:], neg)
        m = jnp.max(s_d, axis=0, keepdims=True)
        if qi > 0:
            m = jnp.maximum(m, jnp.max(s[:r0], axis=0, keepdims=True))
        return s_d, m

    def probs(qi, s, s_d, m):
        r0 = qi * tq
        p_d = jnp.exp2(s_d - m)
        l = jnp.sum(p_d, axis=0, keepdims=True)
        p_o = None
        if qi > 0:
            p_o = jnp.exp2(s[:r0] - m)
            l = l + jnp.sum(p_o, axis=0, keepdims=True)
            p_o = p_o.astype(BF16)
        return p_d.astype(BF16), p_o, l

    def values(qi, hh, p_d, p_o, l):
        r0 = qi * tq
        end = r0 + tq
        vt = vt_ref.at[hh * C_V:(hh + 1) * C_V]
        acc = jnp.dot(vt[:, r0:end], p_d, preferred_element_type=F32)
        if qi > 0:
            acc = acc + jnp.dot(vt[:, :r0], p_o, preferred_element_type=F32)
        o = (acc * (1.0 / l)).T
        cols = slice(hh * C_V, (hh + 1) * C_V)
        o_ref[r0:end, cols] = (o * _silu(g_ref[r0:end, cols].astype(F32))).astype(o_ref.dtype)

    nq = t // tq
    s = [scores(0, hh) for hh in heads]
    sm = [col_max(0, s[hh]) for hh in heads]
    for qi in range(nq):
        last = qi + 1 == nq
        s_next = None if last else [scores(qi + 1, hh) for hh in heads]
        pr = [probs(qi, s[hh], *sm[hh]) for hh in heads]
        for hh in heads:
            values(qi, hh, *pr[hh])
        if not last:
            s = s_next
            sm = [col_max(qi + 1, s[hh]) for hh in heads]


def attention(qn, qpe, kn, kpe, vt, gate, *, tq):
    bsz, t, _ = qn.shape
    pair = lambda width: pl.BlockSpec((None, t, width), lambda b, p: (b, 0, p))
    shared = pl.BlockSpec((None, t, LANES), lambda b, p: (b, 0, 0))
    pair_t = pl.BlockSpec((None, 2 * C_V, t), lambda b, p: (b, p, 0))
    return pl.pallas_call(
        functools.partial(_attn_kernel, tq=tq),
        grid=(bsz, C_HEADS // 2),
        in_specs=[pair(2 * LANES), pair(LANES), pair(2 * LANES), shared, pair_t, pair(2 * LANES)],
        out_specs=pair(2 * LANES),
        out_shape=jax.ShapeDtypeStruct((bsz, t, C_HEADS * C_V), BF16),
        scratch_shapes=[pltpu.VMEM((2, t, 2 * LANES), BF16)],
        compiler_params=_params("parallel", "parallel"),
        name="mla_attention",
    )(qn, qpe, kn, kpe, vt, gate)


def _odd_weights(w_in, w_qb, w_kvb):
    n_in = w_in.shape[1]
    n_pad = -(-(n_in + LANES // 2) // (2 * LANES)) * (2 * LANES) - n_in
    w_in_b = jnp.pad(w_in, ((0, 0), (0, n_pad))).astype(BF16)
    kvb = w_kvb.astype(BF16).reshape(C_RANK, C_HEADS, C_NOPE + C_V)
    wk = kvb[:, :, :C_NOPE].reshape(C_RANK, -1)
    wvt = kvb[:, :, C_NOPE:].reshape(C_RANK, -1).T
    return w_in_b, w_qb.astype(BF16), wk, wvt


def kernel(x, positions, norm_pre, norm_post, ev_w_in, ev_lb_logits, ev_a_onorm, ev_b_ln_w,
           ev_b_ln_b, ev_b_ws, ev_b_bias, ev_w_out, od_w_in, od_q_norm, od_w_qb, od_kv_norm,
           od_w_kvb, od_w_out):
    bsz, t, d = x.shape
    m = bsz * t
    xf = x.reshape(m, d)
    tm_big = min(1024, m)
    tm = min(512, m)

    z = rms_matmul(xf, norm_pre[0].reshape(1, d), ev_w_in[0].astype(BF16), tm=tm_big, tn=1792)
    mix_a, mix_b = even_mixers(z.reshape(bsz, t, -1), ev_lb_logits, ev_a_onorm[0], ev_b_ln_w[0],
                               ev_b_ln_b[0], ev_b_ws[0], ev_b_bias[0], layer_j=0, tb=min(512, t))
    x1, h1 = out_proj([mix_a.reshape(m, -1), mix_b.reshape(m, -1)], ev_w_out[0].astype(BF16),
                      xf, norm_post[0], norm_pre[1], tm=tm)

    w_in_b, wq_all, wk, wvt = _odd_weights(od_w_in[0], od_w_qb[0], od_w_kvb[0])
    c, pe, gate = odd_in_proj(h1, w_in_b, tm=tm)
    inv_freq = ROPE_THETA ** (-jnp.arange(0, C_ROPE, 2, dtype=F32) / C_ROPE)
    freq = jnp.tile(inv_freq, LANES // (C_ROPE // 2)).reshape(1, LANES)
    scale = float((C_NOPE + C_ROPE) ** -0.5 * LOG2E)
    qn, qpe, kn, vt, kpe = mla_expand(c, pe, positions.reshape(m, 1), freq, od_q_norm[0],
                                      od_kv_norm[0], wq_all, wk, wvt, bsz=bsz, tm=min(tm, t),
                                      scale=scale)
    r3 = lambda a: a.reshape(bsz, t, -1)
    o = attention(r3(qn), r3(qpe), r3(kn), r3(kpe), vt, r3(gate), tq=min(256, t))
    (x2,) = out_proj([o.reshape(m, -1)], od_w_out[0].astype(BF16), x1, norm_post[1], None, tm=tm)
    return x2.reshape(bsz, t, d)
```

```python
import functools

import jax
import jax.numpy as jnp
from jax import lax
from jax.experimental import pallas as pl
from jax.experimental.pallas import tpu as pltpu

F32 = jnp.float32
BF16 = jnp.bfloat16

EPS = 1e-6
LANES = 128
VMEM_LIMIT_BYTES = 56 * 1024 * 1024

A_HEADS = 8
A_DK = 128
A_CHUNK = 64
A_SUB = 16
B_GROUPS = 8
B_DIM = 128
B_CHUNK = 128
C_HEADS = 16
C_RANK = 512
C_NOPE = 128
C_ROPE = 64
C_V = 128
ROPE_THETA = 10000.0
LOG2E = 1.4426950408889634
QK_SPLIT = 1
ATTN_HEADS_PER_STEP = 2

NT_DIMS = (((1,), (1,)), ((), ()))
TN_DIMS = (((0,), (0,)), ((), ()))


def _params(*sem):
    return pltpu.CompilerParams(dimension_semantics=sem, vmem_limit_bytes=VMEM_LIMIT_BYTES)


def _rms(x, w):
    ms = jnp.mean(x * x, axis=-1, keepdims=True)
    return x * lax.rsqrt(ms + EPS) * w


def _silu(x):
    return x * (1.0 / (1.0 + jnp.exp(-x)))


def _rms_matmul_kernel(x_ref, g_ref, w_ref, o_ref, h_ref):
    @pl.when(pl.program_id(1) == 0)
    def _():
        h_ref[...] = _rms(x_ref[...], g_ref[...]).astype(BF16)

    o_ref[...] = jnp.dot(h_ref[...], w_ref[...], preferred_element_type=F32).astype(o_ref.dtype)


def rms_matmul(x, g, w, *, tm, tn):
    m, k = x.shape
    n = w.shape[1]
    return pl.pallas_call(
        _rms_matmul_kernel,
        grid=(m // tm, n // tn),
        in_specs=[
            pl.BlockSpec((tm, k), lambda i, j: (i, 0)),
            pl.BlockSpec((1, k), lambda i, j: (0, 0)),
            pl.BlockSpec((k, tn), lambda i, j: (0, j)),
        ],
        out_specs=pl.BlockSpec((tm, tn), lambda i, j: (i, j)),
        out_shape=jax.ShapeDtypeStruct((m, n), BF16),
        scratch_shapes=[pltpu.VMEM((tm, k), BF16)],
        compiler_params=_params("parallel", "arbitrary"),
        name="rms_matmul",
    )(x, g, w)


def _hgrn2_block(q, zf, v, lb, st):
    c, s = A_CHUNK, A_SUB
    m = c // s
    tb = q.shape[0]
    nch = tb // c
    tril_mask = (lax.broadcasted_iota(jnp.int32, (c, c), 1)
                 <= lax.broadcasted_iota(jnp.int32, (c, c), 0))
    tril = jnp.where(tril_mask, 1.0, 0.0).astype(BF16)

    a = jnp.abs(zf)
    t = jnp.exp(-a)
    r = 1.0 / (1.0 + t)
    tr = t * r
    pos = zf >= 0
    sig = jnp.where(pos, r, tr)
    sig_neg = jnp.where(pos, tr, r)
    logf = jnp.log(lb + (1.0 - lb) * sig)
    kk = (1.0 - lb) * sig_neg

    hi = logf.astype(BF16)
    r1 = logf - hi.astype(F32)
    mid = r1.astype(BF16)
    lo = (r1 - mid.astype(F32)).astype(BF16)
    terms = jnp.concatenate([hi, mid, lo], axis=1)
    b_chunks = []
    for n in range(nch):
        y = jnp.dot(tril, terms[n * c:(n + 1) * c], preferred_element_type=F32)
        b_chunks.append((y[:, :A_DK] + y[:, A_DK:2 * A_DK]) + y[:, 2 * A_DK:])
    b = jnp.concatenate(b_chunks, axis=0)

    shape4 = (nch, m, s, A_DK)
    b4 = b.reshape(shape4)
    g4 = (b4 - logf.reshape(shape4))[:, :, 0:1, :]
    e4 = b4[:, :, s - 1:s, :]
    bl4 = e4[:, m - 1:m]
    lc4 = b4 - g4
    kk4 = kk.reshape(shape4)
    q_sub = q.reshape(shape4) * jnp.exp(lc4)
    k_diag = (kk4 * jnp.exp(-lc4)).astype(BF16)
    k_end = kk4 * jnp.exp(e4 - b4)
    q_in = (q_sub * jnp.exp(g4)).astype(BF16)
    k_in = (k_end * jnp.exp(bl4 - e4)).astype(BF16)
    q_sub = q_sub.astype(BF16)
    decay = jnp.exp(bl4)
    k_cross = {(i, j): (k_end[:, j] * jnp.exp(g4[:, i] - e4[:, j])).astype(BF16)
               for i in range(m) for j in range(i - 1)}
    k_end = k_end.astype(BF16)
    vb = v.astype(BF16)
    zeros = jnp.zeros((s, A_DK), BF16)

    o_intra, kv = [], []
    for n in range(nch):
        score_rows = []
        for i in range(m):
            parts = [k_end[n, j] if j == i - 1 else k_cross[(i, j)][n] for j in range(i)]
            parts.append(k_diag[n, i])
            parts.extend([zeros] * (m - 1 - i))
            k_i = jnp.concatenate(parts, axis=0)
            score_rows.append(lax.dot_general(q_sub[n, i], k_i, NT_DIMS,
                                              preferred_element_type=F32))
        scores = jnp.concatenate(score_rows, axis=0)
        scores = jnp.where(tril_mask, scores, 0.0).astype(BF16)
        vb_n = vb[n * c:(n + 1) * c]
        o_intra.append(jnp.dot(scores, vb_n, preferred_element_type=F32))
        kv.append(lax.dot_general(vb_n, k_in[n].reshape(c, A_DK), TN_DIMS,
                                  preferred_element_type=F32))

    states = []
    for n in range(nch):
        states.append(st.astype(BF16))
        st = st * decay[n, 0] + kv[n]
    outs = [o_intra[n] + lax.dot_general(q_in[n].reshape(c, A_DK), states[n], NT_DIMS,
                                         preferred_element_type=F32) for n in range(nch)]
    return jnp.concatenate(outs, axis=0), st


def _even_mixer_kernel(q_ref, f_ref, i_ref, ga_ref, u_ref, v_ref, gb_ref,
                       lbl_ref, onorm_ref, lnw_ref, lnb_ref, ws_ref, bias_ref,
                       oa_ref, ob_ref, st_ref, *, layer_j):
    tb = q_ref.shape[0]

    @pl.when(pl.program_id(2) == 0)
    def _():
        st_ref[...] = jnp.zeros_like(st_ref)

    logits = lbl_ref[...]
    e = jnp.exp(logits - jnp.max(logits, axis=0, keepdims=True))
    lb = jnp.sum(e[:layer_j + 1], axis=0, keepdims=True) / jnp.sum(e, axis=0, keepdims=True)

    o, st = _hgrn2_block(q_ref[...].astype(F32), f_ref[...].astype(F32), i_ref[...].astype(F32),
                         lb, st_ref[...])
    st_ref[...] = st
    oa_ref[...] = (_rms(o, onorm_ref[...]) * _silu(ga_ref[...].astype(F32))).astype(oa_ref.dtype)

    cb = B_CHUNK
    w_causal = jnp.where(lax.broadcasted_iota(jnp.int32, (cb, cb), 1)
                         <= lax.broadcasted_iota(jnp.int32, (cb, cb), 0),
                         ws_ref[...], 0.0).astype(BF16)
    vv = v_ref[...].astype(F32)
    mu = jnp.mean(vv, axis=-1, keepdims=True)
    xc = vv - mu
    vg = (xc * lax.rsqrt(jnp.mean(xc * xc, axis=-1, keepdims=True) + EPS) * lnw_ref[...]
          + lnb_ref[...]).astype(BF16)
    sv = jnp.concatenate([jnp.dot(w_causal, vg[n * cb:(n + 1) * cb], preferred_element_type=F32)
                          for n in range(tb // cb)], axis=0)
    sv = (sv.reshape(tb // cb, cb, B_DIM) + bias_ref[...]).reshape(tb, B_DIM)
    ob_ref[...] = (u_ref[...].astype(F32) * sv * _silu(gb_ref[...].astype(F32))).astype(ob_ref.dtype)


def even_mixers(z, lb_logits, a_onorm, ln_w, ln_b, ws, bias, *, layer_j, tb):
    bsz, t, _ = z.shape
    h = A_HEADS
    n_rows = lb_logits.shape[0]
    lbl = lb_logits.reshape(n_rows, h, A_DK).transpose(1, 0, 2)

    def sec(k):
        return pl.BlockSpec((None, tb, LANES), lambda b, j, s, k=k: (b, s, k * h + j))

    per_head = lambda shape: pl.BlockSpec((None,) + shape, lambda b, j, s: (j, 0, 0))
    out_spec = pl.BlockSpec((None, tb, LANES), lambda b, j, s: (b, s, j))
    out_sds = jax.ShapeDtypeStruct((bsz, t, h * LANES), BF16)
    return pl.pallas_call(
        functools.partial(_even_mixer_kernel, layer_j=layer_j),
        grid=(bsz, h, t // tb),
        in_specs=[sec(k) for k in range(7)] + [
            per_head((n_rows, LANES)),
            pl.BlockSpec((1, LANES), lambda b, j, s: (0, 0)),
            per_head((1, LANES)),
            per_head((1, LANES)),
            per_head((B_CHUNK, B_CHUNK)),
            per_head((B_CHUNK, 1)),
        ],
        out_specs=[out_spec, out_spec],
        out_shape=[out_sds, out_sds],
        scratch_shapes=[pltpu.VMEM((A_DK, A_DK), F32)],
        compiler_params=_params("parallel", "parallel", "arbitrary"),
        name="even_mixers",
    )(z, z, z, z, z, z, z, lbl, a_onorm.reshape(1, LANES),
      ln_w.reshape(B_GROUPS, 1, B_DIM), ln_b.reshape(B_GROUPS, 1, B_DIM),
      ws, bias.reshape(B_GROUPS, B_CHUNK, 1))


def _out_proj_kernel(*refs, n_in, with_next):
    a_refs = refs[:n_in]
    w_refs = refs[n_in:2 * n_in]
    x_ref, post_ref = refs[2 * n_in:2 * n_in + 2]
    rest = refs[2 * n_in + 2:]
    if with_next:
        pre_ref, xo_ref, ho_ref = rest
    else:
        (xo_ref,) = rest
    tm = x_ref.shape[0]
    n_sub = 2 if tm % 32 == 0 else 1
    sub = tm // n_sub

    def project(k):
        rows = slice(k * sub, (k + 1) * sub)
        y = jnp.dot(a_refs[0][rows, :], w_refs[0][...], preferred_element_type=F32)
        for a_ref, w_ref in zip(a_refs[1:], w_refs[1:]):
            y = y + jnp.dot(a_ref[rows, :], w_ref[...], preferred_element_type=F32)
        return y

    def finish(k, y):
        rows = slice(k * sub, (k + 1) * sub)
        x_new = x_ref[rows, :] + _rms(y, post_ref[...])
        xo_ref[rows, :] = x_new
        if with_next:
            ho_ref[rows, :] = _rms(x_new, pre_ref[...]).astype(BF16)

    ys = [project(0)]
    for k in range(n_sub):
        if k + 1 < n_sub:
            ys.append(project(k + 1))
        finish(k, ys[k])


def out_proj(acts, w, x, post_w, next_pre_w, *, tm):
    m, d = x.shape
    n_in = len(acts)
    ka = acts[0].shape[1]
    assert all(a.shape[1] == ka for a in acts) and w.shape == (n_in * ka, d)
    with_next = next_pre_w is not None
    row = lambda width: pl.BlockSpec((tm, width), lambda i: (i, 0))
    const = lambda shape: pl.BlockSpec(shape, lambda i: (0, 0))
    in_specs = ([row(ka) for _ in acts]
                + [pl.BlockSpec((ka, d), lambda i, r=r: (r, 0)) for r in range(n_in)]
                + [row(d), const((1, d))])
    args = list(acts) + [w] * n_in + [x, post_w.reshape(1, d)]
    out_specs = [row(d)]
    out_shape = [jax.ShapeDtypeStruct((m, d), F32)]
    if with_next:
        in_specs.append(const((1, d)))
        args.append(next_pre_w.reshape(1, d))
        out_specs.append(row(d))
        out_shape.append(jax.ShapeDtypeStruct((m, d), BF16))
    return pl.pallas_call(
        functools.partial(_out_proj_kernel, n_in=n_in, with_next=with_next),
        grid=(m // tm,),
        in_specs=in_specs,
        out_specs=out_specs,
        out_shape=out_shape,
        compiler_params=_params("parallel"),
        name="out_proj",
    )(*args)


def _lane_iota():
    return lax.broadcasted_iota(jnp.int32, (1, LANES), 1)


def _join_halves(left, right):
    half = LANES // 2
    return pltpu.roll(jnp.where(_lane_iota() < half, right, left), half, axis=1)


def _odd_in_proj_kernel(h_ref, w_ref, c_ref, pe_ref, gate_ref):
    y = jnp.dot(h_ref[...], w_ref[...], preferred_element_type=F32)
    n_c = c_ref.shape[1]
    c_ref[...] = y[:, :n_c].astype(c_ref.dtype)
    pe_ref[...] = y[:, n_c:n_c + LANES].astype(pe_ref.dtype)
    for j in range(gate_ref.shape[1] // LANES):
        lo = n_c + j * LANES
        tile = _join_halves(y[:, lo:lo + LANES], y[:, lo + LANES:lo + 2 * LANES])
        gate_ref[:, j * LANES:(j + 1) * LANES] = tile.astype(gate_ref.dtype)


def odd_in_proj(h, w, *, tm):
    m, k = h.shape
    widths = [2 * C_RANK, LANES, C_HEADS * C_V]
    return pl.pallas_call(
        _odd_in_proj_kernel,
        grid=(m // tm,),
        in_specs=[pl.BlockSpec((tm, k), lambda i: (i, 0)),
                  pl.BlockSpec(w.shape, lambda i: (0, 0))],
        out_specs=[pl.BlockSpec((tm, wd), lambda i: (i, 0)) for wd in widths],
        out_shape=[jax.ShapeDtypeStruct((m, wd), BF16) for wd in widths],
        compiler_params=_params("parallel"),
        name="odd_in_proj",
    )(h, w)


def _mla_expand_kernel(c_ref, pe_ref, pos_ref, freq_ref, qnorm_ref, kvnorm_ref, wq_ref, wk_ref,
                       wvt_ref, qn_ref, qpe_ref, kn_ref, vt_ref, kpe_ref, *, scale):
    c = c_ref[...].astype(F32)
    cq = _rms(c[:, :C_RANK], qnorm_ref[...]).astype(BF16)
    ckv = _rms(c[:, C_RANK:], kvnorm_ref[...]).astype(BF16)

    q = jnp.dot(cq, wq_ref[...], preferred_element_type=F32)
    kn_ref[...] = jnp.dot(ckv, wk_ref[...], preferred_element_type=F32).astype(kn_ref.dtype)
    vt_ref[...] = lax.dot_general(wvt_ref[...], ckv, NT_DIMS,
                                  preferred_element_type=F32).astype(vt_ref.dtype)

    ang = pos_ref[...].astype(F32) * freq_ref[...]
    cos = jnp.cos(ang)
    sin = jnp.sin(ang)
    lane = _lane_iota()
    first_half = lane < C_ROPE
    low_rot = (lane % C_ROPE) < C_ROPE // 2

    def rope(x):
        rot = jnp.where(low_rot, -pltpu.roll(x, LANES - C_ROPE // 2, axis=1),
                        pltpu.roll(x, C_ROPE // 2, axis=1))
        return x * cos + rot * sin

    for p in range(C_HEADS // 2):
        a, b, c3 = (q[:, (3 * p + k) * LANES:(3 * p + k + 1) * LANES] for k in range(3))
        qn_ref[:, 2 * p * LANES:(2 * p + 1) * LANES] = (a * scale).astype(qn_ref.dtype)
        qn_ref[:, (2 * p + 1) * LANES:(2 * p + 2) * LANES] = (
            _join_halves(b, c3) * scale).astype(qn_ref.dtype)
        pe_pair = jnp.where(first_half, b, c3)
        qpe_ref[:, p * LANES:(p + 1) * LANES] = (rope(pe_pair) * scale).astype(qpe_ref.dtype)

    k_pe = rope(pe_ref[...].astype(F32))
    kpe_ref[...] = jnp.where(first_half, k_pe,
                             pltpu.roll(k_pe, C_ROPE, axis=1)).astype(kpe_ref.dtype)


def mla_expand(c, pe, pos, freq, q_norm, kv_norm, wq, wk, wvt, *, bsz, tm, scale):
    m = c.shape[0]
    t = m // bsz
    nb = t // tm
    row = lambda width: pl.BlockSpec((tm, width), lambda i: (i, 0))
    const = lambda shape: pl.BlockSpec(shape, lambda i: (0, 0))
    n_nope = C_HEADS * C_NOPE
    n_v = C_HEADS * C_V
    widths = [n_nope, C_HEADS * C_ROPE, n_nope, None, LANES]
    out_specs = [row(wd) if wd else pl.BlockSpec((None, n_v, tm), lambda i: (i // nb, 0, i % nb))
                 for wd in widths]
    out_shape = [jax.ShapeDtypeStruct((m, wd) if wd else (bsz, n_v, t), BF16) for wd in widths]
    return pl.pallas_call(
        functools.partial(_mla_expand_kernel, scale=scale),
        grid=(m // tm,),
        in_specs=[row(c.shape[1]), row(pe.shape[1]), row(1), const((1, LANES)),
                  const((1, C_RANK)), const((1, C_RANK)), const(wq.shape), const(wk.shape),
                  const(wvt.shape)],
        out_specs=out_specs,
        out_shape=out_shape,
        compiler_params=_params("parallel"),
        name="mla_expand",
    )(c, pe, pos, freq, q_norm.reshape(1, C_RANK), kv_norm.reshape(1, C_RANK), wq, wk, wvt)


def _attn_kernel(qn_ref, qpe_ref, kn_ref, kpe_ref, vt_ref, g_ref, o_ref, k_scr, *, tq):
    t = qn_ref.shape[0]
    heads = tuple(range(qn_ref.shape[1] // LANES))
    for hh in heads:
        k_scr[hh, :, :LANES] = kn_ref[:, hh * LANES:(hh + 1) * LANES]
        k_scr[hh, :, LANES:] = kpe_ref[...]

    lane = lax.broadcasted_iota(jnp.int32, (1, LANES), 1)
    causal = (lax.broadcasted_iota(jnp.int32, (tq, tq), 0)
              <= lax.broadcasted_iota(jnp.int32, (tq, tq), 1))
    neg = jnp.finfo(F32).min

    def scores(qi, hh):
        r0 = qi * tq
        end = r0 + tq
        q_pe = qpe_ref[r0:end, (hh // 2) * LANES:(hh // 2 + 1) * LANES]
        q_pe = jnp.where((lane // C_ROPE) == hh % 2, q_pe, jnp.zeros_like(q_pe))
        q = jnp.concatenate([qn_ref[r0:end, hh * LANES:(hh + 1) * LANES], q_pe], axis=1)
        part = end // QK_SPLIT
        return jnp.concatenate(
            [lax.dot_general(k_scr[hh, j * part:(j + 1) * part, :], q, NT_DIMS,
                             preferred_element_type=F32) for j in range(QK_SPLIT)],
            axis=0)

    def col_max(qi, s):
        r0 = qi * tq
        s_d = jnp.where(causal, s[r0:], neg)
        m = jnp.max(s_d, axis=0, keepdims=True)
        if qi > 0:
            m = jnp.maximum(m, jnp.max(s[:r0], axis=0, keepdims=True))
        return s_d, m

    def probs(qi, s, s_d, m):
        r0 = qi * tq
        p_d = jnp.exp2(s_d - m)
        l = jnp.sum(p_d, axis=0, keepdims=True)
        p_o = None
        if qi > 0:
            p_o = jnp.exp2(s[:r0] - m)
            l = l + jnp.sum(p_o, axis=0, keepdims=True)
            p_o = p_o.astype(BF16)
        return p_d.astype(BF16), p_o, l

    def values(qi, hh, p_d, p_o, l):
        r0 = qi * tq
        end = r0 + tq
        vt = vt_ref.at[hh * C_V:(hh + 1) * C_V]
        acc = jnp.dot(vt[:, r0:end], p_d, preferred_element_type=F32)
        if qi > 0:
            acc = acc + jnp.dot(vt[:, :r0], p_o, preferred_element_type=F32)
        o = (acc * (1.0 / l)).T
        cols = slice(hh * C_V, (hh + 1) * C_V)
        o_ref[r0:end, cols] = (o * _silu(g_ref[r0:end, cols].astype(F32))).astype(o_ref.dtype)

    nq = t // tq
    s = [scores(0, hh) for hh in heads]
    sm = [col_max(0, s[hh]) for hh in heads]
    for qi in range(nq):
        last = qi + 1 == nq
        s_next = None if last else [scores(qi + 1, hh) for hh in heads]
        pr = [probs(qi, s[hh], *sm[hh]) for hh in heads]
        for hh in heads:
            values(qi, hh, *pr[hh])
        if not last:
            s = s_next
            sm = [col_max(qi + 1, s[hh]) for hh in heads]


def attention(qn, qpe, kn, kpe, vt, gate, *, tq):
    bsz, t, _ = qn.shape
    g = ATTN_HEADS_PER_STEP
    group = lambda width: pl.BlockSpec((None, t, width), lambda b, p: (b, 0, p))
    shared = pl.BlockSpec((None, t, LANES), lambda b, p: (b, 0, 0))
    group_t = pl.BlockSpec((None, g * C_V, t), lambda b, p: (b, p, 0))
    return pl.pallas_call(
        functools.partial(_attn_kernel, tq=tq),
        grid=(bsz, C_HEADS // g),
        in_specs=[group(g * LANES), group(g * C_ROPE), group(g * LANES), shared, group_t,
                  group(g * C_V)],
        out_specs=group(g * C_V),
        out_shape=jax.ShapeDtypeStruct((bsz, t, C_HEADS * C_V), BF16),
        scratch_shapes=[pltpu.VMEM((g, t, 2 * LANES), BF16)],
        compiler_params=_params("parallel", "parallel"),
        name="mla_attention",
    )(qn, qpe, kn, kpe, vt, gate)


def _cast_pad_kernel(w_ref, o_ref):
    n = w_ref.shape[1]
    o_ref[:, :n] = w_ref[...].astype(o_ref.dtype)
    o_ref[:, n:] = jnp.zeros((o_ref.shape[0], o_ref.shape[1] - n), o_ref.dtype)


def cast_pad(w, n_out, *, tr):
    r, n = w.shape
    return pl.pallas_call(
        _cast_pad_kernel,
        grid=(r // tr,),
        in_specs=[pl.BlockSpec((tr, n), lambda i: (i, 0))],
        out_specs=pl.BlockSpec((tr, n_out), lambda i: (i, 0)),
        out_shape=jax.ShapeDtypeStruct((r, n_out), BF16),
        compiler_params=_params("parallel"),
        name="cast_pad",
    )(w)


def _odd_weights(w_in, w_qb, w_kvb):
    n_in = w_in.shape[1]
    n_out = -(-(n_in + LANES // 2) // (2 * LANES)) * (2 * LANES)
    w_in_b = cast_pad(w_in, n_out, tr=256)
    kvb = w_kvb.astype(BF16).reshape(C_RANK, C_HEADS, C_NOPE + C_V)
    wk = kvb[:, :, :C_NOPE].reshape(C_RANK, -1)
    wvt = kvb[:, :, C_NOPE:].reshape(C_RANK, -1).T
    return w_in_b, w_qb.astype(BF16), wk, wvt


def kernel(x, positions, norm_pre, norm_post, ev_w_in, ev_lb_logits, ev_a_onorm, ev_b_ln_w,
           ev_b_ln_b, ev_b_ws, ev_b_bias, ev_w_out, od_w_in, od_q_norm, od_w_qb, od_kv_norm,
           od_w_kvb, od_w_out):
    bsz, t, d = x.shape
    m = bsz * t
    xf = x.reshape(m, d)
    tm_big = min(1024, m)
    tm = min(512, m)

    z = rms_matmul(xf, norm_pre[0].reshape(1, d), ev_w_in[0].astype(BF16), tm=tm_big, tn=1792)
    mix_a, mix_b = even_mixers(z.reshape(bsz, t, -1), ev_lb_logits, ev_a_onorm[0], ev_b_ln_w[0],
                               ev_b_ln_b[0], ev_b_ws[0], ev_b_bias[0], layer_j=0, tb=min(1024, t))
    x1, h1 = out_proj([mix_a.reshape(m, -1), mix_b.reshape(m, -1)], ev_w_out[0].astype(BF16),
                      xf, norm_post[0], norm_pre[1], tm=tm)

    w_in_b, wq_all, wk, wvt = _odd_weights(od_w_in[0], od_w_qb[0], od_w_kvb[0])
    c, pe, gate = odd_in_proj(h1, w_in_b, tm=tm)
    inv_freq = ROPE_THETA ** (-jnp.arange(0, C_ROPE, 2, dtype=F32) / C_ROPE)
    freq = jnp.tile(inv_freq, LANES // (C_ROPE // 2)).reshape(1, LANES)
    scale = float((C_NOPE + C_ROPE) ** -0.5 * LOG2E)
    qn, qpe, kn, vt, kpe = mla_expand(c, pe, positions.reshape(m, 1), freq, od_q_norm[0],
                                      od_kv_norm[0], wq_all, wk, wvt, bsz=bsz, tm=min(tm, t),
                                      scale=scale)
    r3 = lambda a: a.reshape(bsz, t, -1)
    o = attention(r3(qn), r3(qpe), r3(kn), r3(kpe), vt, r3(gate), tq=min(256, t))
    (x2,) = out_proj([o.reshape(m, -1)], od_w_out[0].astype(BF16), x1, norm_post[1], None, tm=tm)
    return x2.reshape(bsz, t, d)
```

```python
import functools

import jax
import jax.numpy as jnp
from jax import lax
from jax.experimental import pallas as pl
from jax.experimental.pallas import tpu as pltpu

F32 = jnp.float32
BF16 = jnp.bfloat16

EPS = 1e-6
LANES = 128
VMEM_LIMIT_BYTES = 56 * 1024 * 1024

A_HEADS = 8
A_DK = 128
A_CHUNK = 64
A_SUB = 16
B_GROUPS = 8
B_DIM = 128
B_CHUNK = 128
C_HEADS = 16
C_RANK = 512
C_NOPE = 128
C_ROPE = 64
C_V = 128
ROPE_THETA = 10000.0
LOG2E = 1.4426950408889634
QK_SPLIT = 1
ATTN_HEADS_PER_STEP = 2

NT_DIMS = (((1,), (1,)), ((), ()))
TN_DIMS = (((0,), (0,)), ((), ()))


def _params(*sem):
    return pltpu.CompilerParams(dimension_semantics=sem, vmem_limit_bytes=VMEM_LIMIT_BYTES)


def _rms(x, w):
    ms = jnp.mean(x * x, axis=-1, keepdims=True)
    return x * lax.rsqrt(ms + EPS) * w


def _silu(x):
    return x * (1.0 / (1.0 + jnp.exp(-x)))


def _column_tile(row):
    return jnp.broadcast_to(row, (LANES, LANES)).T


def _rms_matmul_kernel(x_ref, g_ref, w_ref, o_ref, h_ref):
    @pl.when(pl.program_id(1) == 0)
    def _():
        h_ref[...] = _rms(x_ref[...], g_ref[...]).astype(BF16)

    o_ref[...] = jnp.dot(h_ref[...], w_ref[...], preferred_element_type=F32).astype(o_ref.dtype)


def rms_matmul(x, g, w, *, tm, tn):
    m, k = x.shape
    n = w.shape[1]
    return pl.pallas_call(
        _rms_matmul_kernel,
        grid=(m // tm, n // tn),
        in_specs=[
            pl.BlockSpec((tm, k), lambda i, j: (i, 0)),
            pl.BlockSpec((1, k), lambda i, j: (0, 0)),
            pl.BlockSpec((k, tn), lambda i, j: (0, j)),
        ],
        out_specs=pl.BlockSpec((tm, tn), lambda i, j: (i, j)),
        out_shape=jax.ShapeDtypeStruct((m, n), BF16),
        scratch_shapes=[pltpu.VMEM((tm, k), BF16)],
        compiler_params=_params("parallel", "arbitrary"),
        name="rms_matmul",
    )(x, g, w)


def _hgrn2_block(q, zf, v, lb, st):
    c, s = A_CHUNK, A_SUB
    m = c // s
    tb = q.shape[0]
    nch = tb // c
    tril_mask = (lax.broadcasted_iota(jnp.int32, (c, c), 1)
                 <= lax.broadcasted_iota(jnp.int32, (c, c), 0))
    tril = jnp.where(tril_mask, 1.0, 0.0).astype(BF16)

    a = jnp.abs(zf)
    t = jnp.exp(-a)
    r = 1.0 / (1.0 + t)
    tr = t * r
    pos = zf >= 0
    sig = jnp.where(pos, r, tr)
    sig_neg = jnp.where(pos, tr, r)
    logf = jnp.log(lb + (1.0 - lb) * sig)
    kk = (1.0 - lb) * sig_neg

    hi = logf.astype(BF16)
    r1 = logf - hi.astype(F32)
    mid = r1.astype(BF16)
    lo = (r1 - mid.astype(F32)).astype(BF16)
    terms = jnp.concatenate([hi, mid, lo], axis=1)
    b_chunks = []
    for n in range(nch):
        y = jnp.dot(tril, terms[n * c:(n + 1) * c], preferred_element_type=F32)
        b_chunks.append((y[:, :A_DK] + y[:, A_DK:2 * A_DK]) + y[:, 2 * A_DK:])
    b = jnp.concatenate(b_chunks, axis=0)

    shape4 = (nch, m, s, A_DK)
    b4 = b.reshape(shape4)
    g4 = (b4 - logf.reshape(shape4))[:, :, 0:1, :]
    e4 = b4[:, :, s - 1:s, :]
    bl4 = e4[:, m - 1:m]
    lc4 = b4 - g4
    kk4 = kk.reshape(shape4)
    q_sub = q.reshape(shape4) * jnp.exp(lc4)
    k_diag = (kk4 * jnp.exp(-lc4)).astype(BF16)
    k_end = kk4 * jnp.exp(e4 - b4)
    q_in = (q_sub * jnp.exp(g4)).astype(BF16)
    k_in = (k_end * jnp.exp(bl4 - e4)).astype(BF16)
    q_sub = q_sub.astype(BF16)
    decay = jnp.exp(bl4)
    k_cross = {(i, j): (k_end[:, j] * jnp.exp(g4[:, i] - e4[:, j])).astype(BF16)
               for i in range(m) for j in range(i - 1)}
    k_end = k_end.astype(BF16)
    vb = v.astype(BF16)
    zeros = jnp.zeros((s, A_DK), BF16)

    o_intra, kv = [], []
    for n in range(nch):
        score_rows = []
        for i in range(m):
            parts = [k_end[n, j] if j == i - 1 else k_cross[(i, j)][n] for j in range(i)]
            parts.append(k_diag[n, i])
            parts.extend([zeros] * (m - 1 - i))
            k_i = jnp.concatenate(parts, axis=0)
            score_rows.append(lax.dot_general(q_sub[n, i], k_i, NT_DIMS,
                                              preferred_element_type=F32))
        scores = jnp.concatenate(score_rows, axis=0)
        scores = jnp.where(tril_mask, scores, 0.0).astype(BF16)
        vb_n = vb[n * c:(n + 1) * c]
        o_intra.append(jnp.dot(scores, vb_n, preferred_element_type=F32))
        kv.append(lax.dot_general(vb_n, k_in[n].reshape(c, A_DK), TN_DIMS,
                                  preferred_element_type=F32))

    states = []
    for n in range(nch):
        states.append(st.astype(BF16))
        st = st * decay[n, 0] + kv[n]
    outs = [o_intra[n] + lax.dot_general(q_in[n].reshape(c, A_DK), states[n], NT_DIMS,
                                         preferred_element_type=F32) for n in range(nch)]
    return jnp.concatenate(outs, axis=0), st


def _even_mixer_kernel(q_ref, f_ref, i_ref, ga_ref, u_ref, v_ref, gb_ref,
                       lbl_ref, onorm_ref, lnw_ref, lnb_ref, ws_ref, bias_ref,
                       oa_ref, ob_ref, st_ref, *, layer_j):
    tb = q_ref.shape[0]

    @pl.when(pl.program_id(2) == 0)
    def _():
        st_ref[...] = jnp.zeros_like(st_ref)

    logits = lbl_ref[...]
    e = jnp.exp(logits - jnp.max(logits, axis=0, keepdims=True))
    lb = jnp.sum(e[:layer_j + 1], axis=0, keepdims=True) / jnp.sum(e, axis=0, keepdims=True)

    o, st = _hgrn2_block(q_ref[...].astype(F32), f_ref[...].astype(F32), i_ref[...].astype(F32),
                         lb, st_ref[...])
    st_ref[...] = st
    oa_ref[...] = (_rms(o, onorm_ref[...]) * _silu(ga_ref[...].astype(F32))).astype(oa_ref.dtype)

    cb = B_CHUNK
    w_causal = jnp.where(lax.broadcasted_iota(jnp.int32, (cb, cb), 1)
                         <= lax.broadcasted_iota(jnp.int32, (cb, cb), 0),
                         ws_ref[...], 0.0).astype(BF16)
    vv = v_ref[...].astype(F32)
    mu = jnp.mean(vv, axis=-1, keepdims=True)
    xc = vv - mu
    vg = (xc * lax.rsqrt(jnp.mean(xc * xc, axis=-1, keepdims=True) + EPS) * lnw_ref[...]
          + lnb_ref[...]).astype(BF16)
    sv = jnp.concatenate([jnp.dot(w_causal, vg[n * cb:(n + 1) * cb], preferred_element_type=F32)
                          for n in range(tb // cb)], axis=0)
    bias = _column_tile(bias_ref[...])
    sv = (sv.reshape(tb // cb, cb, B_DIM) + bias).reshape(tb, B_DIM)
    ob_ref[...] = (u_ref[...].astype(F32) * sv * _silu(gb_ref[...].astype(F32))).astype(ob_ref.dtype)


def even_mixers(z, lb_logits, a_onorm, ln_w, ln_b, ws, bias, *, layer_j, tb):
    bsz, t, _ = z.shape
    h = A_HEADS
    n_rows = lb_logits.shape[0]
    lbl = lb_logits.reshape(n_rows, h, A_DK).transpose(1, 0, 2)

    def sec(k):
        return pl.BlockSpec((None, tb, LANES), lambda b, j, s, k=k: (b, s, k * h + j))

    per_head = lambda shape: pl.BlockSpec((None,) + shape, lambda b, j, s: (j, 0, 0))
    out_spec = pl.BlockSpec((None, tb, LANES), lambda b, j, s: (b, s, j))
    out_sds = jax.ShapeDtypeStruct((bsz, t, h * LANES), BF16)
    return pl.pallas_call(
        functools.partial(_even_mixer_kernel, layer_j=layer_j),
        grid=(bsz, h, t // tb),
        in_specs=[sec(k) for k in range(7)] + [
            per_head((n_rows, LANES)),
            pl.BlockSpec((1, LANES), lambda b, j, s: (0, 0)),
            per_head((1, LANES)),
            per_head((1, LANES)),
            per_head((B_CHUNK, B_CHUNK)),
            per_head((1, B_CHUNK)),
        ],
        out_specs=[out_spec, out_spec],
        out_shape=[out_sds, out_sds],
        scratch_shapes=[pltpu.VMEM((A_DK, A_DK), F32)],
        compiler_params=_params("parallel", "parallel", "arbitrary"),
        name="even_mixers",
    )(z, z, z, z, z, z, z, lbl, a_onorm.reshape(1, LANES),
      ln_w.reshape(B_GROUPS, 1, B_DIM), ln_b.reshape(B_GROUPS, 1, B_DIM),
      ws, bias.reshape(B_GROUPS, 1, B_CHUNK))


def _out_proj_kernel(*refs, n_in, with_next):
    a_refs = refs[:n_in]
    w_refs = refs[n_in:2 * n_in]
    x_ref, post_ref = refs[2 * n_in:2 * n_in + 2]
    rest = refs[2 * n_in + 2:]
    if with_next:
        pre_ref, xo_ref, ho_ref = rest
    else:
        (xo_ref,) = rest
    tm = x_ref.shape[0]
    n_sub = 2 if tm % 32 == 0 else 1
    sub = tm // n_sub

    def project(k):
        rows = slice(k * sub, (k + 1) * sub)
        y = jnp.dot(a_refs[0][rows, :], w_refs[0][...], preferred_element_type=F32)
        for a_ref, w_ref in zip(a_refs[1:], w_refs[1:]):
            y = y + jnp.dot(a_ref[rows, :], w_ref[...], preferred_element_type=F32)
        return y

    def finish(k, y):
        rows = slice(k * sub, (k + 1) * sub)
        x_new = x_ref[rows, :] + _rms(y, post_ref[...])
        xo_ref[rows, :] = x_new
        if with_next:
            ho_ref[rows, :] = _rms(x_new, pre_ref[...]).astype(BF16)

    ys = [project(0)]
    for k in range(n_sub):
        if k + 1 < n_sub:
            ys.append(project(k + 1))
        finish(k, ys[k])


def out_proj(acts, w, x, post_w, next_pre_w, *, tm):
    m, d = x.shape
    n_in = len(acts)
    ka = acts[0].shape[1]
    assert all(a.shape[1] == ka for a in acts) and w.shape == (n_in * ka, d)
    with_next = next_pre_w is not None
    row = lambda width: pl.BlockSpec((tm, width), lambda i: (i, 0))
    const = lambda shape: pl.BlockSpec(shape, lambda i: (0, 0))
    in_specs = ([row(ka) for _ in acts]
                + [pl.BlockSpec((ka, d), lambda i, r=r: (r, 0)) for r in range(n_in)]
                + [row(d), const((1, d))])
    args = list(acts) + [w] * n_in + [x, post_w.reshape(1, d)]
    out_specs = [row(d)]
    out_shape = [jax.ShapeDtypeStruct((m, d), F32)]
    if with_next:
        in_specs.append(const((1, d)))
        args.append(next_pre_w.reshape(1, d))
        out_specs.append(row(d))
        out_shape.append(jax.ShapeDtypeStruct((m, d), BF16))
    return pl.pallas_call(
        functools.partial(_out_proj_kernel, n_in=n_in, with_next=with_next),
        grid=(m // tm,),
        in_specs=in_specs,
        out_specs=out_specs,
        out_shape=out_shape,
        compiler_params=_params("parallel"),
        name="out_proj",
    )(*args)


def _lane_iota():
    return lax.broadcasted_iota(jnp.int32, (1, LANES), 1)


def _join_halves(left, right):
    half = LANES // 2
    return pltpu.roll(jnp.where(_lane_iota() < half, right, left), half, axis=1)


def _odd_in_proj_kernel(h_ref, wt_ref, c_ref, pe_ref, gate_ref):
    y = lax.dot_general(h_ref[...], wt_ref[...], NT_DIMS, preferred_element_type=F32)
    n_c = c_ref.shape[1]
    c_ref[...] = y[:, :n_c].astype(c_ref.dtype)
    pe_ref[...] = y[:, n_c:n_c + LANES].astype(pe_ref.dtype)
    gate_ref[...] = y[:, n_c + LANES:].astype(gate_ref.dtype)


def odd_in_proj(h, wt, *, tm):
    m, k = h.shape
    widths = [2 * C_RANK, LANES, C_HEADS * C_V]
    w = wt
    return pl.pallas_call(
        _odd_in_proj_kernel,
        grid=(m // tm,),
        in_specs=[pl.BlockSpec((tm, k), lambda i: (i, 0)),
                  pl.BlockSpec(w.shape, lambda i: (0, 0))],
        out_specs=[pl.BlockSpec((tm, wd), lambda i: (i, 0)) for wd in widths],
        out_shape=[jax.ShapeDtypeStruct((m, wd), BF16) for wd in widths],
        compiler_params=_params("parallel"),
        name="odd_in_proj",
    )(h, w)


def _mla_expand_kernel(c_ref, pe_ref, pos_ref, freq_ref, qnorm_ref, kvnorm_ref, wq_ref, wk_ref,
                       wvt_ref, qn_ref, qpe_ref, kn_ref, vt_ref, kpe_ref, *, scale):
    c = c_ref[...].astype(F32)
    cq = _rms(c[:, :C_RANK], qnorm_ref[...]).astype(BF16)
    ckv = _rms(c[:, C_RANK:], kvnorm_ref[...]).astype(BF16)

    q = jnp.dot(cq, wq_ref[...], preferred_element_type=F32)
    kn_ref[...] = jnp.dot(ckv, wk_ref[...], preferred_element_type=F32).astype(kn_ref.dtype)
    vt_ref[...] = lax.dot_general(wvt_ref[...], ckv, NT_DIMS,
                                  preferred_element_type=F32).astype(vt_ref.dtype)

    pos = pos_ref[...].astype(F32)
    pos = jnp.concatenate([_column_tile(pos[g:g + 1]) for g in range(pos.shape[0])], axis=0)
    ang = pos * freq_ref[...]
    cos = jnp.cos(ang)
    sin = jnp.sin(ang)
    lane = _lane_iota()
    first_half = lane < C_ROPE
    low_rot = (lane % C_ROPE) < C_ROPE // 2

    def rope(x):
        rot = jnp.where(low_rot, -pltpu.roll(x, LANES - C_ROPE // 2, axis=1),
                        pltpu.roll(x, C_ROPE // 2, axis=1))
        return x * cos + rot * sin

    for p in range(C_HEADS // 2):
        a, b, c3 = (q[:, (3 * p + k) * LANES:(3 * p + k + 1) * LANES] for k in range(3))
        qn_ref[:, 2 * p * LANES:(2 * p + 1) * LANES] = (a * scale).astype(qn_ref.dtype)
        qn_ref[:, (2 * p + 1) * LANES:(2 * p + 2) * LANES] = (
            _join_halves(b, c3) * scale).astype(qn_ref.dtype)
        pe_pair = jnp.where(first_half, b, c3)
        qpe_ref[:, p * LANES:(p + 1) * LANES] = (rope(pe_pair) * scale).astype(qpe_ref.dtype)

    k_pe = rope(pe_ref[...].astype(F32))
    kpe_ref[...] = jnp.where(first_half, k_pe,
                             pltpu.roll(k_pe, C_ROPE, axis=1)).astype(kpe_ref.dtype)


def mla_expand(c, pe, pos, freq, q_norm, kv_norm, wq, wk, wvt, *, bsz, tm, scale):
    m = c.shape[0]
    t = m // bsz
    nb = t // tm
    row = lambda width: pl.BlockSpec((tm, width), lambda i: (i, 0))
    const = lambda shape: pl.BlockSpec(shape, lambda i: (0, 0))
    n_nope = C_HEADS * C_NOPE
    n_v = C_HEADS * C_V
    widths = [n_nope, C_HEADS * C_ROPE, n_nope, None, LANES]
    out_specs = [row(wd) if wd else pl.BlockSpec((None, n_v, tm), lambda i: (i // nb, 0, i % nb))
                 for wd in widths]
    out_shape = [jax.ShapeDtypeStruct((m, wd) if wd else (bsz, n_v, t), BF16) for wd in widths]
    return pl.pallas_call(
        functools.partial(_mla_expand_kernel, scale=scale),
        grid=(m // tm,),
        in_specs=[row(c.shape[1]), row(pe.shape[1]),
                  pl.BlockSpec((None, tm // LANES, LANES), lambda i: (i, 0, 0)), const((1, LANES)),
                  const((1, C_RANK)), const((1, C_RANK)), const(wq.shape), const(wk.shape),
                  const(wvt.shape)],
        out_specs=out_specs,
        out_shape=out_shape,
        compiler_params=_params("parallel"),
        name="mla_expand",
    )(c, pe, pos, freq, q_norm.reshape(1, C_RANK), kv_norm.reshape(1, C_RANK), wq, wk, wvt)


def _attn_kernel(qn_ref, qpe_ref, kn_ref, kpe_ref, vt_ref, g_ref, o_ref, k_scr, *, tq):
    t = qn_ref.shape[0]
    heads = tuple(range(qn_ref.shape[1] // LANES))
    for hh in heads:
        k_scr[hh, :, :LANES] = kn_ref[:, hh * LANES:(hh + 1) * LANES]
        k_scr[hh, :, LANES:] = kpe_ref[...]

    lane = lax.broadcasted_iota(jnp.int32, (1, LANES), 1)
    causal = (lax.broadcasted_iota(jnp.int32, (tq, tq), 0)
              <= lax.broadcasted_iota(jnp.int32, (tq, tq), 1))
    neg = jnp.finfo(F32).min

    def scores(qi, hh):
        r0 = qi * tq
        end = r0 + tq
        q_pe = qpe_ref[r0:end, (hh // 2) * LANES:(hh // 2 + 1) * LANES]
        q_pe = jnp.where((lane // C_ROPE) == hh % 2, q_pe, jnp.zeros_like(q_pe))
        q = jnp.concatenate([qn_ref[r0:end, hh * LANES:(hh + 1) * LANES], q_pe], axis=1)
        part = end // QK_SPLIT
        return jnp.concatenate(
            [lax.dot_general(k_scr[hh, j * part:(j + 1) * part, :], q, NT_DIMS,
                             preferred_element_type=F32) for j in range(QK_SPLIT)],
            axis=0)

    def col_max(qi, s):
        r0 = qi * tq
        s_d = jnp.where(causal, s[r0:], neg)
        m = jnp.max(s_d, axis=0, keepdims=True)
        if qi > 0:
            m = jnp.maximum(m, jnp.max(s[:r0], axis=0, keepdims=True))
        return s_d, m

    def probs(qi, s, s_d, m):
        r0 = qi * tq
        p_d = jnp.exp2(s_d - m)
        l = jnp.sum(p_d, axis=0, keepdims=True)
        p_o = None
        if qi > 0:
            p_o = jnp.exp2(s[:r0] - m)
            l = l + jnp.sum(p_o, axis=0, keepdims=True)
            p_o = p_o.astype(BF16)
        return p_d.astype(BF16), p_o, l

    def values(qi, hh, p_d, p_o, l):
        r0 = qi * tq
        end = r0 + tq
        vt = vt_ref.at[hh * C_V:(hh + 1) * C_V]
        acc = jnp.dot(vt[:, r0:end], p_d, preferred_element_type=F32)
        if qi > 0:
            acc = acc + jnp.dot(vt[:, :r0], p_o, preferred_element_type=F32)
        o = (acc * (1.0 / l)).T
        cols = slice(hh * C_V, (hh + 1) * C_V)
        o_ref[r0:end, cols] = (o * _silu(g_ref[r0:end, cols].astype(F32))).astype(o_ref.dtype)

    nq = t // tq
    s = [scores(0, hh) for hh in heads]
    sm = [col_max(0, s[hh]) for hh in heads]
    for qi in range(nq):
        last = qi + 1 == nq
        s_next = None if last else [scores(qi + 1, hh) for hh in heads]
        pr = [probs(qi, s[hh], *sm[hh]) for hh in heads]
        for hh in heads:
            values(qi, hh, *pr[hh])
        if not last:
            s = s_next
            sm = [col_max(qi + 1, s[hh]) for hh in heads]


def attention(qn, qpe, kn, kpe, vt, gate, *, tq):
    bsz, t, _ = qn.shape
    g = ATTN_HEADS_PER_STEP
    group = lambda width: pl.BlockSpec((None, t, width), lambda b, p: (b, 0, p))
    shared = pl.BlockSpec((None, t, LANES), lambda b, p: (b, 0, 0))
    group_t = pl.BlockSpec((None, g * C_V, t), lambda b, p: (b, p, 0))
    return pl.pallas_call(
        functools.partial(_attn_kernel, tq=tq),
        grid=(bsz, C_HEADS // g),
        in_specs=[group(g * LANES), group(g * C_ROPE), group(g * LANES), shared, group_t,
                  group(g * C_V)],
        out_specs=group(g * C_V),
        out_shape=jax.ShapeDtypeStruct((bsz, t, C_HEADS * C_V), BF16),
        scratch_shapes=[pltpu.VMEM((g, t, 2 * LANES), BF16)],
        compiler_params=_params("parallel", "parallel"),
        name="mla_attention",
    )(qn, qpe, kn, kpe, vt, gate)


def _cast_in_proj_kernel(wt_ref, o_ref):
    n_c = 2 * C_RANK
    o_ref[:n_c + C_ROPE] = wt_ref[:n_c + C_ROPE].astype(o_ref.dtype)
    o_ref[n_c + C_ROPE:n_c + LANES] = jnp.zeros((LANES - C_ROPE, o_ref.shape[1]), o_ref.dtype)
    o_ref[n_c + LANES:] = wt_ref[n_c + C_ROPE:].astype(o_ref.dtype)


def cast_in_proj(wt, *, tk):
    n, k = wt.shape
    n_out = n + LANES - C_ROPE
    return pl.pallas_call(
        _cast_in_proj_kernel,
        grid=(k // tk,),
        in_specs=[pl.BlockSpec((n, tk), lambda i: (0, i))],
        out_specs=pl.BlockSpec((n_out, tk), lambda i: (0, i)),
        out_shape=jax.ShapeDtypeStruct((n_out, k), BF16),
        compiler_params=_params("parallel"),
        name="cast_in_proj",
    )(wt)


def _odd_weights(w_in, w_qb, w_kvb):
    w_in_b = cast_in_proj(w_in.T, tk=512)
    kvb = w_kvb.astype(BF16).reshape(C_RANK, C_HEADS, C_NOPE + C_V)
    wk = kvb[:, :, :C_NOPE].reshape(C_RANK, -1)
    wvt = kvb[:, :, C_NOPE:].reshape(C_RANK, -1).T
    return w_in_b, w_qb.astype(BF16), wk, wvt


def kernel(x, positions, norm_pre, norm_post, ev_w_in, ev_lb_logits, ev_a_onorm, ev_b_ln_w,
           ev_b_ln_b, ev_b_ws, ev_b_bias, ev_w_out, od_w_in, od_q_norm, od_w_qb, od_kv_norm,
           od_w_kvb, od_w_out):
    bsz, t, d = x.shape
    m = bsz * t
    xf = x.reshape(m, d)
    tm_big = min(1024, m)
    tm = min(512, m)

    z = rms_matmul(xf, norm_pre[0].reshape(1, d), ev_w_in[0].astype(BF16), tm=tm_big, tn=1792)
    mix_a, mix_b = even_mixers(z.reshape(bsz, t, -1), ev_lb_logits, ev_a_onorm[0], ev_b_ln_w[0],
                               ev_b_ln_b[0], ev_b_ws[0], ev_b_bias[0], layer_j=0, tb=min(1024, t))
    x1, h1 = out_proj([mix_a.reshape(m, -1), mix_b.reshape(m, -1)], ev_w_out[0].astype(BF16),
                      xf, norm_post[0], norm_pre[1], tm=tm)

    w_in_b, wq_all, wk, wvt = _odd_weights(od_w_in[0], od_w_qb[0], od_w_kvb[0])
    c, pe, gate = odd_in_proj(h1, w_in_b, tm=tm)
    inv_freq = ROPE_THETA ** (-jnp.arange(0, C_ROPE, 2, dtype=F32) / C_ROPE)
    freq = jnp.tile(inv_freq, LANES // (C_ROPE // 2)).reshape(1, LANES)
    scale = float((C_NOPE + C_ROPE) ** -0.5 * LOG2E)
    tm_e = min(tm, t)
    pos = positions.reshape(m // tm_e, tm_e // LANES, LANES)
    qn, qpe, kn, vt, kpe = mla_expand(c, pe, pos, freq, od_q_norm[0], od_kv_norm[0], wq_all, wk,
                                      wvt, bsz=bsz, tm=tm_e, scale=scale)
    r3 = lambda a: a.reshape(bsz, t, -1)
    o = attention(r3(qn), r3(qpe), r3(kn), r3(kpe), vt, r3(gate), tq=min(256, t))
    (x2,) = out_proj([o.reshape(m, -1)], od_w_out[0].astype(BF16), x1, norm_post[1], None, tm=tm)
    return x2.reshape(bsz, t, d)
```

```python
import functools

import jax
import jax.numpy as jnp
from jax import lax
from jax.experimental import pallas as pl
from jax.experimental.pallas import tpu as pltpu

F32 = jnp.float32
BF16 = jnp.bfloat16

EPS = 1e-6
LANES = 128
BF16_SUBLANES = 16
VMEM_LIMIT_BYTES = 56 * 1024 * 1024

A_HEADS = 8
A_DK = 128
A_CHUNK = 64
A_SUB = 16
B_GROUPS = 8
B_DIM = 128
B_CHUNK = 128
C_HEADS = 16
C_RANK = 512
C_NOPE = 128
C_ROPE = 64
C_V = 128
ROPE_THETA = 10000.0
LOG2E = 1.4426950408889634
QK_SPLIT = 1
ATTN_HEADS_PER_STEP = 2

NT_DIMS = (((1,), (1,)), ((), ()))
TN_DIMS = (((0,), (0,)), ((), ()))


def _params(*sem):
    return pltpu.CompilerParams(dimension_semantics=sem, vmem_limit_bytes=VMEM_LIMIT_BYTES)


def _rms(x, w):
    ms = jnp.mean(x * x, axis=-1, keepdims=True)
    return x * lax.rsqrt(ms + EPS) * w


def _silu(x):
    return x * (1.0 / (1.0 + jnp.exp(-x)))


def _column_tile(row):
    return jnp.broadcast_to(row, (LANES, LANES)).T


def _rms_matmul_kernel(x_ref, g_ref, w_ref, o_ref, h_ref):
    @pl.when(pl.program_id(1) == 0)
    def _():
        h_ref[...] = _rms(x_ref[...], g_ref[...]).astype(BF16)

    o_ref[...] = jnp.dot(h_ref[...], w_ref[...], preferred_element_type=F32).astype(o_ref.dtype)


def rms_matmul(x, g, w, *, tm, tn):
    m, k = x.shape
    n = w.shape[1]
    return pl.pallas_call(
        _rms_matmul_kernel,
        grid=(m // tm, n // tn),
        in_specs=[
            pl.BlockSpec((tm, k), lambda i, j: (i, 0)),
            pl.BlockSpec((1, k), lambda i, j: (0, 0)),
            pl.BlockSpec((k, tn), lambda i, j: (0, j)),
        ],
        out_specs=pl.BlockSpec((tm, tn), lambda i, j: (i, j)),
        out_shape=jax.ShapeDtypeStruct((m, n), BF16),
        scratch_shapes=[pltpu.VMEM((tm, k), BF16)],
        compiler_params=_params("parallel", "arbitrary"),
        name="rms_matmul",
    )(x, g, w)


def _hgrn2_block(q, zf, v, lb, st):
    c, s = A_CHUNK, A_SUB
    m = c // s
    tb = q.shape[0]
    nch = tb // c
    tril_mask = (lax.broadcasted_iota(jnp.int32, (c, c), 1)
                 <= lax.broadcasted_iota(jnp.int32, (c, c), 0))
    tril = jnp.where(tril_mask, 1.0, 0.0).astype(BF16)

    a = jnp.abs(zf)
    t = jnp.exp(-a)
    r = 1.0 / (1.0 + t)
    tr = t * r
    pos = zf >= 0
    sig = jnp.where(pos, r, tr)
    sig_neg = jnp.where(pos, tr, r)
    logf = jnp.log(lb + (1.0 - lb) * sig)
    kk = (1.0 - lb) * sig_neg

    hi = logf.astype(BF16)
    r1 = logf - hi.astype(F32)
    mid = r1.astype(BF16)
    lo = (r1 - mid.astype(F32)).astype(BF16)
    terms = jnp.concatenate([hi, mid, lo], axis=1)
    b_chunks = []
    for n in range(nch):
        y = jnp.dot(tril, terms[n * c:(n + 1) * c], preferred_element_type=F32)
        b_chunks.append((y[:, :A_DK] + y[:, A_DK:2 * A_DK]) + y[:, 2 * A_DK:])
    b = jnp.concatenate(b_chunks, axis=0)

    shape4 = (nch, m, s, A_DK)
    b4 = b.reshape(shape4)
    g4 = (b4 - logf.reshape(shape4))[:, :, 0:1, :]
    e4 = b4[:, :, s - 1:s, :]
    bl4 = e4[:, m - 1:m]
    lc4 = b4 - g4
    kk4 = kk.reshape(shape4)
    q_sub = q.reshape(shape4) * jnp.exp(lc4)
    k_diag = (kk4 * jnp.exp(-lc4)).astype(BF16)
    k_end = kk4 * jnp.exp(e4 - b4)
    q_in = (q_sub * jnp.exp(g4)).astype(BF16)
    k_in = (k_end * jnp.exp(bl4 - e4)).astype(BF16)
    q_sub = q_sub.astype(BF16)
    decay = jnp.exp(bl4)
    k_cross = {(i, j): (k_end[:, j] * jnp.exp(g4[:, i] - e4[:, j])).astype(BF16)
               for i in range(m) for j in range(i - 1)}
    k_end = k_end.astype(BF16)
    vb = v.astype(BF16)
    zeros = jnp.zeros((s, A_DK), BF16)

    o_intra, kv = [], []
    for n in range(nch):
        score_rows = []
        for i in range(m):
            parts = [k_end[n, j] if j == i - 1 else k_cross[(i, j)][n] for j in range(i)]
            parts.append(k_diag[n, i])
            parts.extend([zeros] * (m - 1 - i))
            k_i = jnp.concatenate(parts, axis=0)
            score_rows.append(lax.dot_general(q_sub[n, i], k_i, NT_DIMS,
                                              preferred_element_type=F32))
        scores = jnp.concatenate(score_rows, axis=0)
        scores = jnp.where(tril_mask, scores, 0.0).astype(BF16)
        vb_n = vb[n * c:(n + 1) * c]
        o_intra.append(jnp.dot(scores, vb_n, preferred_element_type=F32))
        kv.append(lax.dot_general(vb_n, k_in[n].reshape(c, A_DK), TN_DIMS,
                                  preferred_element_type=F32))

    states = []
    for n in range(nch):
        states.append(st.astype(BF16))
        st = st * decay[n, 0] + kv[n]
    outs = [o_intra[n] + lax.dot_general(q_in[n].reshape(c, A_DK), states[n], NT_DIMS,
                                         preferred_element_type=F32) for n in range(nch)]
    return jnp.concatenate(outs, axis=0), st


def _even_mixer_kernel(q_ref, f_ref, i_ref, ga_ref, u_ref, v_ref, gb_ref,
                       lbl_ref, onorm_ref, lnw_ref, lnb_ref, ws_ref, bias_ref,
                       oa_ref, ob_ref, st_ref, *, layer_j):
    tb = q_ref.shape[0]

    @pl.when(pl.program_id(2) == 0)
    def _():
        st_ref[...] = jnp.zeros_like(st_ref)

    logits = lbl_ref[...]
    e = jnp.exp(logits - jnp.max(logits, axis=0, keepdims=True))
    lb = jnp.sum(e[:layer_j + 1], axis=0, keepdims=True) / jnp.sum(e, axis=0, keepdims=True)

    o, st = _hgrn2_block(q_ref[...].astype(F32), f_ref[...].astype(F32), i_ref[...].astype(F32),
                         lb, st_ref[...])
    st_ref[...] = st
    oa_ref[...] = (_rms(o, onorm_ref[...]) * _silu(ga_ref[...].astype(F32))).astype(oa_ref.dtype)

    cb = B_CHUNK
    w_causal = jnp.where(lax.broadcasted_iota(jnp.int32, (cb, cb), 1)
                         <= lax.broadcasted_iota(jnp.int32, (cb, cb), 0),
                         ws_ref[...], 0.0).astype(BF16)
    vv = v_ref[...].astype(F32)
    mu = jnp.mean(vv, axis=-1, keepdims=True)
    xc = vv - mu
    vg = (xc * lax.rsqrt(jnp.mean(xc * xc, axis=-1, keepdims=True) + EPS) * lnw_ref[...]
          + lnb_ref[...]).astype(BF16)
    sv = jnp.concatenate([jnp.dot(w_causal, vg[n * cb:(n + 1) * cb], preferred_element_type=F32)
                          for n in range(tb // cb)], axis=0)
    bias = _column_tile(bias_ref[...])
    sv = (sv.reshape(tb // cb, cb, B_DIM) + bias).reshape(tb, B_DIM)
    ob_ref[...] = (u_ref[...].astype(F32) * sv * _silu(gb_ref[...].astype(F32))).astype(ob_ref.dtype)


def even_mixers(z, lb_logits, a_onorm, ln_w, ln_b, ws, bias, *, layer_j, tb):
    bsz, t, _ = z.shape
    h = A_HEADS
    n_rows = lb_logits.shape[0]
    lbl = lb_logits.reshape(n_rows, h, A_DK).transpose(1, 0, 2)

    def sec(k):
        return pl.BlockSpec((None, tb, LANES), lambda b, j, s, k=k: (b, s, k * h + j))

    per_head = lambda shape: pl.BlockSpec((None,) + shape, lambda b, j, s: (j, 0, 0))
    out_spec = pl.BlockSpec((None, tb, LANES), lambda b, j, s: (b, s, j))
    out_sds = jax.ShapeDtypeStruct((bsz, t, h * LANES), BF16)
    return pl.pallas_call(
        functools.partial(_even_mixer_kernel, layer_j=layer_j),
        grid=(bsz, h, t // tb),
        in_specs=[sec(k) for k in range(7)] + [
            per_head((n_rows, LANES)),
            pl.BlockSpec((1, LANES), lambda b, j, s: (0, 0)),
            per_head((1, LANES)),
            per_head((1, LANES)),
            per_head((B_CHUNK, B_CHUNK)),
            per_head((1, B_CHUNK)),
        ],
        out_specs=[out_spec, out_spec],
        out_shape=[out_sds, out_sds],
        scratch_shapes=[pltpu.VMEM((A_DK, A_DK), F32)],
        compiler_params=_params("parallel", "parallel", "arbitrary"),
        name="even_mixers",
    )(z, z, z, z, z, z, z, lbl, a_onorm.reshape(1, LANES),
      ln_w.reshape(B_GROUPS, 1, B_DIM), ln_b.reshape(B_GROUPS, 1, B_DIM),
      ws, bias.reshape(B_GROUPS, 1, B_CHUNK))


def _out_proj_kernel(*refs, n_in, with_next):
    a_refs = refs[:n_in]
    w_refs = refs[n_in:2 * n_in]
    x_ref, post_ref = refs[2 * n_in:2 * n_in + 2]
    rest = refs[2 * n_in + 2:]
    if with_next:
        pre_ref, xo_ref, ho_ref = rest
    else:
        (xo_ref,) = rest
    tm = x_ref.shape[0]
    n_sub = 2 if tm % 32 == 0 else 1
    sub = tm // n_sub

    def project(k):
        rows = slice(k * sub, (k + 1) * sub)
        y = jnp.dot(a_refs[0][rows, :], w_refs[0][...], preferred_element_type=F32)
        for a_ref, w_ref in zip(a_refs[1:], w_refs[1:]):
            y = y + jnp.dot(a_ref[rows, :], w_ref[...], preferred_element_type=F32)
        return y

    def finish(k, y):
        rows = slice(k * sub, (k + 1) * sub)
        x_new = x_ref[rows, :] + _rms(y, post_ref[...])
        xo_ref[rows, :] = x_new
        if with_next:
            ho_ref[rows, :] = _rms(x_new, pre_ref[...]).astype(BF16)

    ys = [project(0)]
    for k in range(n_sub):
        if k + 1 < n_sub:
            ys.append(project(k + 1))
        finish(k, ys[k])


def out_proj(acts, w, x, post_w, next_pre_w, *, tm):
    m, d = x.shape
    n_in = len(acts)
    ka = acts[0].shape[1]
    assert all(a.shape[1] == ka for a in acts) and w.shape == (n_in * ka, d)
    with_next = next_pre_w is not None
    row = lambda width: pl.BlockSpec((tm, width), lambda i: (i, 0))
    const = lambda shape: pl.BlockSpec(shape, lambda i: (0, 0))
    in_specs = ([row(ka) for _ in acts]
                + [pl.BlockSpec((ka, d), lambda i, r=r: (r, 0)) for r in range(n_in)]
                + [row(d), const((1, d))])
    args = list(acts) + [w] * n_in + [x, post_w.reshape(1, d)]
    out_specs = [row(d)]
    out_shape = [jax.ShapeDtypeStruct((m, d), F32)]
    if with_next:
        in_specs.append(const((1, d)))
        args.append(next_pre_w.reshape(1, d))
        out_specs.append(row(d))
        out_shape.append(jax.ShapeDtypeStruct((m, d), BF16))
    return pl.pallas_call(
        functools.partial(_out_proj_kernel, n_in=n_in, with_next=with_next),
        grid=(m // tm,),
        in_specs=in_specs,
        out_specs=out_specs,
        out_shape=out_shape,
        compiler_params=_params("parallel"),
        name="out_proj",
    )(*args)


def _lane_iota():
    return lax.broadcasted_iota(jnp.int32, (1, LANES), 1)


def _join_halves(left, right):
    half = LANES // 2
    return pltpu.roll(jnp.where(_lane_iota() < half, right, left), half, axis=1)


def _odd_in_proj_kernel(h_ref, wt_ref, c_ref, pe_ref, gate_ref):
    y = lax.dot_general(h_ref[...], wt_ref[...], NT_DIMS, preferred_element_type=F32)
    n_c = c_ref.shape[1]
    c_ref[...] = y[:, :n_c].astype(c_ref.dtype)
    pe_ref[...] = y[:, n_c:n_c + LANES].astype(pe_ref.dtype)
    gate_ref[...] = y[:, n_c + LANES:].astype(gate_ref.dtype)


def odd_in_proj(h, wt, *, tm):
    m, k = h.shape
    widths = [2 * C_RANK, LANES, C_HEADS * C_V]
    w = wt
    return pl.pallas_call(
        _odd_in_proj_kernel,
        grid=(m // tm,),
        in_specs=[pl.BlockSpec((tm, k), lambda i: (i, 0)),
                  pl.BlockSpec(w.shape, lambda i: (0, 0))],
        out_specs=[pl.BlockSpec((tm, wd), lambda i: (i, 0)) for wd in widths],
        out_shape=[jax.ShapeDtypeStruct((m, wd), BF16) for wd in widths],
        compiler_params=_params("parallel"),
        name="odd_in_proj",
    )(h, w)


def _mla_expand_kernel(c_ref, pe_ref, pos_ref, freq_ref, qnorm_ref, kvnorm_ref, wq_ref, wk_ref,
                       wvt_ref, qn_ref, qpe_ref, kn_ref, vt_ref, kpe_ref, *, scale):
    c = c_ref[...].astype(F32)
    cq = _rms(c[:, :C_RANK], qnorm_ref[...]).astype(BF16)
    ckv = _rms(c[:, C_RANK:], kvnorm_ref[...]).astype(BF16)

    q = jnp.dot(cq, wq_ref[...], preferred_element_type=F32)
    kn_ref[...] = jnp.dot(ckv, wk_ref[...], preferred_element_type=F32).astype(kn_ref.dtype)
    vt_ref[...] = lax.dot_general(wvt_ref[...], ckv, NT_DIMS,
                                  preferred_element_type=F32).astype(vt_ref.dtype)

    pos = pos_ref[...].astype(F32)
    pos = jnp.concatenate([_column_tile(pos[g:g + 1]) for g in range(pos.shape[0])], axis=0)
    ang = pos * freq_ref[...]
    cos = jnp.cos(ang)
    sin = jnp.sin(ang)
    lane = _lane_iota()
    first_half = lane < C_ROPE
    low_rot = (lane % C_ROPE) < C_ROPE // 2

    def rope(x):
        rot = jnp.where(low_rot, -pltpu.roll(x, LANES - C_ROPE // 2, axis=1),
                        pltpu.roll(x, C_ROPE // 2, axis=1))
        return x * cos + rot * sin

    for p in range(C_HEADS // 2):
        a, b, c3 = (q[:, (3 * p + k) * LANES:(3 * p + k + 1) * LANES] for k in range(3))
        qn_ref[:, 2 * p * LANES:(2 * p + 1) * LANES] = (a * scale).astype(qn_ref.dtype)
        qn_ref[:, (2 * p + 1) * LANES:(2 * p + 2) * LANES] = (
            _join_halves(b, c3) * scale).astype(qn_ref.dtype)
        pe_pair = jnp.where(first_half, b, c3)
        qpe_ref[:, p * LANES:(p + 1) * LANES] = (rope(pe_pair) * scale).astype(qpe_ref.dtype)

    k_pe = rope(pe_ref[...].astype(F32))
    kpe_ref[...] = jnp.where(first_half, k_pe,
                             pltpu.roll(k_pe, C_ROPE, axis=1)).astype(kpe_ref.dtype)


def mla_expand(c, pe, pos, freq, q_norm, kv_norm, wq, wk, wvt, *, bsz, tm, scale):
    m = c.shape[0]
    t = m // bsz
    nb = t // tm
    row = lambda width: pl.BlockSpec((tm, width), lambda i: (i, 0))
    const = lambda shape: pl.BlockSpec(shape, lambda i: (0, 0))
    n_nope = C_HEADS * C_NOPE
    n_v = C_HEADS * C_V
    widths = [n_nope, C_HEADS * C_ROPE, n_nope, None, LANES]
    out_specs = [row(wd) if wd else pl.BlockSpec((None, n_v, tm), lambda i: (i // nb, 0, i % nb))
                 for wd in widths]
    out_shape = [jax.ShapeDtypeStruct((m, wd) if wd else (bsz, n_v, t), BF16) for wd in widths]
    return pl.pallas_call(
        functools.partial(_mla_expand_kernel, scale=scale),
        grid=(m // tm,),
        in_specs=[row(c.shape[1]), row(pe.shape[1]),
                  pl.BlockSpec((None, tm // LANES, LANES), lambda i: (i, 0, 0)), const((1, LANES)),
                  const((1, C_RANK)), const((1, C_RANK)), const(wq.shape), const(wk.shape),
                  const(wvt.shape)],
        out_specs=out_specs,
        out_shape=out_shape,
        compiler_params=_params("parallel"),
        name="mla_expand",
    )(c, pe, pos, freq, q_norm.reshape(1, C_RANK), kv_norm.reshape(1, C_RANK), wq, wk, wvt)


def _attn_kernel(qn_ref, qpe_ref, kn_ref, kpe_ref, vt_ref, g_ref, o_ref, k_scr, v_scr, *, tq):
    t = qn_ref.shape[0]
    heads = tuple(range(qn_ref.shape[1] // LANES))
    for hh in heads:
        k_scr[hh, :, :LANES] = kn_ref[:, hh * LANES:(hh + 1) * LANES]
        k_scr[hh, :, LANES:] = kpe_ref[...]
        v_scr[hh, :C_V] = vt_ref[hh * C_V:(hh + 1) * C_V]
        v_scr[hh, C_V:] = jnp.ones((v_scr.shape[1] - C_V, t), v_scr.dtype)

    lane = lax.broadcasted_iota(jnp.int32, (1, LANES), 1)
    causal = (lax.broadcasted_iota(jnp.int32, (tq, tq), 0)
              <= lax.broadcasted_iota(jnp.int32, (tq, tq), 1))
    neg = jnp.finfo(BF16).min

    def scores(qi, hh):
        r0 = qi * tq
        end = r0 + tq
        q_pe = qpe_ref[r0:end, (hh // 2) * LANES:(hh // 2 + 1) * LANES]
        q_pe = jnp.where((lane // C_ROPE) == hh % 2, q_pe, jnp.zeros_like(q_pe))
        q = jnp.concatenate([qn_ref[r0:end, hh * LANES:(hh + 1) * LANES], q_pe], axis=1)
        part = end // QK_SPLIT
        return jnp.concatenate(
            [lax.dot_general(k_scr[hh, j * part:(j + 1) * part, :], q, NT_DIMS,
                             preferred_element_type=F32) for j in range(QK_SPLIT)],
            axis=0).astype(BF16)

    def col_max(qi, s):
        r0 = qi * tq
        m = jnp.max(jnp.where(causal, s[r0:], neg), axis=0, keepdims=True)
        if qi > 0:
            m = jnp.maximum(m, jnp.max(s[:r0], axis=0, keepdims=True))
        return m

    def probs(qi, s, m):
        r0 = qi * tq
        p_d = jnp.where(causal, jnp.exp2(s[r0:] - m), jnp.zeros((), BF16))
        p_o = jnp.exp2(s[:r0] - m) if qi > 0 else None
        return p_d, p_o

    def values(qi, hh, p_d, p_o):
        r0 = qi * tq
        end = r0 + tq
        vt = v_scr.at[hh]
        acc = jnp.dot(vt[:, r0:end], p_d, preferred_element_type=F32)
        if qi > 0:
            acc = acc + jnp.dot(vt[:, :r0], p_o, preferred_element_type=F32)
        o = (acc[:C_V] * (1.0 / acc[C_V:C_V + 1])).T
        cols = slice(hh * C_V, (hh + 1) * C_V)
        o_ref[r0:end, cols] = (o * _silu(g_ref[r0:end, cols].astype(F32))).astype(o_ref.dtype)

    nq = t // tq
    s = [scores(0, hh) for hh in heads]
    sm = [col_max(0, s[hh]) for hh in heads]
    for qi in range(nq):
        last = qi + 1 == nq
        s_next = None if last else [scores(qi + 1, hh) for hh in heads]
        pr = [probs(qi, s[hh], sm[hh]) for hh in heads]
        for hh in heads:
            values(qi, hh, *pr[hh])
        if not last:
            s = s_next
            sm = [col_max(qi + 1, s[hh]) for hh in heads]


def attention(qn, qpe, kn, kpe, vt, gate, *, tq):
    bsz, t, _ = qn.shape
    g = ATTN_HEADS_PER_STEP
    group = lambda width: pl.BlockSpec((None, t, width), lambda b, p: (b, 0, p))
    shared = pl.BlockSpec((None, t, LANES), lambda b, p: (b, 0, 0))
    group_t = pl.BlockSpec((None, g * C_V, t), lambda b, p: (b, p, 0))
    return pl.pallas_call(
        functools.partial(_attn_kernel, tq=tq),
        grid=(bsz, C_HEADS // g),
        in_specs=[group(g * LANES), group(g * C_ROPE), group(g * LANES), shared, group_t,
                  group(g * C_V)],
        out_specs=group(g * C_V),
        out_shape=jax.ShapeDtypeStruct((bsz, t, C_HEADS * C_V), BF16),
        scratch_shapes=[pltpu.VMEM((g, t, 2 * LANES), BF16),
                        pltpu.VMEM((g, C_V + BF16_SUBLANES, t), BF16)],
        compiler_params=_params("parallel", "parallel"),
        name="mla_attention",
    )(qn, qpe, kn, kpe, vt, gate)


def _cast_in_proj_kernel(wt_ref, o_ref):
    n_c = 2 * C_RANK
    o_ref[:n_c + C_ROPE] = wt_ref[:n_c + C_ROPE].astype(o_ref.dtype)
    o_ref[n_c + C_ROPE:n_c + LANES] = jnp.zeros((LANES - C_ROPE, o_ref.shape[1]), o_ref.dtype)
    o_ref[n_c + LANES:] = wt_ref[n_c + C_ROPE:].astype(o_ref.dtype)


def cast_in_proj(wt, *, tk):
    n, k = wt.shape
    n_out = n + LANES - C_ROPE
    return pl.pallas_call(
        _cast_in_proj_kernel,
        grid=(k // tk,),
        in_specs=[pl.BlockSpec((n, tk), lambda i: (0, i))],
        out_specs=pl.BlockSpec((n_out, tk), lambda i: (0, i)),
        out_shape=jax.ShapeDtypeStruct((n_out, k), BF16),
        compiler_params=_params("parallel"),
        name="cast_in_proj",
    )(wt)


def _odd_weights(w_in, w_qb, w_kvb):
    w_in_b = cast_in_proj(w_in.T, tk=512)
    kvb = w_kvb.astype(BF16).reshape(C_RANK, C_HEADS, C_NOPE + C_V)
    wk = kvb[:, :, :C_NOPE].reshape(C_RANK, -1)
    wvt = kvb[:, :, C_NOPE:].reshape(C_RANK, -1).T
    return w_in_b, w_qb.astype(BF16), wk, wvt


def kernel(x, positions, norm_pre, norm_post, ev_w_in, ev_lb_logits, ev_a_onorm, ev_b_ln_w,
           ev_b_ln_b, ev_b_ws, ev_b_bias, ev_w_out, od_w_in, od_q_norm, od_w_qb, od_kv_norm,
           od_w_kvb, od_w_out):
    bsz, t, d = x.shape
    m = bsz * t
    xf = x.reshape(m, d)
    tm_big = min(1024, m)
    tm = min(512, m)

    z = rms_matmul(xf, norm_pre[0].reshape(1, d), ev_w_in[0].astype(BF16), tm=tm_big, tn=1792)
    mix_a, mix_b = even_mixers(z.reshape(bsz, t, -1), ev_lb_logits, ev_a_onorm[0], ev_b_ln_w[0],
                               ev_b_ln_b[0], ev_b_ws[0], ev_b_bias[0], layer_j=0, tb=min(1024, t))
    x1, h1 = out_proj([mix_a.reshape(m, -1), mix_b.reshape(m, -1)], ev_w_out[0].astype(BF16),
                      xf, norm_post[0], norm_pre[1], tm=tm)

    w_in_b, wq_all, wk, wvt = _odd_weights(od_w_in[0], od_w_qb[0], od_w_kvb[0])
    c, pe, gate = odd_in_proj(h1, w_in_b, tm=tm)
    inv_freq = ROPE_THETA ** (-jnp.arange(0, C_ROPE, 2, dtype=F32) / C_ROPE)
    freq = jnp.tile(inv_freq, LANES // (C_ROPE // 2)).reshape(1, LANES)
    scale = float((C_NOPE + C_ROPE) ** -0.5 * LOG2E)
    tm_e = min(tm, t)
    pos = positions.reshape(m // tm_e, tm_e // LANES, LANES)
    qn, qpe, kn, vt, kpe = mla_expand(c, pe, pos, freq, od_q_norm[0], od_kv_norm[0], wq_all, wk,
                                      wvt, bsz=bsz, tm=tm_e, scale=scale)
    r3 = lambda a: a.reshape(bsz, t, -1)
    o = attention(r3(qn), r3(qpe), r3(kn), r3(kpe), vt, r3(gate), tq=min(256, t))
    (x2,) = out_proj([o.reshape(m, -1)], od_w_out[0].astype(BF16), x1, norm_post[1], None, tm=tm)
    return x2.reshape(bsz, t, d)
```

```python
import functools

import jax
import jax.numpy as jnp
from jax import lax
from jax.experimental import pallas as pl
from jax.experimental.pallas import tpu as pltpu

F32 = jnp.float32
BF16 = jnp.bfloat16

EPS = 1e-6
LANES = 128
BF16_SUBLANES = 16
VMEM_LIMIT_BYTES = 56 * 1024 * 1024

A_HEADS = 8
A_DK = 128
A_CHUNK = 64
A_SUB = 16
B_GROUPS = 8
B_DIM = 128
B_CHUNK = 128
C_HEADS = 16
C_RANK = 512
C_NOPE = 128
C_ROPE = 64
C_V = 128
ROPE_THETA = 10000.0
LOG2E = 1.4426950408889634
QK_SPLIT = 1
ATTN_HEADS_PER_STEP = 2

NT_DIMS = (((1,), (1,)), ((), ()))
TN_DIMS = (((0,), (0,)), ((), ()))


def _params(*sem):
    return pltpu.CompilerParams(dimension_semantics=sem, vmem_limit_bytes=VMEM_LIMIT_BYTES)


def _rms(x, w):
    ms = jnp.mean(x * x, axis=-1, keepdims=True)
    return x * lax.rsqrt(ms + EPS) * w


def _silu(x):
    return x * (1.0 / (1.0 + jnp.exp(-x)))


def _column_tile(row):
    return jnp.broadcast_to(row, (LANES, LANES)).T


def _rms_matmul_kernel(x_ref, g_ref, w_ref, o_ref, h_ref):
    @pl.when(pl.program_id(1) == 0)
    def _():
        h_ref[...] = _rms(x_ref[...], g_ref[...]).astype(BF16)

    o_ref[...] = jnp.dot(h_ref[...], w_ref[...], preferred_element_type=F32).astype(o_ref.dtype)


def rms_matmul(x, g, w, *, tm, tn):
    m, k = x.shape
    n = w.shape[1]
    return pl.pallas_call(
        _rms_matmul_kernel,
        grid=(m // tm, n // tn),
        in_specs=[
            pl.BlockSpec((tm, k), lambda i, j: (i, 0)),
            pl.BlockSpec((1, k), lambda i, j: (0, 0)),
            pl.BlockSpec((k, tn), lambda i, j: (0, j)),
        ],
        out_specs=pl.BlockSpec((tm, tn), lambda i, j: (i, j)),
        out_shape=jax.ShapeDtypeStruct((m, n), BF16),
        scratch_shapes=[pltpu.VMEM((tm, k), BF16)],
        compiler_params=_params("parallel", "arbitrary"),
        name="rms_matmul",
    )(x, g, w)


def _hgrn2_block(q, zf, v, lb, st):
    c, s = A_CHUNK, A_SUB
    m = c // s
    tb = q.shape[0]
    nch = tb // c
    tril_mask = (lax.broadcasted_iota(jnp.int32, (c, c), 1)
                 <= lax.broadcasted_iota(jnp.int32, (c, c), 0))
    tril = jnp.where(tril_mask, 1.0, 0.0).astype(BF16)

    a = jnp.abs(zf)
    t = jnp.exp(-a)
    r = 1.0 / (1.0 + t)
    tr = t * r
    pos = zf >= 0
    sig = jnp.where(pos, r, tr)
    sig_neg = jnp.where(pos, tr, r)
    logf = jnp.log(lb + (1.0 - lb) * sig)
    kk = (1.0 - lb) * sig_neg

    hi = logf.astype(BF16)
    r1 = logf - hi.astype(F32)
    mid = r1.astype(BF16)
    lo = (r1 - mid.astype(F32)).astype(BF16)
    terms = jnp.concatenate([hi, mid, lo], axis=1)
    b_chunks = []
    for n in range(nch):
        y = jnp.dot(tril, terms[n * c:(n + 1) * c], preferred_element_type=F32)
        b_chunks.append((y[:, :A_DK] + y[:, A_DK:2 * A_DK]) + y[:, 2 * A_DK:])
    b = jnp.concatenate(b_chunks, axis=0)

    shape4 = (nch, m, s, A_DK)
    b4 = b.reshape(shape4)
    g4 = (b4 - logf.reshape(shape4))[:, :, 0:1, :]
    e4 = b4[:, :, s - 1:s, :]
    bl4 = e4[:, m - 1:m]
    lc4 = b4 - g4
    kk4 = kk.reshape(shape4)
    q_sub = q.reshape(shape4) * jnp.exp(lc4)
    k_diag = (kk4 * jnp.exp(-lc4)).astype(BF16)
    k_end = kk4 * jnp.exp(e4 - b4)
    q_in = (q_sub * jnp.exp(g4)).astype(BF16)
    k_in = (k_end * jnp.exp(bl4 - e4)).astype(BF16)
    q_sub = q_sub.astype(BF16)
    decay = jnp.exp(bl4)
    k_cross = {(i, j): (k_end[:, j] * jnp.exp(g4[:, i] - e4[:, j])).astype(BF16)
               for i in range(m) for j in range(i - 1)}
    k_end = k_end.astype(BF16)
    vb = v.astype(BF16)
    zeros = jnp.zeros((s, A_DK), BF16)

    o_intra, kv = [], []
    for n in range(nch):
        score_rows = []
        for i in range(m):
            parts = [k_end[n, j] if j == i - 1 else k_cross[(i, j)][n] for j in range(i)]
            parts.append(k_diag[n, i])
            parts.extend([zeros] * (m - 1 - i))
            k_i = jnp.concatenate(parts, axis=0)
            score_rows.append(lax.dot_general(q_sub[n, i], k_i, NT_DIMS,
                                              preferred_element_type=F32))
        scores = jnp.concatenate(score_rows, axis=0)
        scores = jnp.where(tril_mask, scores, 0.0).astype(BF16)
        vb_n = vb[n * c:(n + 1) * c]
        o_intra.append(jnp.dot(scores, vb_n, preferred_element_type=F32))
        kv.append(lax.dot_general(vb_n, k_in[n].reshape(c, A_DK), TN_DIMS,
                                  preferred_element_type=F32))

    states = []
    for n in range(nch):
        states.append(st.astype(BF16))
        st = st * decay[n, 0] + kv[n]
    outs = [o_intra[n] + lax.dot_general(q_in[n].reshape(c, A_DK), states[n], NT_DIMS,
                                         preferred_element_type=F32) for n in range(nch)]
    return jnp.concatenate(outs, axis=0), st


def _even_mixer_kernel(q_ref, f_ref, i_ref, ga_ref, u_ref, v_ref, gb_ref,
                       lbl_ref, onorm_ref, lnw_ref, lnb_ref, ws_ref, bias_ref,
                       oa_ref, ob_ref, st_ref, *, layer_j):
    tb = q_ref.shape[0]

    @pl.when(pl.program_id(2) == 0)
    def _():
        st_ref[...] = jnp.zeros_like(st_ref)

    logits = lbl_ref[...]
    e = jnp.exp(logits - jnp.max(logits, axis=0, keepdims=True))
    lb = jnp.sum(e[:layer_j + 1], axis=0, keepdims=True) / jnp.sum(e, axis=0, keepdims=True)

    o, st = _hgrn2_block(q_ref[...].astype(F32), f_ref[...].astype(F32), i_ref[...].astype(F32),
                         lb, st_ref[...])
    st_ref[...] = st
    oa_ref[...] = (_rms(o, onorm_ref[...]) * _silu(ga_ref[...].astype(F32))).astype(oa_ref.dtype)

    cb = B_CHUNK
    w_causal = jnp.where(lax.broadcasted_iota(jnp.int32, (cb, cb), 1)
                         <= lax.broadcasted_iota(jnp.int32, (cb, cb), 0),
                         ws_ref[...], 0.0).astype(BF16)
    vv = v_ref[...].astype(F32)
    mu = jnp.mean(vv, axis=-1, keepdims=True)
    xc = vv - mu
    vg = (xc * lax.rsqrt(jnp.mean(xc * xc, axis=-1, keepdims=True) + EPS) * lnw_ref[...]
          + lnb_ref[...]).astype(BF16)
    sv = jnp.concatenate([jnp.dot(w_causal, vg[n * cb:(n + 1) * cb], preferred_element_type=F32)
                          for n in range(tb // cb)], axis=0)
    bias = _column_tile(bias_ref[...])
    sv = (sv.reshape(tb // cb, cb, B_DIM) + bias).reshape(tb, B_DIM)
    ob_ref[...] = (u_ref[...].astype(F32) * sv * _silu(gb_ref[...].astype(F32))).astype(ob_ref.dtype)


def even_mixers(z, lb_logits, a_onorm, ln_w, ln_b, ws, bias, *, layer_j, tb):
    bsz, t, _ = z.shape
    h = A_HEADS
    n_rows = lb_logits.shape[0]
    lbl = lb_logits.reshape(n_rows, h, A_DK).transpose(1, 0, 2)

    def sec(k):
        return pl.BlockSpec((None, tb, LANES), lambda b, j, s, k=k: (b, s, k * h + j))

    per_head = lambda shape: pl.BlockSpec((None,) + shape, lambda b, j, s: (j, 0, 0))
    out_spec = pl.BlockSpec((None, tb, LANES), lambda b, j, s: (b, s, j))
    out_sds = jax.ShapeDtypeStruct((bsz, t, h * LANES), BF16)
    return pl.pallas_call(
        functools.partial(_even_mixer_kernel, layer_j=layer_j),
        grid=(bsz, h, t // tb),
        in_specs=[sec(k) for k in range(7)] + [
            per_head((n_rows, LANES)),
            pl.BlockSpec((1, LANES), lambda b, j, s: (0, 0)),
            per_head((1, LANES)),
            per_head((1, LANES)),
            per_head((B_CHUNK, B_CHUNK)),
            per_head((1, B_CHUNK)),
        ],
        out_specs=[out_spec, out_spec],
        out_shape=[out_sds, out_sds],
        scratch_shapes=[pltpu.VMEM((A_DK, A_DK), F32)],
        compiler_params=_params("parallel", "parallel", "arbitrary"),
        name="even_mixers",
    )(z, z, z, z, z, z, z, lbl, a_onorm.reshape(1, LANES),
      ln_w.reshape(B_GROUPS, 1, B_DIM), ln_b.reshape(B_GROUPS, 1, B_DIM),
      ws, bias.reshape(B_GROUPS, 1, B_CHUNK))


def _out_proj_kernel(*refs, n_in, with_next):
    a_refs = refs[:n_in]
    w_refs = refs[n_in:2 * n_in]
    x_ref, post_ref = refs[2 * n_in:2 * n_in + 2]
    rest = refs[2 * n_in + 2:]
    if with_next:
        pre_ref, xo_ref, ho_ref = rest
    else:
        (xo_ref,) = rest
    tm = x_ref.shape[0]
    n_sub = 2 if tm % 32 == 0 else 1
    sub = tm // n_sub

    def project(k):
        rows = slice(k * sub, (k + 1) * sub)
        y = jnp.dot(a_refs[0][rows, :], w_refs[0][...], preferred_element_type=F32)
        for a_ref, w_ref in zip(a_refs[1:], w_refs[1:]):
            y = y + jnp.dot(a_ref[rows, :], w_ref[...], preferred_element_type=F32)
        return y

    def finish(k, y):
        rows = slice(k * sub, (k + 1) * sub)
        x_new = x_ref[rows, :] + _rms(y, post_ref[...])
        xo_ref[rows, :] = x_new
        if with_next:
            ho_ref[rows, :] = _rms(x_new, pre_ref[...]).astype(BF16)

    ys = [project(0)]
    for k in range(n_sub):
        if k + 1 < n_sub:
            ys.append(project(k + 1))
        finish(k, ys[k])


def out_proj(acts, w, x, post_w, next_pre_w, *, tm):
    m, d = x.shape
    n_in = len(acts)
    ka = acts[0].shape[1]
    assert all(a.shape[1] == ka for a in acts) and w.shape == (n_in * ka, d)
    with_next = next_pre_w is not None
    row = lambda width: pl.BlockSpec((tm, width), lambda i: (i, 0))
    const = lambda shape: pl.BlockSpec(shape, lambda i: (0, 0))
    in_specs = ([row(ka) for _ in acts]
                + [pl.BlockSpec((ka, d), lambda i, r=r: (r, 0)) for r in range(n_in)]
                + [row(d), const((1, d))])
    args = list(acts) + [w] * n_in + [x, post_w.reshape(1, d)]
    out_specs = [row(d)]
    out_shape = [jax.ShapeDtypeStruct((m, d), F32)]
    if with_next:
        in_specs.append(const((1, d)))
        args.append(next_pre_w.reshape(1, d))
        out_specs.append(row(d))
        out_shape.append(jax.ShapeDtypeStruct((m, d), BF16))
    return pl.pallas_call(
        functools.partial(_out_proj_kernel, n_in=n_in, with_next=with_next),
        grid=(m // tm,),
        in_specs=in_specs,
        out_specs=out_specs,
        out_shape=out_shape,
        compiler_params=_params("parallel"),
        name="out_proj",
    )(*args)


def _lane_iota():
    return lax.broadcasted_iota(jnp.int32, (1, LANES), 1)


def _join_halves(left, right):
    half = LANES // 2
    return pltpu.roll(jnp.where(_lane_iota() < half, right, left), half, axis=1)


def _odd_in_proj_kernel(h_ref, wt_ref, c_ref, pe_ref, gate_ref):
    y = lax.dot_general(h_ref[...], wt_ref[...], NT_DIMS, preferred_element_type=F32)
    n_c = c_ref.shape[1]
    c_ref[...] = y[:, :n_c].astype(c_ref.dtype)
    pe_ref[...] = y[:, n_c:n_c + LANES].astype(pe_ref.dtype)
    gate_ref[...] = y[:, n_c + LANES:].astype(gate_ref.dtype)


def odd_in_proj(h, wt, *, tm):
    m, k = h.shape
    widths = [2 * C_RANK, LANES, C_HEADS * C_V]
    w = wt
    return pl.pallas_call(
        _odd_in_proj_kernel,
        grid=(m // tm,),
        in_specs=[pl.BlockSpec((tm, k), lambda i: (i, 0)),
                  pl.BlockSpec(w.shape, lambda i: (0, 0))],
        out_specs=[pl.BlockSpec((tm, wd), lambda i: (i, 0)) for wd in widths],
        out_shape=[jax.ShapeDtypeStruct((m, wd), BF16) for wd in widths],
        compiler_params=_params("parallel"),
        name="odd_in_proj",
    )(h, w)


def _mla_expand_kernel(c_ref, pe_ref, pos_ref, freq_ref, qnorm_ref, kvnorm_ref, wq_ref, wk_ref,
                       wvt_ref, qn_ref, qpe_ref, kn_ref, vt_ref, kpe_ref, *, scale):
    c = c_ref[...].astype(F32)
    cq = _rms(c[:, :C_RANK], qnorm_ref[...]).astype(BF16)
    ckv = _rms(c[:, C_RANK:], kvnorm_ref[...]).astype(BF16)

    q = jnp.dot(cq, wq_ref[...], preferred_element_type=F32)
    kn_ref[...] = jnp.dot(ckv, wk_ref[...], preferred_element_type=F32).astype(kn_ref.dtype)
    vt_ref[...] = lax.dot_general(wvt_ref[...], ckv, NT_DIMS,
                                  preferred_element_type=F32).astype(vt_ref.dtype)

    pos = pos_ref[...].astype(F32)
    pos = jnp.concatenate([_column_tile(pos[g:g + 1]) for g in range(pos.shape[0])], axis=0)
    ang = pos * freq_ref[...]
    cos = jnp.cos(ang)
    sin = jnp.sin(ang)
    lane = _lane_iota()
    first_half = lane < C_ROPE
    low_rot = (lane % C_ROPE) < C_ROPE // 2

    def rope(x):
        rot = jnp.where(low_rot, -pltpu.roll(x, LANES - C_ROPE // 2, axis=1),
                        pltpu.roll(x, C_ROPE // 2, axis=1))
        return x * cos + rot * sin

    for p in range(C_HEADS // 2):
        a, b, c3 = (q[:, (3 * p + k) * LANES:(3 * p + k + 1) * LANES] for k in range(3))
        qn_ref[:, 2 * p * LANES:(2 * p + 1) * LANES] = (a * scale).astype(qn_ref.dtype)
        qn_ref[:, (2 * p + 1) * LANES:(2 * p + 2) * LANES] = (
            _join_halves(b, c3) * scale).astype(qn_ref.dtype)
        pe_pair = jnp.where(first_half, b, c3)
        qpe_ref[:, p * LANES:(p + 1) * LANES] = (rope(pe_pair) * scale).astype(qpe_ref.dtype)

    k_pe = rope(pe_ref[...].astype(F32))
    kpe_ref[...] = jnp.where(first_half, k_pe,
                             pltpu.roll(k_pe, C_ROPE, axis=1)).astype(kpe_ref.dtype)


def mla_expand(c, pe, pos, freq, q_norm, kv_norm, wq, wk, wvt, *, bsz, tm, scale):
    m = c.shape[0]
    t = m // bsz
    nb = t // tm
    row = lambda width: pl.BlockSpec((tm, width), lambda i: (i, 0))
    const = lambda shape: pl.BlockSpec(shape, lambda i: (0, 0))
    n_nope = C_HEADS * C_NOPE
    n_v = C_HEADS * C_V
    widths = [n_nope, C_HEADS * C_ROPE, n_nope, None, LANES]
    out_specs = [row(wd) if wd else pl.BlockSpec((None, n_v, tm), lambda i: (i // nb, 0, i % nb))
                 for wd in widths]
    out_shape = [jax.ShapeDtypeStruct((m, wd) if wd else (bsz, n_v, t), BF16) for wd in widths]
    return pl.pallas_call(
        functools.partial(_mla_expand_kernel, scale=scale),
        grid=(m // tm,),
        in_specs=[row(c.shape[1]), row(pe.shape[1]),
                  pl.BlockSpec((None, tm // LANES, LANES), lambda i: (i, 0, 0)), const((1, LANES)),
                  const((1, C_RANK)), const((1, C_RANK)), const(wq.shape), const(wk.shape),
                  const(wvt.shape)],
        out_specs=out_specs,
        out_shape=out_shape,
        compiler_params=_params("parallel"),
        name="mla_expand",
    )(c, pe, pos, freq, q_norm.reshape(1, C_RANK), kv_norm.reshape(1, C_RANK), wq, wk, wvt)


def _attn_kernel(qn_ref, qpe_ref, kn_ref, kpe_ref, vt_ref, g_ref, o_ref, k_scr, v_scr, *, tq):
    t = qn_ref.shape[0]
    heads = tuple(range(qn_ref.shape[1] // LANES))
    for hh in heads:
        k_scr[hh, :, :LANES] = kn_ref[:, hh * LANES:(hh + 1) * LANES]
        k_scr[hh, :, LANES:] = kpe_ref[...]
        v_scr[hh, :C_V] = vt_ref[hh * C_V:(hh + 1) * C_V]
        v_scr[hh, C_V:] = jnp.ones((v_scr.shape[1] - C_V, t), v_scr.dtype)

    lane = lax.broadcasted_iota(jnp.int32, (1, LANES), 1)
    causal = (lax.broadcasted_iota(jnp.int32, (tq, tq), 0)
              <= lax.broadcasted_iota(jnp.int32, (tq, tq), 1))
    neg = jnp.finfo(BF16).min

    def scores(qi, hh):
        r0 = qi * tq
        end = r0 + tq
        q_pe = qpe_ref[r0:end, (hh // 2) * LANES:(hh // 2 + 1) * LANES]
        q_pe = jnp.where((lane // C_ROPE) == hh % 2, q_pe, jnp.zeros_like(q_pe))
        q = jnp.concatenate([qn_ref[r0:end, hh * LANES:(hh + 1) * LANES], q_pe], axis=1)
        part = end // QK_SPLIT
        return jnp.concatenate(
            [lax.dot_general(k_scr[hh, j * part:(j + 1) * part, :], q, NT_DIMS,
                             preferred_element_type=F32) for j in range(QK_SPLIT)],
            axis=0).astype(BF16)

    def col_max(qi, s):
        r0 = qi * tq
        m = jnp.max(jnp.where(causal, s[r0:], neg), axis=0, keepdims=True)
        if qi > 0:
            m = jnp.maximum(m, jnp.max(s[:r0], axis=0, keepdims=True))
        return m

    def probs(qi, s, m):
        r0 = qi * tq
        p_d = jnp.where(causal, jnp.exp2(s[r0:] - m), jnp.zeros((), BF16))
        p_o = jnp.exp2(s[:r0] - m) if qi > 0 else None
        return p_d, p_o

    def values(qi, hh, p_d, p_o):
        r0 = qi * tq
        end = r0 + tq
        vt = v_scr.at[hh]
        acc = jnp.dot(vt[:, r0:end], p_d, preferred_element_type=F32)
        if qi > 0:
            acc = acc + jnp.dot(vt[:, :r0], p_o, preferred_element_type=F32)
        o = (acc[:C_V] * (1.0 / acc[C_V:C_V + 1])).T
        cols = slice(hh * C_V, (hh + 1) * C_V)
        o_ref[r0:end, cols] = (o * _silu(g_ref[r0:end, cols].astype(F32))).astype(o_ref.dtype)

    nq = t // tq
    s = [scores(0, hh) for hh in heads]
    sm = [col_max(0, s[hh]) for hh in heads]
    for qi in range(nq):
        last = qi + 1 == nq
        for hh in heads:
            s_next = None if last else scores(qi + 1, hh)
            values(qi, hh, *probs(qi, s[hh], sm[hh]))
            if not last:
                s[hh] = s_next
                sm[hh] = col_max(qi + 1, s_next)


def attention(qn, qpe, kn, kpe, vt, gate, *, tq):
    bsz, t, _ = qn.shape
    g = ATTN_HEADS_PER_STEP
    group = lambda width: pl.BlockSpec((None, t, width), lambda b, p: (b, 0, p))
    shared = pl.BlockSpec((None, t, LANES), lambda b, p: (b, 0, 0))
    group_t = pl.BlockSpec((None, g * C_V, t), lambda b, p: (b, p, 0))
    return pl.pallas_call(
        functools.partial(_attn_kernel, tq=tq),
        grid=(bsz, C_HEADS // g),
        in_specs=[group(g * LANES), group(g * C_ROPE), group(g * LANES), shared, group_t,
                  group(g * C_V)],
        out_specs=group(g * C_V),
        out_shape=jax.ShapeDtypeStruct((bsz, t, C_HEADS * C_V), BF16),
        scratch_shapes=[pltpu.VMEM((g, t, 2 * LANES), BF16),
                        pltpu.VMEM((g, C_V + BF16_SUBLANES, t), BF16)],
        compiler_params=_params("parallel", "parallel"),
        name="mla_attention",
    )(qn, qpe, kn, kpe, vt, gate)


def _cast_in_proj_kernel(wt_ref, o_ref):
    n_c = 2 * C_RANK
    o_ref[:n_c + C_ROPE] = wt_ref[:n_c + C_ROPE].astype(o_ref.dtype)
    o_ref[n_c + C_ROPE:n_c + LANES] = jnp.zeros((LANES - C_ROPE, o_ref.shape[1]), o_ref.dtype)
    o_ref[n_c + LANES:] = wt_ref[n_c + C_ROPE:].astype(o_ref.dtype)


def cast_in_proj(wt, *, tk):
    n, k = wt.shape
    n_out = n + LANES - C_ROPE
    return pl.pallas_call(
        _cast_in_proj_kernel,
        grid=(k // tk,),
        in_specs=[pl.BlockSpec((n, tk), lambda i: (0, i))],
        out_specs=pl.BlockSpec((n_out, tk), lambda i: (0, i)),
        out_shape=jax.ShapeDtypeStruct((n_out, k), BF16),
        compiler_params=_params("parallel"),
        name="cast_in_proj",
    )(wt)


def _odd_weights(w_in, w_qb, w_kvb):
    w_in_b = cast_in_proj(w_in.T, tk=512)
    kvb = w_kvb.astype(BF16).reshape(C_RANK, C_HEADS, C_NOPE + C_V)
    wk = kvb[:, :, :C_NOPE].reshape(C_RANK, -1)
    wvt = kvb[:, :, C_NOPE:].reshape(C_RANK, -1).T
    return w_in_b, w_qb.astype(BF16), wk, wvt


def kernel(x, positions, norm_pre, norm_post, ev_w_in, ev_lb_logits, ev_a_onorm, ev_b_ln_w,
           ev_b_ln_b, ev_b_ws, ev_b_bias, ev_w_out, od_w_in, od_q_norm, od_w_qb, od_kv_norm,
           od_w_kvb, od_w_out):
    bsz, t, d = x.shape
    m = bsz * t
    xf = x.reshape(m, d)
    tm_big = min(1024, m)
    tm = min(512, m)

    z = rms_matmul(xf, norm_pre[0].reshape(1, d), ev_w_in[0].astype(BF16), tm=tm_big, tn=1792)
    mix_a, mix_b = even_mixers(z.reshape(bsz, t, -1), ev_lb_logits, ev_a_onorm[0], ev_b_ln_w[0],
                               ev_b_ln_b[0], ev_b_ws[0], ev_b_bias[0], layer_j=0, tb=min(1024, t))
    x1, h1 = out_proj([mix_a.reshape(m, -1), mix_b.reshape(m, -1)], ev_w_out[0].astype(BF16),
                      xf, norm_post[0], norm_pre[1], tm=tm)

    w_in_b, wq_all, wk, wvt = _odd_weights(od_w_in[0], od_w_qb[0], od_w_kvb[0])
    c, pe, gate = odd_in_proj(h1, w_in_b, tm=tm)
    inv_freq = ROPE_THETA ** (-jnp.arange(0, C_ROPE, 2, dtype=F32) / C_ROPE)
    freq = jnp.tile(inv_freq, LANES // (C_ROPE // 2)).reshape(1, LANES)
    scale = float((C_NOPE + C_ROPE) ** -0.5 * LOG2E)
    tm_e = min(tm, t)
    pos = positions.reshape(m // tm_e, tm_e // LANES, LANES)
    qn, qpe, kn, vt, kpe = mla_expand(c, pe, pos, freq, od_q_norm[0], od_kv_norm[0], wq_all, wk,
                                      wvt, bsz=bsz, tm=tm_e, scale=scale)
    r3 = lambda a: a.reshape(bsz, t, -1)
    o = attention(r3(qn), r3(qpe), r3(kn), r3(kpe), vt, r3(gate), tq=min(256, t))
    (x2,) = out_proj([o.reshape(m, -1)], od_w_out[0].astype(BF16), x1, norm_post[1], None, tm=tm)
    return x2.reshape(bsz, t, d)
```

```python
import functools

import jax
import jax.numpy as jnp
from jax import lax
from jax.experimental import pallas as pl
from jax.experimental.pallas import tpu as pltpu

F32 = jnp.float32
BF16 = jnp.bfloat16

EPS = 1e-6
LANES = 128
BF16_SUBLANES = 16
VMEM_LIMIT_BYTES = 56 * 1024 * 1024

A_HEADS = 8
A_DK = 128
A_CHUNK = 64
A_SUB = 16
B_GROUPS = 8
B_DIM = 128
B_CHUNK = 128
C_HEADS = 16
C_RANK = 512
C_NOPE = 128
C_ROPE = 64
C_V = 128
ROPE_THETA = 10000.0
LOG2E = 1.4426950408889634
QK_SPLIT = 1
ATTN_HEADS_PER_STEP = 2

NT_DIMS = (((1,), (1,)), ((), ()))
TN_DIMS = (((0,), (0,)), ((), ()))


def _params(*sem):
    return pltpu.CompilerParams(dimension_semantics=sem, vmem_limit_bytes=VMEM_LIMIT_BYTES)


def _rms(x, w):
    ms = jnp.mean(x * x, axis=-1, keepdims=True)
    return x * lax.rsqrt(ms + EPS) * w


def _silu(x):
    return x * (1.0 / (1.0 + jnp.exp(-x)))


def _column_tile(row):
    return jnp.broadcast_to(row, (LANES, LANES)).T


def _rms_matmul_kernel(x_ref, g_ref, w_ref, o_ref, h_ref):
    @pl.when(pl.program_id(1) == 0)
    def _():
        h_ref[...] = _rms(x_ref[...], g_ref[...]).astype(BF16)

    o_ref[...] = jnp.dot(h_ref[...], w_ref[...], preferred_element_type=F32).astype(o_ref.dtype)


def rms_matmul(x, g, w, *, tm, tn):
    m, k = x.shape
    n = w.shape[1]
    return pl.pallas_call(
        _rms_matmul_kernel,
        grid=(m // tm, n // tn),
        in_specs=[
            pl.BlockSpec((tm, k), lambda i, j: (i, 0)),
            pl.BlockSpec((1, k), lambda i, j: (0, 0)),
            pl.BlockSpec((k, tn), lambda i, j: (0, j)),
        ],
        out_specs=pl.BlockSpec((tm, tn), lambda i, j: (i, j)),
        out_shape=jax.ShapeDtypeStruct((m, n), BF16),
        scratch_shapes=[pltpu.VMEM((tm, k), BF16)],
        compiler_params=_params("parallel", "arbitrary"),
        name="rms_matmul",
    )(x, g, w)


def _hgrn2_block(q, zf, v, lb, st):
    c, s = A_CHUNK, A_SUB
    m = c // s
    tb = q.shape[0]
    nch = tb // c
    tril_mask = (lax.broadcasted_iota(jnp.int32, (c, c), 1)
                 <= lax.broadcasted_iota(jnp.int32, (c, c), 0))
    tril = jnp.where(tril_mask, 1.0, 0.0).astype(BF16)

    a = jnp.abs(zf)
    t = jnp.exp(-a)
    r = 1.0 / (1.0 + t)
    tr = t * r
    pos = zf >= 0
    sig = jnp.where(pos, r, tr)
    sig_neg = jnp.where(pos, tr, r)
    logf = jnp.log(lb + (1.0 - lb) * sig)
    kk = (1.0 - lb) * sig_neg

    hi = logf.astype(BF16)
    r1 = logf - hi.astype(F32)
    mid = r1.astype(BF16)
    lo = (r1 - mid.astype(F32)).astype(BF16)
    terms = jnp.concatenate([hi, mid, lo], axis=1)
    b_chunks = []
    for n in range(nch):
        y = jnp.dot(tril, terms[n * c:(n + 1) * c], preferred_element_type=F32)
        b_chunks.append((y[:, :A_DK] + y[:, A_DK:2 * A_DK]) + y[:, 2 * A_DK:])
    b = jnp.concatenate(b_chunks, axis=0)

    shape4 = (nch, m, s, A_DK)
    b4 = b.reshape(shape4)
    g4 = (b4 - logf.reshape(shape4))[:, :, 0:1, :]
    e4 = b4[:, :, s - 1:s, :]
    bl4 = e4[:, m - 1:m]
    lc4 = b4 - g4
    kk4 = kk.reshape(shape4)
    q_sub = q.reshape(shape4) * jnp.exp(lc4)
    k_diag = (kk4 * jnp.exp(-lc4)).astype(BF16)
    k_end = kk4 * jnp.exp(e4 - b4)
    q_in = (q_sub * jnp.exp(g4)).astype(BF16)
    k_in = (k_end * jnp.exp(bl4 - e4)).astype(BF16)
    q_sub = q_sub.astype(BF16)
    decay = jnp.exp(bl4)
    k_cross = {(i, j): (k_end[:, j] * jnp.exp(g4[:, i] - e4[:, j])).astype(BF16)
               for i in range(m) for j in range(i - 1)}
    k_end = k_end.astype(BF16)
    vb = v.astype(BF16)
    zeros = jnp.zeros((s, A_DK), BF16)

    o_intra, kv = [], []
    for n in range(nch):
        score_rows = []
        for i in range(m):
            parts = [k_end[n, j] if j == i - 1 else k_cross[(i, j)][n] for j in range(i)]
            parts.append(k_diag[n, i])
            parts.extend([zeros] * (m - 1 - i))
            k_i = jnp.concatenate(parts, axis=0)
            score_rows.append(lax.dot_general(q_sub[n, i], k_i, NT_DIMS,
                                              preferred_element_type=F32))
        scores = jnp.concatenate(score_rows, axis=0)
        scores = jnp.where(tril_mask, scores, 0.0).astype(BF16)
        vb_n = vb[n * c:(n + 1) * c]
        o_intra.append(jnp.dot(scores, vb_n, preferred_element_type=F32))
        kv.append(lax.dot_general(vb_n, k_in[n].reshape(c, A_DK), TN_DIMS,
                                  preferred_element_type=F32))

    states = []
    for n in range(nch):
        states.append(st.astype(BF16))
        st = st * decay[n, 0] + kv[n]
    outs = [o_intra[n] + lax.dot_general(q_in[n].reshape(c, A_DK), states[n], NT_DIMS,
                                         preferred_element_type=F32) for n in range(nch)]
    return jnp.concatenate(outs, axis=0), st


def _even_mixer_kernel(q_ref, f_ref, i_ref, ga_ref, u_ref, v_ref, gb_ref,
                       lbl_ref, onorm_ref, lnw_ref, lnb_ref, ws_ref, bias_ref,
                       oa_ref, ob_ref, st_ref, *, layer_j):
    tb = q_ref.shape[0]

    @pl.when(pl.program_id(2) == 0)
    def _():
        st_ref[...] = jnp.zeros_like(st_ref)

    logits = lbl_ref[...]
    e = jnp.exp(logits - jnp.max(logits, axis=0, keepdims=True))
    lb = jnp.sum(e[:layer_j + 1], axis=0, keepdims=True) / jnp.sum(e, axis=0, keepdims=True)

    o, st = _hgrn2_block(q_ref[...].astype(F32), f_ref[...].astype(F32), i_ref[...].astype(F32),
                         lb, st_ref[...])
    st_ref[...] = st
    oa_ref[...] = (_rms(o, onorm_ref[...]) * _silu(ga_ref[...].astype(F32))).astype(oa_ref.dtype)

    cb = B_CHUNK
    w_causal = jnp.where(lax.broadcasted_iota(jnp.int32, (cb, cb), 1)
                         <= lax.broadcasted_iota(jnp.int32, (cb, cb), 0),
                         ws_ref[...], 0.0).astype(BF16)
    vv = v_ref[...].astype(F32)
    mu = jnp.mean(vv, axis=-1, keepdims=True)
    xc = vv - mu
    vg = (xc * lax.rsqrt(jnp.mean(xc * xc, axis=-1, keepdims=True) + EPS) * lnw_ref[...]
          + lnb_ref[...]).astype(BF16)
    sv = jnp.concatenate([jnp.dot(w_causal, vg[n * cb:(n + 1) * cb], preferred_element_type=F32)
                          for n in range(tb // cb)], axis=0)
    bias = _column_tile(bias_ref[...])
    sv = (sv.reshape(tb // cb, cb, B_DIM) + bias).reshape(tb, B_DIM)
    ob_ref[...] = (u_ref[...].astype(F32) * sv * _silu(gb_ref[...].astype(F32))).astype(ob_ref.dtype)


def even_mixers(z, lb_logits, a_onorm, ln_w, ln_b, ws, bias, *, layer_j, tb):
    bsz, t, _ = z.shape
    h = A_HEADS
    n_rows = lb_logits.shape[0]
    lbl = lb_logits.reshape(n_rows, h, A_DK).transpose(1, 0, 2)

    def sec(k):
        return pl.BlockSpec((None, tb, LANES), lambda b, j, s, k=k: (b, s, k * h + j))

    per_head = lambda shape: pl.BlockSpec((None,) + shape, lambda b, j, s: (j, 0, 0))
    out_spec = pl.BlockSpec((None, tb, LANES), lambda b, j, s: (b, s, j))
    out_sds = jax.ShapeDtypeStruct((bsz, t, h * LANES), BF16)
    return pl.pallas_call(
        functools.partial(_even_mixer_kernel, layer_j=layer_j),
        grid=(bsz, h, t // tb),
        in_specs=[sec(k) for k in range(7)] + [
            per_head((n_rows, LANES)),
            pl.BlockSpec((1, LANES), lambda b, j, s: (0, 0)),
            per_head((1, LANES)),
            per_head((1, LANES)),
            per_head((B_CHUNK, B_CHUNK)),
            per_head((1, B_CHUNK)),
        ],
        out_specs=[out_spec, out_spec],
        out_shape=[out_sds, out_sds],
        scratch_shapes=[pltpu.VMEM((A_DK, A_DK), F32)],
        compiler_params=_params("parallel", "parallel", "arbitrary"),
        name="even_mixers",
    )(z, z, z, z, z, z, z, lbl, a_onorm.reshape(1, LANES),
      ln_w.reshape(B_GROUPS, 1, B_DIM), ln_b.reshape(B_GROUPS, 1, B_DIM),
      ws, bias.reshape(B_GROUPS, 1, B_CHUNK))


def _out_proj_kernel(*refs, n_in, with_next):
    a_refs = refs[:n_in]
    w_refs = refs[n_in:2 * n_in]
    x_ref, post_ref = refs[2 * n_in:2 * n_in + 2]
    rest = refs[2 * n_in + 2:]
    if with_next:
        pre_ref, xo_ref, ho_ref = rest
    else:
        (xo_ref,) = rest
    tm = x_ref.shape[0]
    n_sub = 2 if tm % 32 == 0 else 1
    sub = tm // n_sub

    def project(k):
        rows = slice(k * sub, (k + 1) * sub)
        y = jnp.dot(a_refs[0][rows, :], w_refs[0][...], preferred_element_type=F32)
        for a_ref, w_ref in zip(a_refs[1:], w_refs[1:]):
            y = y + jnp.dot(a_ref[rows, :], w_ref[...], preferred_element_type=F32)
        return y

    def finish(k, y):
        rows = slice(k * sub, (k + 1) * sub)
        x_new = x_ref[rows, :] + _rms(y, post_ref[...])
        xo_ref[rows, :] = x_new
        if with_next:
            ho_ref[rows, :] = _rms(x_new, pre_ref[...]).astype(BF16)

    ys = [project(0)]
    for k in range(n_sub):
        if k + 1 < n_sub:
            ys.append(project(k + 1))
        finish(k, ys[k])


def out_proj(acts, w, x, post_w, next_pre_w, *, tm):
    m, d = x.shape
    n_in = len(acts)
    ka = acts[0].shape[1]
    assert all(a.shape[1] == ka for a in acts) and w.shape == (n_in * ka, d)
    with_next = next_pre_w is not None
    row = lambda width: pl.BlockSpec((tm, width), lambda i: (i, 0))
    const = lambda shape: pl.BlockSpec(shape, lambda i: (0, 0))
    in_specs = ([row(ka) for _ in acts]
                + [pl.BlockSpec((ka, d), lambda i, r=r: (r, 0)) for r in range(n_in)]
                + [row(d), const((1, d))])
    args = list(acts) + [w] * n_in + [x, post_w.reshape(1, d)]
    out_specs = [row(d)]
    out_shape = [jax.ShapeDtypeStruct((m, d), F32)]
    if with_next:
        in_specs.append(const((1, d)))
        args.append(next_pre_w.reshape(1, d))
        out_specs.append(row(d))
        out_shape.append(jax.ShapeDtypeStruct((m, d), BF16))
    return pl.pallas_call(
        functools.partial(_out_proj_kernel, n_in=n_in, with_next=with_next),
        grid=(m // tm,),
        in_specs=in_specs,
        out_specs=out_specs,
        out_shape=out_shape,
        compiler_params=_params("parallel"),
        name="out_proj",
    )(*args)


def _lane_iota():
    return lax.broadcasted_iota(jnp.int32, (1, LANES), 1)


def _join_halves(left, right):
    half = LANES // 2
    return pltpu.roll(jnp.where(_lane_iota() < half, right, left), half, axis=1)


def _odd_in_proj_kernel(h_ref, wt_ref, c_ref, pe_ref, gate_ref):
    y = lax.dot_general(h_ref[...], wt_ref[...], NT_DIMS, preferred_element_type=F32)
    n_c = c_ref.shape[1]
    c_ref[...] = y[:, :n_c].astype(c_ref.dtype)
    pe_ref[...] = y[:, n_c:n_c + LANES].astype(pe_ref.dtype)
    gate_ref[...] = y[:, n_c + LANES:].astype(gate_ref.dtype)


def odd_in_proj(h, wt, *, tm):
    m, k = h.shape
    widths = [2 * C_RANK, LANES, C_HEADS * C_V]
    w = wt
    return pl.pallas_call(
        _odd_in_proj_kernel,
        grid=(m // tm,),
        in_specs=[pl.BlockSpec((tm, k), lambda i: (i, 0)),
                  pl.BlockSpec(w.shape, lambda i: (0, 0))],
        out_specs=[pl.BlockSpec((tm, wd), lambda i: (i, 0)) for wd in widths],
        out_shape=[jax.ShapeDtypeStruct((m, wd), BF16) for wd in widths],
        compiler_params=_params("parallel"),
        name="odd_in_proj",
    )(h, w)


def _mla_expand_kernel(c_ref, pe_ref, pos_ref, freq_ref, qnorm_ref, kvnorm_ref, wq_ref, wk_ref,
                       wvt_ref, qn_ref, qpe_ref, kn_ref, vt_ref, kpe_ref, *, scale):
    c = c_ref[...].astype(F32)
    cq = _rms(c[:, :C_RANK], qnorm_ref[...]).astype(BF16)
    ckv = _rms(c[:, C_RANK:], kvnorm_ref[...]).astype(BF16)

    q = jnp.dot(cq, wq_ref[...], preferred_element_type=F32)
    kn_ref[...] = jnp.dot(ckv, wk_ref[...], preferred_element_type=F32).astype(kn_ref.dtype)
    vt_ref[...] = lax.dot_general(wvt_ref[...], ckv, NT_DIMS,
                                  preferred_element_type=F32).astype(vt_ref.dtype)

    pos = pos_ref[...].astype(F32)
    pos = jnp.concatenate([_column_tile(pos[g:g + 1]) for g in range(pos.shape[0])], axis=0)
    ang = pos * freq_ref[...]
    cos = jnp.cos(ang)
    sin = jnp.sin(ang)
    lane = _lane_iota()
    first_half = lane < C_ROPE
    low_rot = (lane % C_ROPE) < C_ROPE // 2

    def rope(x):
        rot = jnp.where(low_rot, -pltpu.roll(x, LANES - C_ROPE // 2, axis=1),
                        pltpu.roll(x, C_ROPE // 2, axis=1))
        return x * cos + rot * sin

    for p in range(C_HEADS // 2):
        a, b, c3 = (q[:, (3 * p + k) * LANES:(3 * p + k + 1) * LANES] for k in range(3))
        qn_ref[:, 2 * p * LANES:(2 * p + 1) * LANES] = (a * scale).astype(qn_ref.dtype)
        qn_ref[:, (2 * p + 1) * LANES:(2 * p + 2) * LANES] = (
            _join_halves(b, c3) * scale).astype(qn_ref.dtype)
        pe_pair = jnp.where(first_half, b, c3)
        qpe_ref[:, p * LANES:(p + 1) * LANES] = (rope(pe_pair) * scale).astype(qpe_ref.dtype)

    k_pe = rope(pe_ref[...].astype(F32))
    kpe_ref[...] = jnp.where(first_half, k_pe,
                             pltpu.roll(k_pe, C_ROPE, axis=1)).astype(kpe_ref.dtype)


def mla_expand(c, pe, pos, freq, q_norm, kv_norm, wq, wk, wvt, *, bsz, tm, scale):
    m = c.shape[0]
    t = m // bsz
    nb = t // tm
    row = lambda width: pl.BlockSpec((tm, width), lambda i: (i, 0))
    const = lambda shape: pl.BlockSpec(shape, lambda i: (0, 0))
    n_nope = C_HEADS * C_NOPE
    n_v = C_HEADS * C_V
    widths = [n_nope, C_HEADS * C_ROPE, n_nope, None, LANES]
    out_specs = [row(wd) if wd else pl.BlockSpec((None, n_v, tm), lambda i: (i // nb, 0, i % nb))
                 for wd in widths]
    out_shape = [jax.ShapeDtypeStruct((m, wd) if wd else (bsz, n_v, t), BF16) for wd in widths]
    return pl.pallas_call(
        functools.partial(_mla_expand_kernel, scale=scale),
        grid=(m // tm,),
        in_specs=[row(c.shape[1]), row(pe.shape[1]),
                  pl.BlockSpec((None, tm // LANES, LANES), lambda i: (i, 0, 0)), const((1, LANES)),
                  const((1, C_RANK)), const((1, C_RANK)), const(wq.shape), const(wk.shape),
                  const(wvt.shape)],
        out_specs=out_specs,
        out_shape=out_shape,
        compiler_params=_params("parallel"),
        name="mla_expand",
    )(c, pe, pos, freq, q_norm.reshape(1, C_RANK), kv_norm.reshape(1, C_RANK), wq, wk, wvt)


def _attn_kernel(qn_ref, qpe_ref, kn_ref, kpe_ref, vt_ref, g_ref, o_ref, k_scr, v_scr, *, tq):
    t = qn_ref.shape[0]
    heads = tuple(range(qn_ref.shape[1] // LANES))
    for hh in heads:
        k_scr[hh, :, :LANES] = kn_ref[:, hh * LANES:(hh + 1) * LANES]
        k_scr[hh, :, LANES:] = kpe_ref[...]
        v_scr[hh, :C_V] = vt_ref[hh * C_V:(hh + 1) * C_V]
        v_scr[hh, C_V:] = jnp.ones((v_scr.shape[1] - C_V, t), v_scr.dtype)

    lane = lax.broadcasted_iota(jnp.int32, (1, LANES), 1)
    causal = (lax.broadcasted_iota(jnp.int32, (tq, tq), 0)
              <= lax.broadcasted_iota(jnp.int32, (tq, tq), 1))
    neg = jnp.finfo(BF16).min

    def scores(qi, hh):
        r0 = qi * tq
        end = r0 + tq
        q_pe = qpe_ref[r0:end, (hh // 2) * LANES:(hh // 2 + 1) * LANES]
        q_pe = jnp.where((lane // C_ROPE) == hh % 2, q_pe, jnp.zeros_like(q_pe))
        q = jnp.concatenate([qn_ref[r0:end, hh * LANES:(hh + 1) * LANES], q_pe], axis=1)
        part = end // QK_SPLIT
        return jnp.concatenate(
            [lax.dot_general(k_scr[hh, j * part:(j + 1) * part, :], q, NT_DIMS,
                             preferred_element_type=F32) for j in range(QK_SPLIT)],
            axis=0).astype(BF16)

    def col_max(qi, s):
        r0 = qi * tq
        m = jnp.max(jnp.where(causal, s[r0:], neg), axis=0, keepdims=True)
        if qi > 0:
            m = jnp.maximum(m, jnp.max(s[:r0], axis=0, keepdims=True))
        return m

    def probs(qi, s, m):
        r0 = qi * tq
        p_d = jnp.where(causal, jnp.exp2(s[r0:] - m), jnp.zeros((), BF16))
        p_o = jnp.exp2(s[:r0] - m) if qi > 0 else None
        return p_d, p_o

    def values(qi, hh, p_d, p_o):
        r0 = qi * tq
        end = r0 + tq
        vt = v_scr.at[hh]
        acc = jnp.dot(vt[:, r0:end], p_d, preferred_element_type=F32)
        if qi > 0:
            acc = acc + jnp.dot(vt[:, :r0], p_o, preferred_element_type=F32)
        o = (acc[:C_V] * (1.0 / acc[C_V:C_V + 1])).T
        cols = slice(hh * C_V, (hh + 1) * C_V)
        o_ref[r0:end, cols] = (o * _silu(g_ref[r0:end, cols].astype(F32))).astype(o_ref.dtype)

    nq = t // tq
    s = [scores(0, hh) for hh in heads]
    sm = [col_max(0, s[hh]) for hh in heads]
    for qi in range(nq):
        last = qi + 1 == nq
        for hh in heads:
            s_next = None if last else scores(qi + 1, hh)
            values(qi, hh, *probs(qi, s[hh], sm[hh]))
            if not last:
                s[hh] = s_next
                sm[hh] = col_max(qi + 1, s_next)


def attention(qn, qpe, kn, kpe, vt, gate, *, tq):
    bsz, t, _ = qn.shape
    g = ATTN_HEADS_PER_STEP
    group = lambda width: pl.BlockSpec((None, t, width), lambda b, p: (b, 0, p))
    shared = pl.BlockSpec((None, t, LANES), lambda b, p: (b, 0, 0))
    group_t = pl.BlockSpec((None, g * C_V, t), lambda b, p: (b, p, 0))
    return pl.pallas_call(
        functools.partial(_attn_kernel, tq=tq),
        grid=(bsz, C_HEADS // g),
        in_specs=[group(g * LANES), group(g * C_ROPE), group(g * LANES), shared, group_t,
                  group(g * C_V)],
        out_specs=group(g * C_V),
        out_shape=jax.ShapeDtypeStruct((bsz, t, C_HEADS * C_V), BF16),
        scratch_shapes=[pltpu.VMEM((g, t, 2 * LANES), BF16),
                        pltpu.VMEM((g, C_V + BF16_SUBLANES, t), BF16)],
        compiler_params=_params("parallel", "parallel"),
        name="mla_attention",
    )(qn, qpe, kn, kpe, vt, gate)


def _cast_in_proj_kernel(wt_ref, o_ref):
    n_c = 2 * C_RANK
    o_ref[:n_c + C_ROPE] = wt_ref[:n_c + C_ROPE].astype(o_ref.dtype)
    o_ref[n_c + C_ROPE:n_c + LANES] = jnp.zeros((LANES - C_ROPE, o_ref.shape[1]), o_ref.dtype)
    o_ref[n_c + LANES:] = wt_ref[n_c + C_ROPE:].astype(o_ref.dtype)


def cast_in_proj(wt, *, tk):
    n, k = wt.shape
    n_out = n + LANES - C_ROPE
    return pl.pallas_call(
        _cast_in_proj_kernel,
        grid=(k // tk,),
        in_specs=[pl.BlockSpec((n, tk), lambda i: (0, i))],
        out_specs=pl.BlockSpec((n_out, tk), lambda i: (0, i)),
        out_shape=jax.ShapeDtypeStruct((n_out, k), BF16),
        compiler_params=_params("parallel"),
        name="cast_in_proj",
    )(wt)


def _odd_weights(w_in, w_qb, w_kvb):
    w_in_b = cast_in_proj(w_in.T, tk=512)
    kvb = w_kvb.astype(BF16).reshape(C_RANK, C_HEADS, C_NOPE + C_V)
    wk = kvb[:, :, :C_NOPE].reshape(C_RANK, -1)
    wvt = kvb[:, :, C_NOPE:].reshape(C_RANK, -1).T
    return w_in_b, w_qb, wk, wvt


def kernel(x, positions, norm_pre, norm_post, ev_w_in, ev_lb_logits, ev_a_onorm, ev_b_ln_w,
           ev_b_ln_b, ev_b_ws, ev_b_bias, ev_w_out, od_w_in, od_q_norm, od_w_qb, od_kv_norm,
           od_w_kvb, od_w_out):
    bsz, t, d = x.shape
    m = bsz * t
    xf = x.reshape(m, d)
    tm_big = min(1024, m)
    tm = min(512, m)

    z = rms_matmul(xf, norm_pre[0].reshape(1, d), ev_w_in[0], tm=tm_big, tn=1024)
    mix_a, mix_b = even_mixers(z.reshape(bsz, t, -1), ev_lb_logits, ev_a_onorm[0], ev_b_ln_w[0],
                               ev_b_ln_b[0], ev_b_ws[0], ev_b_bias[0], layer_j=0, tb=min(1024, t))
    x1, h1 = out_proj([mix_a.reshape(m, -1), mix_b.reshape(m, -1)], ev_w_out[0],
                      xf, norm_post[0], norm_pre[1], tm=tm)

    w_in_b, wq_all, wk, wvt = _odd_weights(od_w_in[0], od_w_qb[0], od_w_kvb[0])
    c, pe, gate = odd_in_proj(h1, w_in_b, tm=tm)
    inv_freq = ROPE_THETA ** (-jnp.arange(0, C_ROPE, 2, dtype=F32) / C_ROPE)
    freq = jnp.tile(inv_freq, LANES // (C_ROPE // 2)).reshape(1, LANES)
    scale = float((C_NOPE + C_ROPE) ** -0.5 * LOG2E)
    tm_e = min(tm, t)
    pos = positions.reshape(m // tm_e, tm_e // LANES, LANES)
    qn, qpe, kn, vt, kpe = mla_expand(c, pe, pos, freq, od_q_norm[0], od_kv_norm[0], wq_all, wk,
                                      wvt, bsz=bsz, tm=tm_e, scale=scale)
    r3 = lambda a: a.reshape(bsz, t, -1)
    o = attention(r3(qn), r3(qpe), r3(kn), r3(kpe), vt, r3(gate), tq=min(256, t))
    (x2,) = out_proj([o.reshape(m, -1)], od_w_out[0], x1, norm_post[1], None, tm=tm)
    return x2.reshape(bsz, t, d)
```

```python
import functools

import jax
import jax.numpy as jnp
from jax import lax
from jax.experimental import pallas as pl
from jax.experimental.pallas import tpu as pltpu

F32 = jnp.float32
BF16 = jnp.bfloat16

EPS = 1e-6
LANES = 128
BF16_SUBLANES = 16
VMEM_LIMIT_BYTES = 56 * 1024 * 1024
VMEM_LIMIT_BYTES_IN_PROJ = 62 * 1024 * 1024

A_HEADS = 8
A_DK = 128
A_CHUNK = 64
A_SUB = 16
B_GROUPS = 8
B_DIM = 128
B_CHUNK = 128
C_HEADS = 16
C_RANK = 512
C_NOPE = 128
C_ROPE = 64
C_V = 128
ROPE_THETA = 10000.0
LOG2E = 1.4426950408889634
QK_SPLIT = 1
ATTN_HEADS_PER_STEP = 2

NT_DIMS = (((1,), (1,)), ((), ()))
TN_DIMS = (((0,), (0,)), ((), ()))


def _params(*sem, vmem_limit=VMEM_LIMIT_BYTES):
    return pltpu.CompilerParams(dimension_semantics=sem, vmem_limit_bytes=vmem_limit)


def _rms(x, w):
    ms = jnp.mean(x * x, axis=-1, keepdims=True)
    return x * lax.rsqrt(ms + EPS) * w


def _silu(x):
    return x * (1.0 / (1.0 + jnp.exp(-x)))


def _column_tile(row):
    return jnp.broadcast_to(row, (LANES, LANES)).T


def _rms_matmul_kernel(x_ref, g_ref, w_ref, o_ref, h_ref):
    @pl.when(pl.program_id(1) == 0)
    def _():
        h_ref[...] = _rms(x_ref[...], g_ref[...]).astype(BF16)

    o_ref[...] = jnp.dot(h_ref[...], w_ref[...], preferred_element_type=F32).astype(o_ref.dtype)


def rms_matmul(x, g, w, *, tm, tn):
    m, k = x.shape
    n = w.shape[1]
    return pl.pallas_call(
        _rms_matmul_kernel,
        grid=(m // tm, n // tn),
        in_specs=[
            pl.BlockSpec((tm, k), lambda i, j: (i, 0)),
            pl.BlockSpec((1, k), lambda i, j: (0, 0)),
            pl.BlockSpec((k, tn), lambda i, j: (0, j)),
        ],
        out_specs=pl.BlockSpec((tm, tn), lambda i, j: (i, j)),
        out_shape=jax.ShapeDtypeStruct((m, n), BF16),
        scratch_shapes=[pltpu.VMEM((tm, k), BF16)],
        compiler_params=_params("parallel", "arbitrary", vmem_limit=VMEM_LIMIT_BYTES_IN_PROJ),
        name="rms_matmul",
    )(x, g, w)


def _hgrn2_block(q, zf, v, lb, st):
    c, s = A_CHUNK, A_SUB
    m = c // s
    tb = q.shape[0]
    nch = tb // c
    tril_mask = (lax.broadcasted_iota(jnp.int32, (c, c), 1)
                 <= lax.broadcasted_iota(jnp.int32, (c, c), 0))
    tril = jnp.where(tril_mask, 1.0, 0.0).astype(BF16)

    a = jnp.abs(zf)
    t = jnp.exp(-a)
    r = 1.0 / (1.0 + t)
    tr = t * r
    pos = zf >= 0
    sig = jnp.where(pos, r, tr)
    sig_neg = jnp.where(pos, tr, r)
    logf = jnp.log(lb + (1.0 - lb) * sig)
    kk = (1.0 - lb) * sig_neg

    hi = logf.astype(BF16)
    r1 = logf - hi.astype(F32)
    mid = r1.astype(BF16)
    lo = (r1 - mid.astype(F32)).astype(BF16)
    terms = jnp.concatenate([hi, mid, lo], axis=1)
    b_chunks = []
    for n in range(nch):
        y = jnp.dot(tril, terms[n * c:(n + 1) * c], preferred_element_type=F32)
        b_chunks.append((y[:, :A_DK] + y[:, A_DK:2 * A_DK]) + y[:, 2 * A_DK:])
    b = jnp.concatenate(b_chunks, axis=0)

    shape4 = (nch, m, s, A_DK)
    b4 = b.reshape(shape4)
    g4 = (b4 - logf.reshape(shape4))[:, :, 0:1, :]
    e4 = b4[:, :, s - 1:s, :]
    bl4 = e4[:, m - 1:m]
    lc4 = b4 - g4
    kk4 = kk.reshape(shape4)
    q_sub = q.reshape(shape4) * jnp.exp(lc4)
    k_diag = (kk4 * jnp.exp(-lc4)).astype(BF16)
    k_end = kk4 * jnp.exp(e4 - b4)
    q_in = (q_sub * jnp.exp(g4)).astype(BF16)
    k_in = (k_end * jnp.exp(bl4 - e4)).astype(BF16)
    q_sub = q_sub.astype(BF16)
    decay = jnp.exp(bl4)
    k_cross = {(i, j): (k_end[:, j] * jnp.exp(g4[:, i] - e4[:, j])).astype(BF16)
               for i in range(m) for j in range(i - 1)}
    k_end = k_end.astype(BF16)
    vb = v.astype(BF16)
    zeros = jnp.zeros((s, A_DK), BF16)

    o_intra, kv = [], []
    for n in range(nch):
        score_rows = []
        for i in range(m):
            parts = [k_end[n, j] if j == i - 1 else k_cross[(i, j)][n] for j in range(i)]
            parts.append(k_diag[n, i])
            parts.extend([zeros] * (m - 1 - i))
            k_i = jnp.concatenate(parts, axis=0)
            score_rows.append(lax.dot_general(q_sub[n, i], k_i, NT_DIMS,
                                              preferred_element_type=F32))
        scores = jnp.concatenate(score_rows, axis=0)
        scores = jnp.where(tril_mask, scores, 0.0).astype(BF16)
        vb_n = vb[n * c:(n + 1) * c]
        o_intra.append(jnp.dot(scores, vb_n, preferred_element_type=F32))
        kv.append(lax.dot_general(vb_n, k_in[n].reshape(c, A_DK), TN_DIMS,
                                  preferred_element_type=F32))

    states = []
    for n in range(nch):
        states.append(st.astype(BF16))
        st = st * decay[n, 0] + kv[n]
    outs = [o_intra[n] + lax.dot_general(q_in[n].reshape(c, A_DK), states[n], NT_DIMS,
                                         preferred_element_type=F32) for n in range(nch)]
    return jnp.concatenate(outs, axis=0), st


def _even_mixer_kernel(q_ref, f_ref, i_ref, ga_ref, u_ref, v_ref, gb_ref,
                       lbl_ref, onorm_ref, lnw_ref, lnb_ref, ws_ref, bias_ref,
                       oa_ref, ob_ref, st_ref, *, layer_j):
    tb = q_ref.shape[0]

    @pl.when(pl.program_id(2) == 0)
    def _():
        st_ref[...] = jnp.zeros_like(st_ref)

    logits = lbl_ref[...]
    e = jnp.exp(logits - jnp.max(logits, axis=0, keepdims=True))
    lb = jnp.sum(e[:layer_j + 1], axis=0, keepdims=True) / jnp.sum(e, axis=0, keepdims=True)

    o, st = _hgrn2_block(q_ref[...].astype(F32), f_ref[...].astype(F32), i_ref[...].astype(F32),
                         lb, st_ref[...])
    st_ref[...] = st
    oa_ref[...] = (_rms(o, onorm_ref[...]) * _silu(ga_ref[...].astype(F32))).astype(oa_ref.dtype)

    cb = B_CHUNK
    w_causal = jnp.where(lax.broadcasted_iota(jnp.int32, (cb, cb), 1)
                         <= lax.broadcasted_iota(jnp.int32, (cb, cb), 0),
                         ws_ref[...], 0.0).astype(BF16)
    vv = v_ref[...].astype(F32)
    mu = jnp.mean(vv, axis=-1, keepdims=True)
    xc = vv - mu
    vg = (xc * lax.rsqrt(jnp.mean(xc * xc, axis=-1, keepdims=True) + EPS) * lnw_ref[...]
          + lnb_ref[...]).astype(BF16)
    sv = jnp.concatenate([jnp.dot(w_causal, vg[n * cb:(n + 1) * cb], preferred_element_type=F32)
                          for n in range(tb // cb)], axis=0)
    bias = _column_tile(bias_ref[...])
    sv = (sv.reshape(tb // cb, cb, B_DIM) + bias).reshape(tb, B_DIM)
    ob_ref[...] = (u_ref[...].astype(F32) * sv * _silu(gb_ref[...].astype(F32))).astype(ob_ref.dtype)


def even_mixers(z, lb_logits, a_onorm, ln_w, ln_b, ws, bias, *, layer_j, tb):
    bsz, t, _ = z.shape
    h = A_HEADS
    n_rows = lb_logits.shape[0]
    lbl = lb_logits.reshape(n_rows, h, A_DK).transpose(1, 0, 2)

    def sec(k):
        return pl.BlockSpec((None, tb, LANES), lambda b, j, s, k=k: (b, s, k * h + j))

    per_head = lambda shape: pl.BlockSpec((None,) + shape, lambda b, j, s: (j, 0, 0))
    out_spec = pl.BlockSpec((None, tb, LANES), lambda b, j, s: (b, s, j))
    out_sds = jax.ShapeDtypeStruct((bsz, t, h * LANES), BF16)
    return pl.pallas_call(
        functools.partial(_even_mixer_kernel, layer_j=layer_j),
        grid=(bsz, h, t // tb),
        in_specs=[sec(k) for k in range(7)] + [
            per_head((n_rows, LANES)),
            pl.BlockSpec((1, LANES), lambda b, j, s: (0, 0)),
            per_head((1, LANES)),
            per_head((1, LANES)),
            per_head((B_CHUNK, B_CHUNK)),
            per_head((1, B_CHUNK)),
        ],
        out_specs=[out_spec, out_spec],
        out_shape=[out_sds, out_sds],
        scratch_shapes=[pltpu.VMEM((A_DK, A_DK), F32)],
        compiler_params=_params("parallel", "parallel", "arbitrary"),
        name="even_mixers",
    )(z, z, z, z, z, z, z, lbl, a_onorm.reshape(1, LANES),
      ln_w.reshape(B_GROUPS, 1, B_DIM), ln_b.reshape(B_GROUPS, 1, B_DIM),
      ws, bias.reshape(B_GROUPS, 1, B_CHUNK))


def _out_proj_kernel(*refs, n_in, with_next):
    a_refs = refs[:n_in]
    w_refs = refs[n_in:2 * n_in]
    x_ref, post_ref = refs[2 * n_in:2 * n_in + 2]
    rest = refs[2 * n_in + 2:]
    if with_next:
        pre_ref, xo_ref, ho_ref = rest
    else:
        (xo_ref,) = rest
    tm = x_ref.shape[0]
    n_sub = 2 if tm % 32 == 0 else 1
    sub = tm // n_sub

    def project(k):
        rows = slice(k * sub, (k + 1) * sub)
        y = jnp.dot(a_refs[0][rows, :], w_refs[0][...], preferred_element_type=F32)
        for a_ref, w_ref in zip(a_refs[1:], w_refs[1:]):
            y = y + jnp.dot(a_ref[rows, :], w_ref[...], preferred_element_type=F32)
        return y

    def finish(k, y):
        rows = slice(k * sub, (k + 1) * sub)
        x_new = x_ref[rows, :] + _rms(y, post_ref[...])
        xo_ref[rows, :] = x_new
        if with_next:
            ho_ref[rows, :] = _rms(x_new, pre_ref[...]).astype(BF16)

    ys = [project(0)]
    for k in range(n_sub):
        if k + 1 < n_sub:
            ys.append(project(k + 1))
        finish(k, ys[k])


def out_proj(acts, w, x, post_w, next_pre_w, *, tm):
    m, d = x.shape
    n_in = len(acts)
    ka = acts[0].shape[1]
    assert all(a.shape[1] == ka for a in acts) and w.shape == (n_in * ka, d)
    with_next = next_pre_w is not None
    row = lambda width: pl.BlockSpec((tm, width), lambda i: (i, 0))
    const = lambda shape: pl.BlockSpec(shape, lambda i: (0, 0))
    in_specs = ([row(ka) for _ in acts]
                + [pl.BlockSpec((ka, d), lambda i, r=r: (r, 0)) for r in range(n_in)]
                + [row(d), const((1, d))])
    args = list(acts) + [w] * n_in + [x, post_w.reshape(1, d)]
    out_specs = [row(d)]
    out_shape = [jax.ShapeDtypeStruct((m, d), F32)]
    if with_next:
        in_specs.append(const((1, d)))
        args.append(next_pre_w.reshape(1, d))
        out_specs.append(row(d))
        out_shape.append(jax.ShapeDtypeStruct((m, d), BF16))
    return pl.pallas_call(
        functools.partial(_out_proj_kernel, n_in=n_in, with_next=with_next),
        grid=(m // tm,),
        in_specs=in_specs,
        out_specs=out_specs,
        out_shape=out_shape,
        compiler_params=_params("parallel"),
        name="out_proj",
    )(*args)


def _lane_iota():
    return lax.broadcasted_iota(jnp.int32, (1, LANES), 1)


def _join_halves(left, right):
    half = LANES // 2
    return pltpu.roll(jnp.where(_lane_iota() < half, right, left), half, axis=1)


def _odd_in_proj_kernel(h_ref, wt_ref, c_ref, pe_ref, gate_ref):
    y = lax.dot_general(h_ref[...], wt_ref[...], NT_DIMS, preferred_element_type=F32)
    n_c = c_ref.shape[1]
    c_ref[...] = y[:, :n_c].astype(c_ref.dtype)
    pe_ref[...] = y[:, n_c:n_c + LANES].astype(pe_ref.dtype)
    gate_ref[...] = y[:, n_c + LANES:].astype(gate_ref.dtype)


def odd_in_proj(h, wt, *, tm):
    m, k = h.shape
    widths = [2 * C_RANK, LANES, C_HEADS * C_V]
    w = wt
    return pl.pallas_call(
        _odd_in_proj_kernel,
        grid=(m // tm,),
        in_specs=[pl.BlockSpec((tm, k), lambda i: (i, 0)),
                  pl.BlockSpec(w.shape, lambda i: (0, 0))],
        out_specs=[pl.BlockSpec((tm, wd), lambda i: (i, 0)) for wd in widths],
        out_shape=[jax.ShapeDtypeStruct((m, wd), BF16) for wd in widths],
        compiler_params=_params("parallel"),
        name="odd_in_proj",
    )(h, w)


def _mla_expand_kernel(c_ref, pe_ref, pos_ref, freq_ref, qnorm_ref, kvnorm_ref, wq_ref, wk_ref,
                       wvt_ref, qn_ref, qpe_ref, kn_ref, vt_ref, kpe_ref, *, scale):
    c = c_ref[...].astype(F32)
    cq = _rms(c[:, :C_RANK], qnorm_ref[...]).astype(BF16)
    ckv = _rms(c[:, C_RANK:], kvnorm_ref[...]).astype(BF16)

    q = jnp.dot(cq, wq_ref[...], preferred_element_type=F32)
    kn_ref[...] = jnp.dot(ckv, wk_ref[...], preferred_element_type=F32).astype(kn_ref.dtype)
    vt_ref[...] = lax.dot_general(wvt_ref[...], ckv, NT_DIMS,
                                  preferred_element_type=F32).astype(vt_ref.dtype)

    pos = pos_ref[...].astype(F32)
    pos = jnp.concatenate([_column_tile(pos[g:g + 1]) for g in range(pos.shape[0])], axis=0)
    ang = pos * freq_ref[...]
    cos = jnp.cos(ang)
    sin = jnp.sin(ang)
    lane = _lane_iota()
    first_half = lane < C_ROPE
    low_rot = (lane % C_ROPE) < C_ROPE // 2

    def rope(x):
        rot = jnp.where(low_rot, -pltpu.roll(x, LANES - C_ROPE // 2, axis=1),
                        pltpu.roll(x, C_ROPE // 2, axis=1))
        return x * cos + rot * sin

    for p in range(C_HEADS // 2):
        a, b, c3 = (q[:, (3 * p + k) * LANES:(3 * p + k + 1) * LANES] for k in range(3))
        qn_ref[:, 2 * p * LANES:(2 * p + 1) * LANES] = (a * scale).astype(qn_ref.dtype)
        qn_ref[:, (2 * p + 1) * LANES:(2 * p + 2) * LANES] = (
            _join_halves(b, c3) * scale).astype(qn_ref.dtype)
        pe_pair = jnp.where(first_half, b, c3)
        qpe_ref[:, p * LANES:(p + 1) * LANES] = (rope(pe_pair) * scale).astype(qpe_ref.dtype)

    k_pe = rope(pe_ref[...].astype(F32))
    kpe_ref[...] = jnp.where(first_half, k_pe,
                             pltpu.roll(k_pe, C_ROPE, axis=1)).astype(kpe_ref.dtype)


def mla_expand(c, pe, pos, freq, q_norm, kv_norm, wq, wk, wvt, *, bsz, tm, scale):
    m = c.shape[0]
    t = m // bsz
    nb = t // tm
    row = lambda width: pl.BlockSpec((tm, width), lambda i: (i, 0))
    const = lambda shape: pl.BlockSpec(shape, lambda i: (0, 0))
    n_nope = C_HEADS * C_NOPE
    n_v = C_HEADS * C_V
    widths = [n_nope, C_HEADS * C_ROPE, n_nope, None, LANES]
    out_specs = [row(wd) if wd else pl.BlockSpec((None, n_v, tm), lambda i: (i // nb, 0, i % nb))
                 for wd in widths]
    out_shape = [jax.ShapeDtypeStruct((m, wd) if wd else (bsz, n_v, t), BF16) for wd in widths]
    return pl.pallas_call(
        functools.partial(_mla_expand_kernel, scale=scale),
        grid=(m // tm,),
        in_specs=[row(c.shape[1]), row(pe.shape[1]),
                  pl.BlockSpec((None, tm // LANES, LANES), lambda i: (i, 0, 0)), const((1, LANES)),
                  const((1, C_RANK)), const((1, C_RANK)), const(wq.shape), const(wk.shape),
                  const(wvt.shape)],
        out_specs=out_specs,
        out_shape=out_shape,
        compiler_params=_params("parallel"),
        name="mla_expand",
    )(c, pe, pos, freq, q_norm.reshape(1, C_RANK), kv_norm.reshape(1, C_RANK), wq, wk, wvt)


def _attn_kernel(qn_ref, qpe_ref, kn_ref, kpe_ref, vt_ref, g_ref, o_ref, k_scr, v_scr, *, tq):
    t = qn_ref.shape[0]
    heads = tuple(range(qn_ref.shape[1] // LANES))
    for hh in heads:
        k_scr[hh, :, :LANES] = kn_ref[:, hh * LANES:(hh + 1) * LANES]
        k_scr[hh, :, LANES:] = kpe_ref[...]
        v_scr[hh, :C_V] = vt_ref[hh * C_V:(hh + 1) * C_V]
        v_scr[hh, C_V:] = jnp.ones((v_scr.shape[1] - C_V, t), v_scr.dtype)

    lane = lax.broadcasted_iota(jnp.int32, (1, LANES), 1)
    causal = (lax.broadcasted_iota(jnp.int32, (tq, tq), 0)
              <= lax.broadcasted_iota(jnp.int32, (tq, tq), 1))
    neg = jnp.finfo(BF16).min

    def scores(qi, hh):
        r0 = qi * tq
        end = r0 + tq
        q_pe = qpe_ref[r0:end, (hh // 2) * LANES:(hh // 2 + 1) * LANES]
        q_pe = jnp.where((lane // C_ROPE) == hh % 2, q_pe, jnp.zeros_like(q_pe))
        q = jnp.concatenate([qn_ref[r0:end, hh * LANES:(hh + 1) * LANES], q_pe], axis=1)
        part = end // QK_SPLIT
        return jnp.concatenate(
            [lax.dot_general(k_scr[hh, j * part:(j + 1) * part, :], q, NT_DIMS,
                             preferred_element_type=F32) for j in range(QK_SPLIT)],
            axis=0).astype(BF16)

    def col_max(qi, s):
        r0 = qi * tq
        m = jnp.max(jnp.where(causal, s[r0:], neg), axis=0, keepdims=True)
        if qi > 0:
            m = jnp.maximum(m, jnp.max(s[:r0], axis=0, keepdims=True))
        return m

    def probs(qi, s, m):
        r0 = qi * tq
        p_d = jnp.where(causal, jnp.exp2(s[r0:] - m), jnp.zeros((), BF16))
        p_o = jnp.exp2(s[:r0] - m) if qi > 0 else None
        return p_d, p_o

    def values(qi, hh, p_d, p_o):
        r0 = qi * tq
        end = r0 + tq
        vt = v_scr.at[hh]
        acc = jnp.dot(vt[:, r0:end], p_d, preferred_element_type=F32)
        if qi > 0:
            acc = acc + jnp.dot(vt[:, :r0], p_o, preferred_element_type=F32)
        o = (acc[:C_V] * (1.0 / acc[C_V:C_V + 1])).T
        cols = slice(hh * C_V, (hh + 1) * C_V)
        o_ref[r0:end, cols] = (o * _silu(g_ref[r0:end, cols].astype(F32))).astype(o_ref.dtype)

    nq = t // tq
    s = [scores(0, hh) for hh in heads]
    sm = [col_max(0, s[hh]) for hh in heads]
    for qi in range(nq):
        last = qi + 1 == nq
        for hh in heads:
            s_next = None if last else scores(qi + 1, hh)
            values(qi, hh, *probs(qi, s[hh], sm[hh]))
            if not last:
                s[hh] = s_next
                sm[hh] = col_max(qi + 1, s_next)


def attention(qn, qpe, kn, kpe, vt, gate, *, tq):
    bsz, t, _ = qn.shape
    g = ATTN_HEADS_PER_STEP
    group = lambda width: pl.BlockSpec((None, t, width), lambda b, p: (b, 0, p))
    shared = pl.BlockSpec((None, t, LANES), lambda b, p: (b, 0, 0))
    group_t = pl.BlockSpec((None, g * C_V, t), lambda b, p: (b, p, 0))
    return pl.pallas_call(
        functools.partial(_attn_kernel, tq=tq),
        grid=(bsz, C_HEADS // g),
        in_specs=[group(g * LANES), group(g * C_ROPE), group(g * LANES), shared, group_t,
                  group(g * C_V)],
        out_specs=group(g * C_V),
        out_shape=jax.ShapeDtypeStruct((bsz, t, C_HEADS * C_V), BF16),
        scratch_shapes=[pltpu.VMEM((g, t, 2 * LANES), BF16),
                        pltpu.VMEM((g, C_V + BF16_SUBLANES, t), BF16)],
        compiler_params=_params("parallel", "parallel"),
        name="mla_attention",
    )(qn, qpe, kn, kpe, vt, gate)


def _cast_in_proj_kernel(wt_ref, o_ref):
    n_c = 2 * C_RANK
    o_ref[:n_c + C_ROPE] = wt_ref[:n_c + C_ROPE].astype(o_ref.dtype)
    o_ref[n_c + C_ROPE:n_c + LANES] = jnp.zeros((LANES - C_ROPE, o_ref.shape[1]), o_ref.dtype)
    o_ref[n_c + LANES:] = wt_ref[n_c + C_ROPE:].astype(o_ref.dtype)


def cast_in_proj(wt, *, tk):
    n, k = wt.shape
    n_out = n + LANES - C_ROPE
    return pl.pallas_call(
        _cast_in_proj_kernel,
        grid=(k // tk,),
        in_specs=[pl.BlockSpec((n, tk), lambda i: (0, i))],
        out_specs=pl.BlockSpec((n_out, tk), lambda i: (0, i)),
        out_shape=jax.ShapeDtypeStruct((n_out, k), BF16),
        compiler_params=_params("parallel"),
        name="cast_in_proj",
    )(wt)


def _odd_weights(w_in, w_qb, w_kvb):
    w_in_b = cast_in_proj(w_in.T, tk=512)
    kvb = w_kvb.astype(BF16).reshape(C_RANK, C_HEADS, C_NOPE + C_V)
    wk = kvb[:, :, :C_NOPE].reshape(C_RANK, -1)
    wvt = kvb[:, :, C_NOPE:].reshape(C_RANK, -1).T
    return w_in_b, w_qb, wk, wvt


def kernel(x, positions, norm_pre, norm_post, ev_w_in, ev_lb_logits, ev_a_onorm, ev_b_ln_w,
           ev_b_ln_b, ev_b_ws, ev_b_bias, ev_w_out, od_w_in, od_q_norm, od_w_qb, od_kv_norm,
           od_w_kvb, od_w_out):
    bsz, t, d = x.shape
    m = bsz * t
    xf = x.reshape(m, d)
    tm_big = min(1024, m)
    tm = min(512, m)

    z = rms_matmul(xf, norm_pre[0].reshape(1, d), ev_w_in[0], tm=tm_big, tn=1792)
    mix_a, mix_b = even_mixers(z.reshape(bsz, t, -1), ev_lb_logits, ev_a_onorm[0], ev_b_ln_w[0],
                               ev_b_ln_b[0], ev_b_ws[0], ev_b_bias[0], layer_j=0, tb=min(1024, t))
    x1, h1 = out_proj([mix_a.reshape(m, -1), mix_b.reshape(m, -1)], ev_w_out[0],
                      xf, norm_post[0], norm_pre[1], tm=tm)

    w_in_b, wq_all, wk, wvt = _odd_weights(od_w_in[0], od_w_qb[0], od_w_kvb[0])
    c, pe, gate = odd_in_proj(h1, w_in_b, tm=tm)
    inv_freq = ROPE_THETA ** (-jnp.arange(0, C_ROPE, 2, dtype=F32) / C_ROPE)
    freq = jnp.tile(inv_freq, LANES // (C_ROPE // 2)).reshape(1, LANES)
    scale = float((C_NOPE + C_ROPE) ** -0.5 * LOG2E)
    tm_e = min(tm, t)
    pos = positions.reshape(m // tm_e, tm_e // LANES, LANES)
    qn, qpe, kn, vt, kpe = mla_expand(c, pe, pos, freq, od_q_norm[0], od_kv_norm[0], wq_all, wk,
                                      wvt, bsz=bsz, tm=tm_e, scale=scale)
    r3 = lambda a: a.reshape(bsz, t, -1)
    o = attention(r3(qn), r3(qpe), r3(kn), r3(kpe), vt, r3(gate), tq=min(256, t))
    (x2,) = out_proj([o.reshape(m, -1)], od_w_out[0], x1, norm_post[1], None, tm=tm)
    return x2.reshape(bsz, t, d)
```

```python
import functools

import jax
import jax.numpy as jnp
from jax import lax
from jax.experimental import pallas as pl
from jax.experimental.pallas import tpu as pltpu

F32 = jnp.float32
BF16 = jnp.bfloat16

EPS = 1e-6
LANES = 128
BF16_SUBLANES = 16
VMEM_LIMIT_BYTES = 56 * 1024 * 1024
VMEM_LIMIT_BYTES_IN_PROJ = 62 * 1024 * 1024

A_HEADS = 8
A_DK = 128
A_CHUNK = 64
A_SUB = 16
B_GROUPS = 8
B_DIM = 128
B_CHUNK = 128
C_HEADS = 16
C_RANK = 512
C_NOPE = 128
C_ROPE = 64
C_V = 128
ROPE_THETA = 10000.0
LOG2E = 1.4426950408889634
ATTN_HEADS_PER_STEP = 2

NT_DIMS = (((1,), (1,)), ((), ()))
TN_DIMS = (((0,), (0,)), ((), ()))


def _params(*sem, vmem_limit=VMEM_LIMIT_BYTES):
    return pltpu.CompilerParams(dimension_semantics=sem, vmem_limit_bytes=vmem_limit)


def _rms(x, w):
    ms = jnp.mean(x * x, axis=-1, keepdims=True)
    return x * lax.rsqrt(ms + EPS) * w


def _silu(x):
    return x * (1.0 / (1.0 + jnp.exp(-x)))


def _column_tile(row):
    return jnp.broadcast_to(row, (LANES, LANES)).T


def _rms_matmul_kernel(x_ref, g_ref, w_ref, o_ref, h_ref):
    @pl.when(pl.program_id(1) == 0)
    def _():
        h_ref[...] = _rms(x_ref[...], g_ref[...]).astype(BF16)

    o_ref[...] = jnp.dot(h_ref[...], w_ref[...], preferred_element_type=F32).astype(o_ref.dtype)


def rms_matmul(x, g, w, *, tm, tn):
    m, k = x.shape
    n = w.shape[1]
    return pl.pallas_call(
        _rms_matmul_kernel,
        grid=(m // tm, n // tn),
        in_specs=[
            pl.BlockSpec((tm, k), lambda i, j: (i, 0)),
            pl.BlockSpec((1, k), lambda i, j: (0, 0)),
            pl.BlockSpec((k, tn), lambda i, j: (0, j)),
        ],
        out_specs=pl.BlockSpec((tm, tn), lambda i, j: (i, j)),
        out_shape=jax.ShapeDtypeStruct((m, n), BF16),
        scratch_shapes=[pltpu.VMEM((tm, k), BF16)],
        compiler_params=_params("parallel", "arbitrary", vmem_limit=VMEM_LIMIT_BYTES_IN_PROJ),
        name="rms_matmul",
    )(x, g, w)


def _hgrn2_block(q, zf, v, lb, st):
    c, s = A_CHUNK, A_SUB
    m = c // s
    tb = q.shape[0]
    nch = tb // c
    tril_mask = (lax.broadcasted_iota(jnp.int32, (c, c), 1)
                 <= lax.broadcasted_iota(jnp.int32, (c, c), 0))
    tril = jnp.where(tril_mask, 1.0, 0.0).astype(BF16)

    a = jnp.abs(zf)
    t = jnp.exp(-a)
    r = 1.0 / (1.0 + t)
    tr = t * r
    pos = zf >= 0
    sig = jnp.where(pos, r, tr)
    sig_neg = jnp.where(pos, tr, r)
    logf = jnp.log(lb + (1.0 - lb) * sig)
    kk = (1.0 - lb) * sig_neg

    hi = logf.astype(BF16)
    r1 = logf - hi.astype(F32)
    mid = r1.astype(BF16)
    lo = (r1 - mid.astype(F32)).astype(BF16)
    terms = jnp.concatenate([hi, mid, lo], axis=1)
    b_chunks = []
    for n in range(nch):
        y = jnp.dot(tril, terms[n * c:(n + 1) * c], preferred_element_type=F32)
        b_chunks.append((y[:, :A_DK] + y[:, A_DK:2 * A_DK]) + y[:, 2 * A_DK:])
    b = jnp.concatenate(b_chunks, axis=0)

    shape4 = (nch, m, s, A_DK)
    b4 = b.reshape(shape4)
    g4 = (b4 - logf.reshape(shape4))[:, :, 0:1, :]
    e4 = b4[:, :, s - 1:s, :]
    bl4 = e4[:, m - 1:m]
    lc4 = b4 - g4
    kk4 = kk.reshape(shape4)
    q_sub = q.reshape(shape4) * jnp.exp(lc4)
    k_diag = (kk4 * jnp.exp(-lc4)).astype(BF16)
    k_end = kk4 * jnp.exp(e4 - b4)
    q_in = (q_sub * jnp.exp(g4)).astype(BF16)
    k_in = (k_end * jnp.exp(bl4 - e4)).astype(BF16)
    q_sub = q_sub.astype(BF16)
    decay = jnp.exp(bl4)
    k_cross = {(i, j): (k_end[:, j] * jnp.exp(g4[:, i] - e4[:, j])).astype(BF16)
               for i in range(m) for j in range(i - 1)}
    k_end = k_end.astype(BF16)
    vb = v.astype(BF16)
    zeros = jnp.zeros((s, A_DK), BF16)

    o_intra, kv = [], []
    for n in range(nch):
        score_rows = []
        for i in range(m):
            parts = [k_end[n, j] if j == i - 1 else k_cross[(i, j)][n] for j in range(i)]
            parts.append(k_diag[n, i])
            parts.extend([zeros] * (m - 1 - i))
            k_i = jnp.concatenate(parts, axis=0)
            score_rows.append(lax.dot_general(q_sub[n, i], k_i, NT_DIMS,
                                              preferred_element_type=F32))
        scores = jnp.concatenate(score_rows, axis=0)
        scores = jnp.where(tril_mask, scores, 0.0).astype(BF16)
        vb_n = vb[n * c:(n + 1) * c]
        o_intra.append(jnp.dot(scores, vb_n, preferred_element_type=F32))
        kv.append(lax.dot_general(vb_n, k_in[n].reshape(c, A_DK), TN_DIMS,
                                  preferred_element_type=F32))

    states = []
    for n in range(nch):
        states.append(st.astype(BF16))
        st = st * decay[n, 0] + kv[n]
    outs = [o_intra[n] + lax.dot_general(q_in[n].reshape(c, A_DK), states[n], NT_DIMS,
                                         preferred_element_type=F32) for n in range(nch)]
    return jnp.concatenate(outs, axis=0), st


def _even_mixer_kernel(q_ref, f_ref, i_ref, ga_ref, u_ref, v_ref, gb_ref,
                       lbl_ref, onorm_ref, lnw_ref, lnb_ref, ws_ref, bias_ref,
                       oa_ref, ob_ref, st_ref, *, layer_j):
    tb = q_ref.shape[0]

    @pl.when(pl.program_id(2) == 0)
    def _():
        st_ref[...] = jnp.zeros_like(st_ref)

    logits = lbl_ref[...]
    e = jnp.exp(logits - jnp.max(logits, axis=0, keepdims=True))
    lb = jnp.sum(e[:layer_j + 1], axis=0, keepdims=True) / jnp.sum(e, axis=0, keepdims=True)

    o, st = _hgrn2_block(q_ref[...].astype(F32), f_ref[...].astype(F32), i_ref[...].astype(F32),
                         lb, st_ref[...])
    st_ref[...] = st
    oa_ref[...] = (_rms(o, onorm_ref[...]) * _silu(ga_ref[...].astype(F32))).astype(oa_ref.dtype)

    cb = B_CHUNK
    w_causal = jnp.where(lax.broadcasted_iota(jnp.int32, (cb, cb), 1)
                         <= lax.broadcasted_iota(jnp.int32, (cb, cb), 0),
                         ws_ref[...], 0.0).astype(BF16)
    vv = v_ref[...].astype(F32)
    mu = jnp.mean(vv, axis=-1, keepdims=True)
    xc = vv - mu
    vg = (xc * lax.rsqrt(jnp.mean(xc * xc, axis=-1, keepdims=True) + EPS) * lnw_ref[...]
          + lnb_ref[...]).astype(BF16)
    sv = jnp.concatenate([jnp.dot(w_causal, vg[n * cb:(n + 1) * cb], preferred_element_type=F32)
                          for n in range(tb // cb)], axis=0)
    bias = _column_tile(bias_ref[...])
    sv = (sv.reshape(tb // cb, cb, B_DIM) + bias).reshape(tb, B_DIM)
    ob_ref[...] = (u_ref[...].astype(F32) * sv * _silu(gb_ref[...].astype(F32))).astype(ob_ref.dtype)


def even_mixers(z, lb_logits, a_onorm, ln_w, ln_b, ws, bias, *, layer_j, tb):
    bsz, t, _ = z.shape
    h = A_HEADS
    n_rows = lb_logits.shape[0]
    lbl = lb_logits.reshape(n_rows, h, A_DK).transpose(1, 0, 2)

    def sec(k):
        return pl.BlockSpec((None, tb, LANES), lambda b, j, s, k=k: (b, s, k * h + j))

    per_head = lambda shape: pl.BlockSpec((None,) + shape, lambda b, j, s: (j, 0, 0))
    out_spec = pl.BlockSpec((None, tb, LANES), lambda b, j, s: (b, s, j))
    out_sds = jax.ShapeDtypeStruct((bsz, t, h * LANES), BF16)
    return pl.pallas_call(
        functools.partial(_even_mixer_kernel, layer_j=layer_j),
        grid=(bsz, h, t // tb),
        in_specs=[sec(k) for k in range(7)] + [
            per_head((n_rows, LANES)),
            pl.BlockSpec((1, LANES), lambda b, j, s: (0, 0)),
            per_head((1, LANES)),
            per_head((1, LANES)),
            per_head((B_CHUNK, B_CHUNK)),
            per_head((1, B_CHUNK)),
        ],
        out_specs=[out_spec, out_spec],
        out_shape=[out_sds, out_sds],
        scratch_shapes=[pltpu.VMEM((A_DK, A_DK), F32)],
        compiler_params=_params("parallel", "parallel", "arbitrary"),
        name="even_mixers",
    )(z, z, z, z, z, z, z, lbl, a_onorm.reshape(1, LANES),
      ln_w.reshape(B_GROUPS, 1, B_DIM), ln_b.reshape(B_GROUPS, 1, B_DIM),
      ws, bias.reshape(B_GROUPS, 1, B_CHUNK))


def _out_proj_kernel(*refs, n_in, with_next):
    a_refs = refs[:n_in]
    w_refs = refs[n_in:2 * n_in]
    x_ref, post_ref = refs[2 * n_in:2 * n_in + 2]
    rest = refs[2 * n_in + 2:]
    if with_next:
        pre_ref, xo_ref, ho_ref = rest
    else:
        (xo_ref,) = rest
    tm = x_ref.shape[0]
    n_sub = 2 if tm % 32 == 0 else 1
    sub = tm // n_sub

    def project(k):
        rows = slice(k * sub, (k + 1) * sub)
        y = jnp.dot(a_refs[0][rows, :], w_refs[0][...], preferred_element_type=F32)
        for a_ref, w_ref in zip(a_refs[1:], w_refs[1:]):
            y = y + jnp.dot(a_ref[rows, :], w_ref[...], preferred_element_type=F32)
        return y

    def finish(k, y):
        rows = slice(k * sub, (k + 1) * sub)
        x_new = x_ref[rows, :] + _rms(y, post_ref[...])
        xo_ref[rows, :] = x_new
        if with_next:
            ho_ref[rows, :] = _rms(x_new, pre_ref[...]).astype(BF16)

    ys = [project(0)]
    for k in range(n_sub):
        if k + 1 < n_sub:
            ys.append(project(k + 1))
        finish(k, ys[k])


def out_proj(acts, w, x, post_w, next_pre_w, *, tm):
    m, d = x.shape
    n_in = len(acts)
    ka = acts[0].shape[1]
    assert all(a.shape[1] == ka for a in acts) and w.shape == (n_in * ka, d)
    with_next = next_pre_w is not None
    row = lambda width: pl.BlockSpec((tm, width), lambda i: (i, 0))
    const = lambda shape: pl.BlockSpec(shape, lambda i: (0, 0))
    in_specs = ([row(ka) for _ in acts]
                + [pl.BlockSpec((ka, d), lambda i, r=r: (r, 0)) for r in range(n_in)]
                + [row(d), const((1, d))])
    args = list(acts) + [w] * n_in + [x, post_w.reshape(1, d)]
    out_specs = [row(d)]
    out_shape = [jax.ShapeDtypeStruct((m, d), F32)]
    if with_next:
        in_specs.append(const((1, d)))
        args.append(next_pre_w.reshape(1, d))
        out_specs.append(row(d))
        out_shape.append(jax.ShapeDtypeStruct((m, d), BF16))
    return pl.pallas_call(
        functools.partial(_out_proj_kernel, n_in=n_in, with_next=with_next),
        grid=(m // tm,),
        in_specs=in_specs,
        out_specs=out_specs,
        out_shape=out_shape,
        compiler_params=_params("parallel"),
        name="out_proj",
    )(*args)


def _lane_iota():
    return lax.broadcasted_iota(jnp.int32, (1, LANES), 1)


def _join_halves(left, right):
    half = LANES // 2
    return pltpu.roll(jnp.where(_lane_iota() < half, right, left), half, axis=1)


def _odd_in_proj_kernel(h_ref, wt_ref, c_ref, pe_ref, gate_ref):
    h = h_ref[...]
    n_c = c_ref.shape[1]
    n_pe = n_c + C_ROPE
    nt = lambda rows: lax.dot_general(h, wt_ref[rows, :], NT_DIMS, preferred_element_type=F32)
    c_ref[...] = nt(slice(0, n_c)).astype(c_ref.dtype)
    pe_ref[:, :C_ROPE] = nt(slice(n_c, n_pe)).astype(pe_ref.dtype)
    pe_ref[:, C_ROPE:] = jnp.zeros((pe_ref.shape[0], LANES - C_ROPE), pe_ref.dtype)
    gate_ref[...] = nt(slice(n_pe, wt_ref.shape[0])).astype(gate_ref.dtype)


def odd_in_proj(h, wt, *, tm):
    m, k = h.shape
    widths = [2 * C_RANK, LANES, C_HEADS * C_V]
    w = wt
    return pl.pallas_call(
        _odd_in_proj_kernel,
        grid=(m // tm,),
        in_specs=[pl.BlockSpec((tm, k), lambda i: (i, 0)),
                  pl.BlockSpec(w.shape, lambda i: (0, 0))],
        out_specs=[pl.BlockSpec((tm, wd), lambda i: (i, 0)) for wd in widths],
        out_shape=[jax.ShapeDtypeStruct((m, wd), BF16) for wd in widths],
        compiler_params=_params("parallel"),
        name="odd_in_proj",
    )(h, w)


def _mla_expand_kernel(c_ref, pe_ref, pos_ref, freq_ref, qnorm_ref, kvnorm_ref, wq_ref, wk_ref,
                       wvt_ref, qn_ref, qpe_ref, kn_ref, vt_ref, kpe_ref, *, scale):
    c = c_ref[...].astype(F32)
    cq = _rms(c[:, :C_RANK], qnorm_ref[...]).astype(BF16)
    ckv = _rms(c[:, C_RANK:], kvnorm_ref[...]).astype(BF16)

    q = jnp.dot(cq, wq_ref[...], preferred_element_type=F32)
    kn_ref[...] = jnp.dot(ckv, wk_ref[...], preferred_element_type=F32).astype(kn_ref.dtype)
    vt_ref[...] = lax.dot_general(wvt_ref[...], ckv, NT_DIMS,
                                  preferred_element_type=F32).astype(vt_ref.dtype)

    pos = pos_ref[...].astype(F32)
    pos = jnp.concatenate([_column_tile(pos[g:g + 1]) for g in range(pos.shape[0])], axis=0)
    ang = pos * freq_ref[...]
    cos = jnp.cos(ang)
    sin = jnp.sin(ang)
    lane = _lane_iota()
    first_half = lane < C_ROPE
    low_rot = (lane % C_ROPE) < C_ROPE // 2

    def rope(x):
        rot = jnp.where(low_rot, -pltpu.roll(x, LANES - C_ROPE // 2, axis=1),
                        pltpu.roll(x, C_ROPE // 2, axis=1))
        return x * cos + rot * sin

    for p in range(C_HEADS // 2):
        a, b, c3 = (q[:, (3 * p + k) * LANES:(3 * p + k + 1) * LANES] for k in range(3))
        qn_ref[:, 2 * p * LANES:(2 * p + 1) * LANES] = (a * scale).astype(qn_ref.dtype)
        qn_ref[:, (2 * p + 1) * LANES:(2 * p + 2) * LANES] = (
            _join_halves(b, c3) * scale).astype(qn_ref.dtype)
        pe_pair = jnp.where(first_half, b, c3)
        qpe_ref[:, p * LANES:(p + 1) * LANES] = (rope(pe_pair) * scale).astype(qpe_ref.dtype)

    k_pe = rope(pe_ref[...].astype(F32))
    kpe_ref[...] = jnp.where(first_half, k_pe,
                             pltpu.roll(k_pe, C_ROPE, axis=1)).astype(kpe_ref.dtype)


def mla_expand(c, pe, pos, freq, q_norm, kv_norm, wq, wk, wvt, *, bsz, tm, scale):
    m = c.shape[0]
    t = m // bsz
    nb = t // tm
    row = lambda width: pl.BlockSpec((tm, width), lambda i: (i, 0))
    const = lambda shape: pl.BlockSpec(shape, lambda i: (0, 0))
    n_nope = C_HEADS * C_NOPE
    n_v = C_HEADS * C_V
    widths = [n_nope, C_HEADS * C_ROPE, n_nope, None, LANES]
    out_specs = [row(wd) if wd else pl.BlockSpec((None, n_v, tm), lambda i: (i // nb, 0, i % nb))
                 for wd in widths]
    out_shape = [jax.ShapeDtypeStruct((m, wd) if wd else (bsz, n_v, t), BF16) for wd in widths]
    return pl.pallas_call(
        functools.partial(_mla_expand_kernel, scale=scale),
        grid=(m // tm,),
        in_specs=[row(c.shape[1]), row(pe.shape[1]),
                  pl.BlockSpec((None, tm // LANES, LANES), lambda i: (i, 0, 0)), const((1, LANES)),
                  const((1, C_RANK)), const((1, C_RANK)), const(wq.shape), const(wk.shape),
                  const(wvt.shape)],
        out_specs=out_specs,
        out_shape=out_shape,
        compiler_params=_params("parallel"),
        name="mla_expand",
    )(c, pe, pos, freq, q_norm.reshape(1, C_RANK), kv_norm.reshape(1, C_RANK), wq, wk, wvt)


def _attn_kernel(qn_ref, qpe_ref, kn_ref, kpe_ref, vt_ref, g_ref, o_ref, k_scr, v_scr, *, tq):
    t = qn_ref.shape[0]
    heads = tuple(range(qn_ref.shape[1] // LANES))
    for hh in heads:
        k_scr[hh, :, :LANES] = kn_ref[:, hh * LANES:(hh + 1) * LANES]
        k_scr[hh, :, LANES:] = kpe_ref[...]
        v_scr[hh, :C_V] = vt_ref[hh * C_V:(hh + 1) * C_V]
        v_scr[hh, C_V:] = jnp.ones((v_scr.shape[1] - C_V, t), v_scr.dtype)

    lane = lax.broadcasted_iota(jnp.int32, (1, LANES), 1)
    causal = (lax.broadcasted_iota(jnp.int32, (tq, tq), 0)
              <= lax.broadcasted_iota(jnp.int32, (tq, tq), 1))
    neg = jnp.finfo(BF16).min

    def scores(qi, hh):
        r0 = qi * tq
        end = r0 + tq
        q_pe = qpe_ref[r0:end, (hh // 2) * LANES:(hh // 2 + 1) * LANES]
        q_pe = jnp.where((lane // C_ROPE) == hh % 2, q_pe, jnp.zeros_like(q_pe))
        q = jnp.concatenate([qn_ref[r0:end, hh * LANES:(hh + 1) * LANES], q_pe], axis=1)
        return lax.dot_general(k_scr[hh, :end, :], q, NT_DIMS,
                               preferred_element_type=F32).astype(BF16)

    def col_max(qi, s):
        r0 = qi * tq
        m = jnp.max(jnp.where(causal, s[r0:], neg), axis=0, keepdims=True)
        if qi > 0:
            m = jnp.maximum(m, jnp.max(s[:r0], axis=0, keepdims=True))
        return m

    def probs(qi, s, m):
        r0 = qi * tq
        p_d = jnp.where(causal, jnp.exp2(s[r0:] - m), jnp.zeros((), BF16))
        p_o = jnp.exp2(s[:r0] - m) if qi > 0 else None
        return p_d, p_o

    def values(qi, hh, p_d, p_o):
        r0 = qi * tq
        end = r0 + tq
        vt = v_scr.at[hh]
        acc = jnp.dot(vt[:, r0:end], p_d, preferred_element_type=F32)
        if qi > 0:
            acc = acc + jnp.dot(vt[:, :r0], p_o, preferred_element_type=F32)
        o = (acc[:C_V] * (1.0 / acc[C_V:C_V + 1])).T
        cols = slice(hh * C_V, (hh + 1) * C_V)
        o_ref[r0:end, cols] = (o * _silu(g_ref[r0:end, cols].astype(F32))).astype(o_ref.dtype)

    nq = t // tq
    s = [scores(0, hh) for hh in heads]
    sm = [col_max(0, s[hh]) for hh in heads]
    for qi in range(nq):
        last = qi + 1 == nq
        for hh in heads:
            s_next = None if last else scores(qi + 1, hh)
            values(qi, hh, *probs(qi, s[hh], sm[hh]))
            if not last:
                s[hh] = s_next
                sm[hh] = col_max(qi + 1, s_next)


def attention(qn, qpe, kn, kpe, vt, gate, *, tq):
    bsz, t, _ = qn.shape
    g = ATTN_HEADS_PER_STEP
    group = lambda width: pl.BlockSpec((None, t, width), lambda b, p: (b, 0, p))
    shared = pl.BlockSpec((None, t, LANES), lambda b, p: (b, 0, 0))
    group_t = pl.BlockSpec((None, g * C_V, t), lambda b, p: (b, p, 0))
    return pl.pallas_call(
        functools.partial(_attn_kernel, tq=tq),
        grid=(bsz, C_HEADS // g),
        in_specs=[group(g * LANES), group(g * C_ROPE), group(g * LANES), shared, group_t,
                  group(g * C_V)],
        out_specs=group(g * C_V),
        out_shape=jax.ShapeDtypeStruct((bsz, t, C_HEADS * C_V), BF16),
        scratch_shapes=[pltpu.VMEM((g, t, 2 * LANES), BF16),
                        pltpu.VMEM((g, C_V + BF16_SUBLANES, t), BF16)],
        compiler_params=_params("parallel", "parallel"),
        name="mla_attention",
    )(qn, qpe, kn, kpe, vt, gate)


def _odd_weights(w_in, w_qb, w_kvb):
    w_in_b = w_in.T
    kvb = w_kvb.astype(BF16).reshape(C_RANK, C_HEADS, C_NOPE + C_V)
    wk = kvb[:, :, :C_NOPE].reshape(C_RANK, -1)
    wvt = kvb[:, :, C_NOPE:].reshape(C_RANK, -1).T
    return w_in_b, w_qb, wk, wvt


def kernel(x, positions, norm_pre, norm_post, ev_w_in, ev_lb_logits, ev_a_onorm, ev_b_ln_w,
           ev_b_ln_b, ev_b_ws, ev_b_bias, ev_w_out, od_w_in, od_q_norm, od_w_qb, od_kv_norm,
           od_w_kvb, od_w_out):
    bsz, t, d = x.shape
    m = bsz * t
    xf = x.reshape(m, d)
    tm_big = min(1024, m)
    tm = min(512, m)

    z = rms_matmul(xf, norm_pre[0].reshape(1, d), ev_w_in[0], tm=tm_big, tn=1792)
    mix_a, mix_b = even_mixers(z.reshape(bsz, t, -1), ev_lb_logits, ev_a_onorm[0], ev_b_ln_w[0],
                               ev_b_ln_b[0], ev_b_ws[0], ev_b_bias[0], layer_j=0, tb=min(1024, t))
    x1, h1 = out_proj([mix_a.reshape(m, -1), mix_b.reshape(m, -1)], ev_w_out[0],
                      xf, norm_post[0], norm_pre[1], tm=tm)

    w_in_b, wq_all, wk, wvt = _odd_weights(od_w_in[0], od_w_qb[0], od_w_kvb[0])
    c, pe, gate = odd_in_proj(h1, w_in_b, tm=tm)
    inv_freq = ROPE_THETA ** (-jnp.arange(0, C_ROPE, 2, dtype=F32) / C_ROPE)
    freq = jnp.tile(inv_freq, LANES // (C_ROPE // 2)).reshape(1, LANES)
    scale = float((C_NOPE + C_ROPE) ** -0.5 * LOG2E)
    tm_e = min(tm, t)
    pos = positions.reshape(m // tm_e, tm_e // LANES, LANES)
    qn, qpe, kn, vt, kpe = mla_expand(c, pe, pos, freq, od_q_norm[0], od_kv_norm[0], wq_all, wk,
                                      wvt, bsz=bsz, tm=tm_e, scale=scale)
    r3 = lambda a: a.reshape(bsz, t, -1)
    o = attention(r3(qn), r3(qpe), r3(kn), r3(kpe), vt, r3(gate), tq=min(256, t))
    (x2,) = out_proj([o.reshape(m, -1)], od_w_out[0], x1, norm_post[1], None, tm=tm)
    return x2.reshape(bsz, t, d)
```

```python
import functools

import jax
import jax.numpy as jnp
from jax import lax
from jax.experimental import pallas as pl
from jax.experimental.pallas import tpu as pltpu

F32 = jnp.float32
BF16 = jnp.bfloat16

EPS = 1e-6
LANES = 128
BF16_SUBLANES = 16
VMEM_LIMIT_BYTES = 56 * 1024 * 1024
VMEM_LIMIT_BYTES_IN_PROJ = 62 * 1024 * 1024

A_HEADS = 8
A_DK = 128
A_CHUNK = 64
A_SUB = 16
B_GROUPS = 8
B_DIM = 128
B_CHUNK = 128
C_HEADS = 16
C_RANK = 512
C_NOPE = 128
C_ROPE = 64
C_V = 128
ROPE_THETA = 10000.0
LOG2E = 1.4426950408889634
ATTN_HEADS_PER_STEP = 2

NT_DIMS = (((1,), (1,)), ((), ()))
TN_DIMS = (((0,), (0,)), ((), ()))


def _params(*sem, vmem_limit=VMEM_LIMIT_BYTES):
    return pltpu.CompilerParams(dimension_semantics=sem, vmem_limit_bytes=vmem_limit)


def _rms(x, w):
    ms = jnp.mean(x * x, axis=-1, keepdims=True)
    return x * lax.rsqrt(ms + EPS) * w


def _silu(x):
    return x * (1.0 / (1.0 + jnp.exp(-x)))


def _column_tile(row):
    return jnp.broadcast_to(row, (LANES, LANES)).T


def _rms_matmul_kernel(x_ref, g_ref, w_ref, o_ref, h_ref):
    @pl.when(pl.program_id(1) == 0)
    def _():
        h_ref[...] = _rms(x_ref[...], g_ref[...]).astype(BF16)

    o_ref[...] = jnp.dot(h_ref[...], w_ref[...], preferred_element_type=F32).astype(o_ref.dtype)


def rms_matmul(x, g, w, *, tm, tn):
    m, k = x.shape
    n = w.shape[1]
    return pl.pallas_call(
        _rms_matmul_kernel,
        grid=(m // tm, n // tn),
        in_specs=[
            pl.BlockSpec((tm, k), lambda i, j: (i, 0)),
            pl.BlockSpec((1, k), lambda i, j: (0, 0)),
            pl.BlockSpec((k, tn), lambda i, j: (0, j)),
        ],
        out_specs=pl.BlockSpec((tm, tn), lambda i, j: (i, j)),
        out_shape=jax.ShapeDtypeStruct((m, n), BF16),
        scratch_shapes=[pltpu.VMEM((tm, k), BF16)],
        compiler_params=_params("parallel", "arbitrary", vmem_limit=VMEM_LIMIT_BYTES_IN_PROJ),
        name="rms_matmul",
    )(x, g, w)


def _hgrn2_block(q, zf, v, lb, st):
    c, s = A_CHUNK, A_SUB
    m = c // s
    tb = q.shape[0]
    nch = tb // c
    tril_mask = (lax.broadcasted_iota(jnp.int32, (c, c), 1)
                 <= lax.broadcasted_iota(jnp.int32, (c, c), 0))
    tril = jnp.where(tril_mask, 1.0, 0.0).astype(BF16)

    t = jnp.exp(-jnp.abs(zf))
    u = (1.0 - lb) * (1.0 / (1.0 + t))
    w = t * u
    pos = zf >= 0
    logf = jnp.log2(lb + jnp.where(pos, u, w))
    kk = jnp.where(pos, w, u)

    hi = logf.astype(BF16)
    r1 = logf - hi.astype(F32)
    mid = r1.astype(BF16)
    lo = (r1 - mid.astype(F32)).astype(BF16)
    terms = jnp.concatenate([hi, mid, lo], axis=1)
    b_chunks = []
    for n in range(nch):
        y = jnp.dot(tril, terms[n * c:(n + 1) * c], preferred_element_type=F32)
        b_chunks.append((y[:, :A_DK] + y[:, A_DK:2 * A_DK]) + y[:, 2 * A_DK:])
    b = jnp.concatenate(b_chunks, axis=0)

    shape4 = (nch, m, s, A_DK)
    b4 = b.reshape(shape4)
    g4 = (b4 - logf.reshape(shape4))[:, :, 0:1, :]
    e4 = b4[:, :, s - 1:s, :]
    bl4 = e4[:, m - 1:m]
    lc4 = b4 - g4
    kk4 = kk.reshape(shape4)
    q_sub = (q.reshape(shape4) * jnp.exp2(lc4)).astype(BF16)
    k_diag = (kk4 * jnp.exp2(-lc4)).astype(BF16)
    k_end = (kk4 * jnp.exp2(e4 - b4)).astype(BF16)
    q_in = q_sub * jnp.exp2(g4).astype(BF16)
    k_in = k_end * jnp.exp2(bl4 - e4).astype(BF16)
    decay = jnp.exp2(bl4)
    k_cross = {(i, j): k_end[:, j] * jnp.exp2(g4[:, i] - e4[:, j]).astype(BF16)
               for i in range(m) for j in range(i - 1)}
    vb = v.astype(BF16)
    zeros = jnp.zeros((s, A_DK), BF16)

    o_intra, kv = [], []
    for n in range(nch):
        score_rows = []
        for i in range(m):
            parts = [k_end[n, j] if j == i - 1 else k_cross[(i, j)][n] for j in range(i)]
            parts.append(k_diag[n, i])
            parts.extend([zeros] * (m - 1 - i))
            k_i = jnp.concatenate(parts, axis=0)
            score_rows.append(lax.dot_general(q_sub[n, i], k_i, NT_DIMS,
                                              preferred_element_type=F32))
        scores = jnp.concatenate(score_rows, axis=0)
        scores = jnp.where(tril_mask, scores, 0.0).astype(BF16)
        vb_n = vb[n * c:(n + 1) * c]
        o_intra.append(jnp.dot(scores, vb_n, preferred_element_type=F32))
        kv.append(lax.dot_general(vb_n, k_in[n].reshape(c, A_DK), TN_DIMS,
                                  preferred_element_type=F32))

    states = []
    for n in range(nch):
        states.append(st.astype(BF16))
        st = st * decay[n, 0] + kv[n]
    outs = [o_intra[n] + lax.dot_general(q_in[n].reshape(c, A_DK), states[n], NT_DIMS,
                                         preferred_element_type=F32) for n in range(nch)]
    return jnp.concatenate(outs, axis=0), st


def _even_mixer_kernel(q_ref, f_ref, i_ref, ga_ref, u_ref, v_ref, gb_ref,
                       lbl_ref, onorm_ref, lnw_ref, lnb_ref, ws_ref, bias_ref,
                       oa_ref, ob_ref, st_ref, *, layer_j):
    tb = q_ref.shape[0]

    @pl.when(pl.program_id(2) == 0)
    def _():
        st_ref[...] = jnp.zeros_like(st_ref)

    logits = lbl_ref[...]
    e = jnp.exp(logits - jnp.max(logits, axis=0, keepdims=True))
    lb = jnp.sum(e[:layer_j + 1], axis=0, keepdims=True) / jnp.sum(e, axis=0, keepdims=True)

    o, st = _hgrn2_block(q_ref[...].astype(F32), f_ref[...].astype(F32), i_ref[...].astype(F32),
                         lb, st_ref[...])
    st_ref[...] = st
    oa_ref[...] = (_rms(o, onorm_ref[...]) * _silu(ga_ref[...].astype(F32))).astype(oa_ref.dtype)

    cb = B_CHUNK
    w_causal = jnp.where(lax.broadcasted_iota(jnp.int32, (cb, cb), 1)
                         <= lax.broadcasted_iota(jnp.int32, (cb, cb), 0),
                         ws_ref[...], 0.0).astype(BF16)
    vv = v_ref[...].astype(F32)
    mu = jnp.mean(vv, axis=-1, keepdims=True)
    xc = vv - mu
    vg = (xc * lax.rsqrt(jnp.mean(xc * xc, axis=-1, keepdims=True) + EPS) * lnw_ref[...]
          + lnb_ref[...]).astype(BF16)
    sv = jnp.concatenate([jnp.dot(w_causal, vg[n * cb:(n + 1) * cb], preferred_element_type=F32)
                          for n in range(tb // cb)], axis=0)
    bias = _column_tile(bias_ref[...])
    sv = (sv.reshape(tb // cb, cb, B_DIM) + bias).reshape(tb, B_DIM)
    ob_ref[...] = (u_ref[...].astype(F32) * sv * _silu(gb_ref[...].astype(F32))).astype(ob_ref.dtype)


def even_mixers(z, lb_logits, a_onorm, ln_w, ln_b, ws, bias, *, layer_j, tb):
    bsz, t, _ = z.shape
    h = A_HEADS
    n_rows = lb_logits.shape[0]
    lbl = lb_logits.reshape(n_rows, h, A_DK).transpose(1, 0, 2)

    def sec(k):
        return pl.BlockSpec((None, tb, LANES), lambda b, j, s, k=k: (b, s, k * h + j))

    per_head = lambda shape: pl.BlockSpec((None,) + shape, lambda b, j, s: (j, 0, 0))
    out_spec = pl.BlockSpec((None, tb, LANES), lambda b, j, s: (b, s, j))
    out_sds = jax.ShapeDtypeStruct((bsz, t, h * LANES), BF16)
    return pl.pallas_call(
        functools.partial(_even_mixer_kernel, layer_j=layer_j),
        grid=(bsz, h, t // tb),
        in_specs=[sec(k) for k in range(7)] + [
            per_head((n_rows, LANES)),
            pl.BlockSpec((1, LANES), lambda b, j, s: (0, 0)),
            per_head((1, LANES)),
            per_head((1, LANES)),
            per_head((B_CHUNK, B_CHUNK)),
            per_head((1, B_CHUNK)),
        ],
        out_specs=[out_spec, out_spec],
        out_shape=[out_sds, out_sds],
        scratch_shapes=[pltpu.VMEM((A_DK, A_DK), F32)],
        compiler_params=_params("parallel", "parallel", "arbitrary"),
        name="even_mixers",
    )(z, z, z, z, z, z, z, lbl, a_onorm.reshape(1, LANES),
      ln_w.reshape(B_GROUPS, 1, B_DIM), ln_b.reshape(B_GROUPS, 1, B_DIM),
      ws, bias.reshape(B_GROUPS, 1, B_CHUNK))


def _out_proj_kernel(*refs, n_in, with_next):
    a_refs = refs[:n_in]
    w_refs = refs[n_in:2 * n_in]
    x_ref, post_ref = refs[2 * n_in:2 * n_in + 2]
    rest = refs[2 * n_in + 2:]
    if with_next:
        pre_ref, xo_ref, ho_ref = rest
    else:
        (xo_ref,) = rest
    tm = x_ref.shape[0]
    n_sub = 2 if tm % 32 == 0 else 1
    sub = tm // n_sub

    def project(k):
        rows = slice(k * sub, (k + 1) * sub)
        y = jnp.dot(a_refs[0][rows, :], w_refs[0][...], preferred_element_type=F32)
        for a_ref, w_ref in zip(a_refs[1:], w_refs[1:]):
            y = y + jnp.dot(a_ref[rows, :], w_ref[...], preferred_element_type=F32)
        return y

    def finish(k, y):
        rows = slice(k * sub, (k + 1) * sub)
        x_new = x_ref[rows, :] + _rms(y, post_ref[...])
        xo_ref[rows, :] = x_new
        if with_next:
            ho_ref[rows, :] = _rms(x_new, pre_ref[...]).astype(BF16)

    ys = [project(0)]
    for k in range(n_sub):
        if k + 1 < n_sub:
            ys.append(project(k + 1))
        finish(k, ys[k])


def out_proj(acts, w, x, post_w, next_pre_w, *, tm):
    m, d = x.shape
    n_in = len(acts)
    ka = acts[0].shape[1]
    assert all(a.shape[1] == ka for a in acts) and w.shape == (n_in * ka, d)
    with_next = next_pre_w is not None
    row = lambda width: pl.BlockSpec((tm, width), lambda i: (i, 0))
    const = lambda shape: pl.BlockSpec(shape, lambda i: (0, 0))
    in_specs = ([row(ka) for _ in acts]
                + [pl.BlockSpec((ka, d), lambda i, r=r: (r, 0)) for r in range(n_in)]
                + [row(d), const((1, d))])
    args = list(acts) + [w] * n_in + [x, post_w.reshape(1, d)]
    out_specs = [row(d)]
    out_shape = [jax.ShapeDtypeStruct((m, d), F32)]
    if with_next:
        in_specs.append(const((1, d)))
        args.append(next_pre_w.reshape(1, d))
        out_specs.append(row(d))
        out_shape.append(jax.ShapeDtypeStruct((m, d), BF16))
    return pl.pallas_call(
        functools.partial(_out_proj_kernel, n_in=n_in, with_next=with_next),
        grid=(m // tm,),
        in_specs=in_specs,
        out_specs=out_specs,
        out_shape=out_shape,
        compiler_params=_params("parallel"),
        name="out_proj",
    )(*args)


def _lane_iota():
    return lax.broadcasted_iota(jnp.int32, (1, LANES), 1)


def _join_halves(left, right):
    half = LANES // 2
    return pltpu.roll(jnp.where(_lane_iota() < half, right, left), half, axis=1)


def _odd_in_proj_kernel(h_ref, wt_ref, c_ref, pe_ref, gate_ref):
    h = h_ref[...]
    n_c = c_ref.shape[1]
    n_pe = n_c + C_ROPE
    nt = lambda rows: lax.dot_general(h, wt_ref[rows, :], NT_DIMS, preferred_element_type=F32)
    c_ref[...] = nt(slice(0, n_c)).astype(c_ref.dtype)
    pe_ref[:, :C_ROPE] = nt(slice(n_c, n_pe)).astype(pe_ref.dtype)
    pe_ref[:, C_ROPE:] = jnp.zeros((pe_ref.shape[0], LANES - C_ROPE), pe_ref.dtype)
    gate_ref[...] = nt(slice(n_pe, wt_ref.shape[0])).astype(gate_ref.dtype)


def odd_in_proj(h, wt, *, tm):
    m, k = h.shape
    widths = [2 * C_RANK, LANES, C_HEADS * C_V]
    w = wt
    return pl.pallas_call(
        _odd_in_proj_kernel,
        grid=(m // tm,),
        in_specs=[pl.BlockSpec((tm, k), lambda i: (i, 0)),
                  pl.BlockSpec(w.shape, lambda i: (0, 0))],
        out_specs=[pl.BlockSpec((tm, wd), lambda i: (i, 0)) for wd in widths],
        out_shape=[jax.ShapeDtypeStruct((m, wd), BF16) for wd in widths],
        compiler_params=_params("parallel"),
        name="odd_in_proj",
    )(h, w)


def _mla_expand_kernel(c_ref, pe_ref, pos_ref, freq_ref, qnorm_ref, kvnorm_ref, wq_ref, wk_ref,
                       wvt_ref, qn_ref, qpe_ref, kn_ref, vt_ref, kpe_ref, *, scale):
    c = c_ref[...].astype(F32)
    cq = _rms(c[:, :C_RANK], qnorm_ref[...]).astype(BF16)
    ckv = _rms(c[:, C_RANK:], kvnorm_ref[...]).astype(BF16)

    q = jnp.dot(cq, wq_ref[...], preferred_element_type=F32)
    kn_ref[...] = jnp.dot(ckv, wk_ref[...], preferred_element_type=F32).astype(kn_ref.dtype)
    vt_ref[...] = lax.dot_general(wvt_ref[...], ckv, NT_DIMS,
                                  preferred_element_type=F32).astype(vt_ref.dtype)

    pos = pos_ref[...].astype(F32)
    pos = jnp.concatenate([_column_tile(pos[g:g + 1]) for g in range(pos.shape[0])], axis=0)
    ang = pos * freq_ref[...]
    cos = jnp.cos(ang)
    sin = jnp.sin(ang)
    lane = _lane_iota()
    first_half = lane < C_ROPE
    low_rot = (lane % C_ROPE) < C_ROPE // 2

    def rope(x):
        rot = jnp.where(low_rot, -pltpu.roll(x, LANES - C_ROPE // 2, axis=1),
                        pltpu.roll(x, C_ROPE // 2, axis=1))
        return x * cos + rot * sin

    for p in range(C_HEADS // 2):
        a, b, c3 = (q[:, (3 * p + k) * LANES:(3 * p + k + 1) * LANES] for k in range(3))
        qn_ref[:, 2 * p * LANES:(2 * p + 1) * LANES] = (a * scale).astype(qn_ref.dtype)
        qn_ref[:, (2 * p + 1) * LANES:(2 * p + 2) * LANES] = (
            _join_halves(b, c3) * scale).astype(qn_ref.dtype)
        pe_pair = jnp.where(first_half, b, c3)
        qpe_ref[:, p * LANES:(p + 1) * LANES] = (rope(pe_pair) * scale).astype(qpe_ref.dtype)

    k_pe = rope(pe_ref[...].astype(F32))
    kpe_ref[...] = jnp.where(first_half, k_pe,
                             pltpu.roll(k_pe, C_ROPE, axis=1)).astype(kpe_ref.dtype)


def mla_expand(c, pe, pos, freq, q_norm, kv_norm, wq, wk, wvt, *, bsz, tm, scale):
    m = c.shape[0]
    t = m // bsz
    nb = t // tm
    row = lambda width: pl.BlockSpec((tm, width), lambda i: (i, 0))
    const = lambda shape: pl.BlockSpec(shape, lambda i: (0, 0))
    n_nope = C_HEADS * C_NOPE
    n_v = C_HEADS * C_V
    widths = [n_nope, C_HEADS * C_ROPE, n_nope, None, LANES]
    out_specs = [row(wd) if wd else pl.BlockSpec((None, n_v, tm), lambda i: (i // nb, 0, i % nb))
                 for wd in widths]
    out_shape = [jax.ShapeDtypeStruct((m, wd) if wd else (bsz, n_v, t), BF16) for wd in widths]
    return pl.pallas_call(
        functools.partial(_mla_expand_kernel, scale=scale),
        grid=(m // tm,),
        in_specs=[row(c.shape[1]), row(pe.shape[1]),
                  pl.BlockSpec((None, tm // LANES, LANES), lambda i: (i, 0, 0)), const((1, LANES)),
                  const((1, C_RANK)), const((1, C_RANK)), const(wq.shape), const(wk.shape),
                  const(wvt.shape)],
        out_specs=out_specs,
        out_shape=out_shape,
        compiler_params=_params("parallel"),
        name="mla_expand",
    )(c, pe, pos, freq, q_norm.reshape(1, C_RANK), kv_norm.reshape(1, C_RANK), wq, wk, wvt)


def _attn_kernel(qn_ref, qpe_ref, kn_ref, kpe_ref, vt_ref, g_ref, o_ref, k_scr, v_scr, *, tq):
    t = qn_ref.shape[0]
    heads = tuple(range(qn_ref.shape[1] // LANES))
    for hh in heads:
        k_scr[hh, :, :LANES] = kn_ref[:, hh * LANES:(hh + 1) * LANES]
        k_scr[hh, :, LANES:] = kpe_ref[...]
        v_scr[hh, :C_V] = vt_ref[hh * C_V:(hh + 1) * C_V]
        v_scr[hh, C_V:] = jnp.ones((v_scr.shape[1] - C_V, t), v_scr.dtype)

    lane = lax.broadcasted_iota(jnp.int32, (1, LANES), 1)
    causal = (lax.broadcasted_iota(jnp.int32, (tq, tq), 0)
              <= lax.broadcasted_iota(jnp.int32, (tq, tq), 1))
    neg = jnp.finfo(BF16).min

    def scores(qi, hh):
        r0 = qi * tq
        end = r0 + tq
        q_pe = qpe_ref[r0:end, (hh // 2) * LANES:(hh // 2 + 1) * LANES]
        q_pe = jnp.where((lane // C_ROPE) == hh % 2, q_pe, jnp.zeros_like(q_pe))
        q = jnp.concatenate([qn_ref[r0:end, hh * LANES:(hh + 1) * LANES], q_pe], axis=1)
        return lax.dot_general(k_scr[hh, :end, :], q, NT_DIMS,
                               preferred_element_type=F32).astype(BF16)

    def col_max(qi, s):
        r0 = qi * tq
        m = jnp.max(jnp.where(causal, s[r0:], neg), axis=0, keepdims=True)
        if qi > 0:
            m = jnp.maximum(m, jnp.max(s[:r0], axis=0, keepdims=True))
        return m

    def probs(qi, s, m):
        r0 = qi * tq
        p_d = jnp.where(causal, jnp.exp2(s[r0:] - m), jnp.zeros((), BF16))
        p_o = jnp.exp2(s[:r0] - m) if qi > 0 else None
        return p_d, p_o

    def values(qi, hh, p_d, p_o):
        r0 = qi * tq
        end = r0 + tq
        vt = v_scr.at[hh]
        acc = jnp.dot(vt[:, r0:end], p_d, preferred_element_type=F32)
        if qi > 0:
            acc = acc + jnp.dot(vt[:, :r0], p_o, preferred_element_type=F32)
        o = (acc[:C_V] * (1.0 / acc[C_V:C_V + 1])).T
        cols = slice(hh * C_V, (hh + 1) * C_V)
        o_ref[r0:end, cols] = (o * _silu(g_ref[r0:end, cols].astype(F32))).astype(o_ref.dtype)

    nq = t // tq
    s = [scores(0, hh) for hh in heads]
    sm = [col_max(0, s[hh]) for hh in heads]
    for qi in range(nq):
        last = qi + 1 == nq
        for hh in heads:
            s_next = None if last else scores(qi + 1, hh)
            values(qi, hh, *probs(qi, s[hh], sm[hh]))
            if not last:
                s[hh] = s_next
                sm[hh] = col_max(qi + 1, s_next)


def attention(qn, qpe, kn, kpe, vt, gate, *, tq):
    bsz, t, _ = qn.shape
    g = ATTN_HEADS_PER_STEP
    group = lambda width: pl.BlockSpec((None, t, width), lambda b, p: (b, 0, p))
    shared = pl.BlockSpec((None, t, LANES), lambda b, p: (b, 0, 0))
    group_t = pl.BlockSpec((None, g * C_V, t), lambda b, p: (b, p, 0))
    return pl.pallas_call(
        functools.partial(_attn_kernel, tq=tq),
        grid=(bsz, C_HEADS // g),
        in_specs=[group(g * LANES), group(g * C_ROPE), group(g * LANES), shared, group_t,
                  group(g * C_V)],
        out_specs=group(g * C_V),
        out_shape=jax.ShapeDtypeStruct((bsz, t, C_HEADS * C_V), BF16),
        scratch_shapes=[pltpu.VMEM((g, t, 2 * LANES), BF16),
                        pltpu.VMEM((g, C_V + BF16_SUBLANES, t), BF16)],
        compiler_params=_params("parallel", "parallel"),
        name="mla_attention",
    )(qn, qpe, kn, kpe, vt, gate)


def _odd_weights(w_in, w_qb, w_kvb):
    w_in_b = w_in.T
    kvb = w_kvb.astype(BF16).reshape(C_RANK, C_HEADS, C_NOPE + C_V)
    wk = kvb[:, :, :C_NOPE].reshape(C_RANK, -1)
    wvt = kvb[:, :, C_NOPE:].reshape(C_RANK, -1).T
    return w_in_b, w_qb, wk, wvt


def kernel(x, positions, norm_pre, norm_post, ev_w_in, ev_lb_logits, ev_a_onorm, ev_b_ln_w,
           ev_b_ln_b, ev_b_ws, ev_b_bias, ev_w_out, od_w_in, od_q_norm, od_w_qb, od_kv_norm,
           od_w_kvb, od_w_out):
    bsz, t, d = x.shape
    m = bsz * t
    xf = x.reshape(m, d)
    tm_big = min(1024, m)
    tm = min(512, m)

    z = rms_matmul(xf, norm_pre[0].reshape(1, d), ev_w_in[0], tm=tm_big, tn=1792)
    mix_a, mix_b = even_mixers(z.reshape(bsz, t, -1), ev_lb_logits, ev_a_onorm[0], ev_b_ln_w[0],
                               ev_b_ln_b[0], ev_b_ws[0], ev_b_bias[0], layer_j=0, tb=min(2048, t))
    x1, h1 = out_proj([mix_a.reshape(m, -1), mix_b.reshape(m, -1)], ev_w_out[0],
                      xf, norm_post[0], norm_pre[1], tm=tm)

    w_in_b, wq_all, wk, wvt = _odd_weights(od_w_in[0], od_w_qb[0], od_w_kvb[0])
    c, pe, gate = odd_in_proj(h1, w_in_b, tm=tm)
    inv_freq = ROPE_THETA ** (-jnp.arange(0, C_ROPE, 2, dtype=F32) / C_ROPE)
    freq = jnp.tile(inv_freq, LANES // (C_ROPE // 2)).reshape(1, LANES)
    scale = float((C_NOPE + C_ROPE) ** -0.5 * LOG2E)
    tm_e = min(tm, t)
    pos = positions.reshape(m // tm_e, tm_e // LANES, LANES)
    qn, qpe, kn, vt, kpe = mla_expand(c, pe, pos, freq, od_q_norm[0], od_kv_norm[0], wq_all, wk,
                                      wvt, bsz=bsz, tm=tm_e, scale=scale)
    r3 = lambda a: a.reshape(bsz, t, -1)
    o = attention(r3(qn), r3(qpe), r3(kn), r3(kpe), vt, r3(gate), tq=min(256, t))
    (x2,) = out_proj([o.reshape(m, -1)], od_w_out[0], x1, norm_post[1], None, tm=tm)
    return x2.reshape(bsz, t, d)
```

```python
import functools

import jax
import jax.numpy as jnp
from jax import lax
from jax.experimental import pallas as pl
from jax.experimental.pallas import tpu as pltpu

F32 = jnp.float32
BF16 = jnp.bfloat16

EPS = 1e-6
LANES = 128
BF16_SUBLANES = 16
VMEM_LIMIT_BYTES = 56 * 1024 * 1024
VMEM_LIMIT_BYTES_IN_PROJ = 62 * 1024 * 1024

A_HEADS = 8
A_DK = 128
A_CHUNK = 64
A_SUB = 16
B_GROUPS = 8
B_DIM = 128
B_CHUNK = 128
C_HEADS = 16
C_RANK = 512
C_NOPE = 128
C_ROPE = 64
C_V = 128
ROPE_THETA = 10000.0
LOG2E = 1.4426950408889634
ATTN_HEADS_PER_STEP = 4

NT_DIMS = (((1,), (1,)), ((), ()))
TN_DIMS = (((0,), (0,)), ((), ()))


def _params(*sem, vmem_limit=VMEM_LIMIT_BYTES):
    return pltpu.CompilerParams(dimension_semantics=sem, vmem_limit_bytes=vmem_limit)


def _rms(x, w):
    ms = jnp.mean(x * x, axis=-1, keepdims=True)
    return x * lax.rsqrt(ms + EPS) * w


def _silu(x):
    return x * (1.0 / (1.0 + jnp.exp(-x)))


def _column_tile(row):
    return jnp.broadcast_to(row, (LANES, LANES)).T


def _rms_matmul_kernel(x_ref, g_ref, w_ref, o_ref, h_ref):
    @pl.when(pl.program_id(1) == 0)
    def _():
        h_ref[...] = _rms(x_ref[...], g_ref[...]).astype(BF16)

    o_ref[...] = jnp.dot(h_ref[...], w_ref[...], preferred_element_type=F32).astype(o_ref.dtype)


def rms_matmul(x, g, w, *, tm, tn):
    m, k = x.shape
    n = w.shape[1]
    return pl.pallas_call(
        _rms_matmul_kernel,
        grid=(m // tm, n // tn),
        in_specs=[
            pl.BlockSpec((tm, k), lambda i, j: (i, 0)),
            pl.BlockSpec((1, k), lambda i, j: (0, 0)),
            pl.BlockSpec((k, tn), lambda i, j: (0, j)),
        ],
        out_specs=pl.BlockSpec((tm, tn), lambda i, j: (i, j)),
        out_shape=jax.ShapeDtypeStruct((m, n), BF16),
        scratch_shapes=[pltpu.VMEM((tm, k), BF16)],
        compiler_params=_params("parallel", "arbitrary", vmem_limit=VMEM_LIMIT_BYTES_IN_PROJ),
        name="rms_matmul",
    )(x, g, w)


def _hgrn2_block(q, zf, v, lb, st):
    c, s = A_CHUNK, A_SUB
    m = c // s
    tb = q.shape[0]
    nch = tb // c
    tril_mask = (lax.broadcasted_iota(jnp.int32, (c, c), 1)
                 <= lax.broadcasted_iota(jnp.int32, (c, c), 0))
    tril = jnp.where(tril_mask, 1.0, 0.0).astype(BF16)

    t = jnp.exp(-jnp.abs(zf))
    u = (1.0 - lb) * (1.0 / (1.0 + t))
    w = t * u
    pos = zf >= 0
    logf = jnp.log2(lb + jnp.where(pos, u, w))
    kk = jnp.where(pos, w, u)

    hi = logf.astype(BF16)
    r1 = logf - hi.astype(F32)
    mid = r1.astype(BF16)
    lo = (r1 - mid.astype(F32)).astype(BF16)
    terms = jnp.concatenate([hi, mid, lo], axis=1)
    b_chunks = []
    for n in range(nch):
        y = jnp.dot(tril, terms[n * c:(n + 1) * c], preferred_element_type=F32)
        b_chunks.append((y[:, :A_DK] + y[:, A_DK:2 * A_DK]) + y[:, 2 * A_DK:])
    b = jnp.concatenate(b_chunks, axis=0)

    shape4 = (nch, m, s, A_DK)
    b4 = b.reshape(shape4)
    g4 = (b4 - logf.reshape(shape4))[:, :, 0:1, :]
    e4 = b4[:, :, s - 1:s, :]
    bl4 = e4[:, m - 1:m]
    lc4 = b4 - g4
    kk4 = kk.reshape(shape4)
    q_sub = (q.reshape(shape4) * jnp.exp2(lc4)).astype(BF16)
    k_diag = (kk4 * jnp.exp2(-lc4)).astype(BF16)
    k_end = (kk4 * jnp.exp2(e4 - b4)).astype(BF16)
    q_in = q_sub * jnp.exp2(g4).astype(BF16)
    k_in = k_end * jnp.exp2(bl4 - e4).astype(BF16)
    decay = jnp.exp2(bl4)
    k_cross = {(i, j): k_end[:, j] * jnp.exp2(g4[:, i] - e4[:, j]).astype(BF16)
               for i in range(m) for j in range(i - 1)}
    vb = v.astype(BF16)
    zeros = jnp.zeros((s, A_DK), BF16)

    o_intra, kv = [], []
    for n in range(nch):
        score_rows = []
        for i in range(m):
            parts = [k_end[n, j] if j == i - 1 else k_cross[(i, j)][n] for j in range(i)]
            parts.append(k_diag[n, i])
            parts.extend([zeros] * (m - 1 - i))
            k_i = jnp.concatenate(parts, axis=0)
            score_rows.append(lax.dot_general(q_sub[n, i], k_i, NT_DIMS,
                                              preferred_element_type=F32))
        scores = jnp.concatenate(score_rows, axis=0)
        scores = jnp.where(tril_mask, scores, 0.0).astype(BF16)
        vb_n = vb[n * c:(n + 1) * c]
        o_intra.append(jnp.dot(scores, vb_n, preferred_element_type=F32))
        kv.append(lax.dot_general(vb_n, k_in[n].reshape(c, A_DK), TN_DIMS,
                                  preferred_element_type=F32))

    states = []
    for n in range(nch):
        states.append(st.astype(BF16))
        st = st * decay[n, 0] + kv[n]
    outs = [o_intra[n] + lax.dot_general(q_in[n].reshape(c, A_DK), states[n], NT_DIMS,
                                         preferred_element_type=F32) for n in range(nch)]
    return jnp.concatenate(outs, axis=0), st


def _even_mixer_kernel(q_ref, f_ref, i_ref, ga_ref, u_ref, v_ref, gb_ref,
                       lbl_ref, onorm_ref, lnw_ref, lnb_ref, ws_ref, bias_ref,
                       oa_ref, ob_ref, st_ref, *, layer_j):
    tb = q_ref.shape[0]

    @pl.when(pl.program_id(2) == 0)
    def _():
        st_ref[...] = jnp.zeros_like(st_ref)

    logits = lbl_ref[...]
    e = jnp.exp(logits - jnp.max(logits, axis=0, keepdims=True))
    lb = jnp.sum(e[:layer_j + 1], axis=0, keepdims=True) / jnp.sum(e, axis=0, keepdims=True)

    o, st = _hgrn2_block(q_ref[...].astype(F32), f_ref[...].astype(F32), i_ref[...].astype(F32),
                         lb, st_ref[...])
    st_ref[...] = st
    oa_ref[...] = (_rms(o, onorm_ref[...]) * _silu(ga_ref[...].astype(F32))).astype(oa_ref.dtype)

    cb = B_CHUNK
    w_causal = jnp.where(lax.broadcasted_iota(jnp.int32, (cb, cb), 1)
                         <= lax.broadcasted_iota(jnp.int32, (cb, cb), 0),
                         ws_ref[...], 0.0).astype(BF16)
    vv = v_ref[...].astype(F32)
    mu = jnp.mean(vv, axis=-1, keepdims=True)
    xc = vv - mu
    vg = (xc * lax.rsqrt(jnp.mean(xc * xc, axis=-1, keepdims=True) + EPS) * lnw_ref[...]
          + lnb_ref[...]).astype(BF16)
    sv = jnp.concatenate([jnp.dot(w_causal, vg[n * cb:(n + 1) * cb], preferred_element_type=F32)
                          for n in range(tb // cb)], axis=0)
    bias = _column_tile(bias_ref[...])
    sv = (sv.reshape(tb // cb, cb, B_DIM) + bias).reshape(tb, B_DIM)
    ob_ref[...] = (u_ref[...].astype(F32) * sv * _silu(gb_ref[...].astype(F32))).astype(ob_ref.dtype)


def even_mixers(z, lb_logits, a_onorm, ln_w, ln_b, ws, bias, *, layer_j, tb):
    bsz, t, _ = z.shape
    h = A_HEADS
    n_rows = lb_logits.shape[0]
    lbl = lb_logits.reshape(n_rows, h, A_DK).transpose(1, 0, 2)

    def sec(k):
        return pl.BlockSpec((None, tb, LANES), lambda b, j, s, k=k: (b, s, k * h + j))

    per_head = lambda shape: pl.BlockSpec((None,) + shape, lambda b, j, s: (j, 0, 0))
    out_spec = pl.BlockSpec((None, tb, LANES), lambda b, j, s: (b, s, j))
    out_sds = jax.ShapeDtypeStruct((bsz, t, h * LANES), BF16)
    return pl.pallas_call(
        functools.partial(_even_mixer_kernel, layer_j=layer_j),
        grid=(bsz, h, t // tb),
        in_specs=[sec(k) for k in range(7)] + [
            per_head((n_rows, LANES)),
            pl.BlockSpec((1, LANES), lambda b, j, s: (0, 0)),
            per_head((1, LANES)),
            per_head((1, LANES)),
            per_head((B_CHUNK, B_CHUNK)),
            per_head((1, B_CHUNK)),
        ],
        out_specs=[out_spec, out_spec],
        out_shape=[out_sds, out_sds],
        scratch_shapes=[pltpu.VMEM((A_DK, A_DK), F32)],
        compiler_params=_params("parallel", "parallel", "arbitrary"),
        name="even_mixers",
    )(z, z, z, z, z, z, z, lbl, a_onorm.reshape(1, LANES),
      ln_w.reshape(B_GROUPS, 1, B_DIM), ln_b.reshape(B_GROUPS, 1, B_DIM),
      ws, bias.reshape(B_GROUPS, 1, B_CHUNK))


def _out_proj_kernel(*refs, n_in, with_next):
    a_refs = refs[:n_in]
    w_refs = refs[n_in:2 * n_in]
    x_ref, post_ref = refs[2 * n_in:2 * n_in + 2]
    rest = refs[2 * n_in + 2:]
    if with_next:
        pre_ref, xo_ref, ho_ref = rest
    else:
        (xo_ref,) = rest
    tm = x_ref.shape[0]
    n_sub = 2 if tm % 32 == 0 else 1
    sub = tm // n_sub

    def project(k):
        rows = slice(k * sub, (k + 1) * sub)
        y = jnp.dot(a_refs[0][rows, :], w_refs[0][...], preferred_element_type=F32)
        for a_ref, w_ref in zip(a_refs[1:], w_refs[1:]):
            y = y + jnp.dot(a_ref[rows, :], w_ref[...], preferred_element_type=F32)
        return y

    def finish(k, y):
        rows = slice(k * sub, (k + 1) * sub)
        x_new = x_ref[rows, :] + _rms(y, post_ref[...])
        xo_ref[rows, :] = x_new
        if with_next:
            ho_ref[rows, :] = _rms(x_new, pre_ref[...]).astype(BF16)

    ys = [project(0)]
    for k in range(n_sub):
        if k + 1 < n_sub:
            ys.append(project(k + 1))
        finish(k, ys[k])


def out_proj(acts, w, x, post_w, next_pre_w, *, tm):
    m, d = x.shape
    n_in = len(acts)
    ka = acts[0].shape[1]
    assert all(a.shape[1] == ka for a in acts) and w.shape == (n_in * ka, d)
    with_next = next_pre_w is not None
    row = lambda width: pl.BlockSpec((tm, width), lambda i: (i, 0))
    const = lambda shape: pl.BlockSpec(shape, lambda i: (0, 0))
    in_specs = ([row(ka) for _ in acts]
                + [pl.BlockSpec((ka, d), lambda i, r=r: (r, 0)) for r in range(n_in)]
                + [row(d), const((1, d))])
    args = list(acts) + [w] * n_in + [x, post_w.reshape(1, d)]
    out_specs = [row(d)]
    out_shape = [jax.ShapeDtypeStruct((m, d), F32)]
    if with_next:
        in_specs.append(const((1, d)))
        args.append(next_pre_w.reshape(1, d))
        out_specs.append(row(d))
        out_shape.append(jax.ShapeDtypeStruct((m, d), BF16))
    return pl.pallas_call(
        functools.partial(_out_proj_kernel, n_in=n_in, with_next=with_next),
        grid=(m // tm,),
        in_specs=in_specs,
        out_specs=out_specs,
        out_shape=out_shape,
        compiler_params=_params("parallel"),
        name="out_proj",
    )(*args)


def _lane_iota():
    return lax.broadcasted_iota(jnp.int32, (1, LANES), 1)


def _join_halves(left, right):
    half = LANES // 2
    return pltpu.roll(jnp.where(_lane_iota() < half, right, left), half, axis=1)


def _odd_in_proj_kernel(h_ref, wt_ref, c_ref, pe_ref, gate_ref):
    h = h_ref[...]
    n_c = c_ref.shape[1]
    n_pe = n_c + C_ROPE
    nt = lambda rows: lax.dot_general(h, wt_ref[rows, :], NT_DIMS, preferred_element_type=F32)
    c_ref[...] = nt(slice(0, n_c)).astype(c_ref.dtype)
    pe_ref[:, :C_ROPE] = nt(slice(n_c, n_pe)).astype(pe_ref.dtype)
    pe_ref[:, C_ROPE:] = jnp.zeros((pe_ref.shape[0], LANES - C_ROPE), pe_ref.dtype)
    gate_ref[...] = nt(slice(n_pe, wt_ref.shape[0])).astype(gate_ref.dtype)


def odd_in_proj(h, wt, *, tm):
    m, k = h.shape
    widths = [2 * C_RANK, LANES, C_HEADS * C_V]
    w = wt
    return pl.pallas_call(
        _odd_in_proj_kernel,
        grid=(m // tm,),
        in_specs=[pl.BlockSpec((tm, k), lambda i: (i, 0)),
                  pl.BlockSpec(w.shape, lambda i: (0, 0))],
        out_specs=[pl.BlockSpec((tm, wd), lambda i: (i, 0)) for wd in widths],
        out_shape=[jax.ShapeDtypeStruct((m, wd), BF16) for wd in widths],
        compiler_params=_params("parallel"),
        name="odd_in_proj",
    )(h, w)


def _mla_expand_kernel(c_ref, pe_ref, pos_ref, freq_ref, qnorm_ref, kvnorm_ref, wq_ref, wk_ref,
                       wvt_ref, qn_ref, qpe_ref, kn_ref, vt_ref, kpe_ref, *, scale):
    c = c_ref[...].astype(F32)
    cq = _rms(c[:, :C_RANK], qnorm_ref[...]).astype(BF16)
    ckv = _rms(c[:, C_RANK:], kvnorm_ref[...]).astype(BF16)

    q = jnp.dot(cq, wq_ref[...], preferred_element_type=F32)
    kn_ref[...] = jnp.dot(ckv, wk_ref[...], preferred_element_type=F32).astype(kn_ref.dtype)
    vt_ref[...] = lax.dot_general(wvt_ref[...], ckv, NT_DIMS,
                                  preferred_element_type=F32).astype(vt_ref.dtype)

    pos = pos_ref[...].astype(F32)
    pos = jnp.concatenate([_column_tile(pos[g:g + 1]) for g in range(pos.shape[0])], axis=0)
    ang = pos * freq_ref[...]
    cos = jnp.cos(ang)
    sin = jnp.sin(ang)
    lane = _lane_iota()
    first_half = lane < C_ROPE
    low_rot = (lane % C_ROPE) < C_ROPE // 2

    def rope(x):
        rot = jnp.where(low_rot, -pltpu.roll(x, LANES - C_ROPE // 2, axis=1),
                        pltpu.roll(x, C_ROPE // 2, axis=1))
        return x * cos + rot * sin

    for p in range(C_HEADS // 2):
        a, b, c3 = (q[:, (3 * p + k) * LANES:(3 * p + k + 1) * LANES] for k in range(3))
        qn_ref[:, 2 * p * LANES:(2 * p + 1) * LANES] = (a * scale).astype(qn_ref.dtype)
        qn_ref[:, (2 * p + 1) * LANES:(2 * p + 2) * LANES] = (
            _join_halves(b, c3) * scale).astype(qn_ref.dtype)
        pe_pair = jnp.where(first_half, b, c3)
        qpe_ref[:, p * LANES:(p + 1) * LANES] = (rope(pe_pair) * scale).astype(qpe_ref.dtype)

    k_pe = rope(pe_ref[...].astype(F32))
    kpe_ref[...] = jnp.where(first_half, k_pe,
                             pltpu.roll(k_pe, C_ROPE, axis=1)).astype(kpe_ref.dtype)


def mla_expand(c, pe, pos, freq, q_norm, kv_norm, wq, wk, wvt, *, bsz, tm, scale):
    m = c.shape[0]
    t = m // bsz
    nb = t // tm
    row = lambda width: pl.BlockSpec((tm, width), lambda i: (i, 0))
    const = lambda shape: pl.BlockSpec(shape, lambda i: (0, 0))
    n_nope = C_HEADS * C_NOPE
    n_v = C_HEADS * C_V
    widths = [n_nope, C_HEADS * C_ROPE, n_nope, None, LANES]
    out_specs = [row(wd) if wd else pl.BlockSpec((None, n_v, tm), lambda i: (i // nb, 0, i % nb))
                 for wd in widths]
    out_shape = [jax.ShapeDtypeStruct((m, wd) if wd else (bsz, n_v, t), BF16) for wd in widths]
    return pl.pallas_call(
        functools.partial(_mla_expand_kernel, scale=scale),
        grid=(m // tm,),
        in_specs=[row(c.shape[1]), row(pe.shape[1]),
                  pl.BlockSpec((None, tm // LANES, LANES), lambda i: (i, 0, 0)), const((1, LANES)),
                  const((1, C_RANK)), const((1, C_RANK)), const(wq.shape), const(wk.shape),
                  const(wvt.shape)],
        out_specs=out_specs,
        out_shape=out_shape,
        compiler_params=_params("parallel"),
        name="mla_expand",
    )(c, pe, pos, freq, q_norm.reshape(1, C_RANK), kv_norm.reshape(1, C_RANK), wq, wk, wvt)


def _attn_kernel(qn_ref, qpe_ref, kn_ref, kpe_ref, vt_ref, g_ref, o_ref, k_scr, v_scr, *, tq):
    t = qn_ref.shape[0]
    heads = tuple(range(qn_ref.shape[1] // LANES))
    for hh in heads:
        k_scr[hh, :, :LANES] = kn_ref[:, hh * LANES:(hh + 1) * LANES]
        k_scr[hh, :, LANES:] = kpe_ref[...]
        v_scr[hh, :C_V] = vt_ref[hh * C_V:(hh + 1) * C_V]
        v_scr[hh, C_V:] = jnp.ones((v_scr.shape[1] - C_V, t), v_scr.dtype)

    lane = lax.broadcasted_iota(jnp.int32, (1, LANES), 1)
    causal = (lax.broadcasted_iota(jnp.int32, (tq, tq), 0)
              <= lax.broadcasted_iota(jnp.int32, (tq, tq), 1))
    neg = jnp.finfo(BF16).min

    def scores(qi, hh):
        r0 = qi * tq
        end = r0 + tq
        q_pe = qpe_ref[r0:end, (hh // 2) * LANES:(hh // 2 + 1) * LANES]
        q_pe = jnp.where((lane // C_ROPE) == hh % 2, q_pe, jnp.zeros_like(q_pe))
        q = jnp.concatenate([qn_ref[r0:end, hh * LANES:(hh + 1) * LANES], q_pe], axis=1)
        return lax.dot_general(k_scr[hh, :end, :], q, NT_DIMS,
                               preferred_element_type=F32).astype(BF16)

    def col_max(qi, s):
        r0 = qi * tq
        m = jnp.max(jnp.where(causal, s[r0:], neg), axis=0, keepdims=True)
        if qi > 0:
            m = jnp.maximum(m, jnp.max(s[:r0], axis=0, keepdims=True))
        return m

    def probs(qi, s, m):
        r0 = qi * tq
        p_d = jnp.where(causal, jnp.exp2(s[r0:] - m), jnp.zeros((), BF16))
        p_o = jnp.exp2(s[:r0] - m) if qi > 0 else None
        return p_d, p_o

    def values(qi, hh, p_d, p_o):
        r0 = qi * tq
        end = r0 + tq
        vt = v_scr.at[hh]
        acc = jnp.dot(vt[:, r0:end], p_d, preferred_element_type=F32)
        if qi > 0:
            acc = acc + jnp.dot(vt[:, :r0], p_o, preferred_element_type=F32)
        o = (acc[:C_V] * (1.0 / acc[C_V:C_V + 1])).T
        cols = slice(hh * C_V, (hh + 1) * C_V)
        o_ref[r0:end, cols] = (o * _silu(g_ref[r0:end, cols].astype(F32))).astype(o_ref.dtype)

    nq = t // tq
    s = [scores(0, hh) for hh in heads]
    sm = [col_max(0, s[hh]) for hh in heads]
    for qi in range(nq):
        last = qi + 1 == nq
        for hh in heads:
            s_next = None if last else scores(qi + 1, hh)
            values(qi, hh, *probs(qi, s[hh], sm[hh]))
            if not last:
                s[hh] = s_next
                sm[hh] = col_max(qi + 1, s_next)


def attention(qn, qpe, kn, kpe, vt, gate, *, tq):
    bsz, t, _ = qn.shape
    g = ATTN_HEADS_PER_STEP
    group = lambda width: pl.BlockSpec((None, t, width), lambda b, p: (b, 0, p))
    shared = pl.BlockSpec((None, t, LANES), lambda b, p: (b, 0, 0))
    group_t = pl.BlockSpec((None, g * C_V, t), lambda b, p: (b, p, 0))
    return pl.pallas_call(
        functools.partial(_attn_kernel, tq=tq),
        grid=(bsz, C_HEADS // g),
        in_specs=[group(g * LANES), group(g * C_ROPE), group(g * LANES), shared, group_t,
                  group(g * C_V)],
        out_specs=group(g * C_V),
        out_shape=jax.ShapeDtypeStruct((bsz, t, C_HEADS * C_V), BF16),
        scratch_shapes=[pltpu.VMEM((g, t, 2 * LANES), BF16),
                        pltpu.VMEM((g, C_V + BF16_SUBLANES, t), BF16)],
        compiler_params=_params("parallel", "parallel"),
        name="mla_attention",
    )(qn, qpe, kn, kpe, vt, gate)


def _odd_weights(w_in, w_qb, w_kvb):
    w_in_b = w_in.T
    kvb = w_kvb.astype(BF16).reshape(C_RANK, C_HEADS, C_NOPE + C_V)
    wk = kvb[:, :, :C_NOPE].reshape(C_RANK, -1)
    wvt = kvb[:, :, C_NOPE:].reshape(C_RANK, -1).T
    return w_in_b, w_qb, wk, wvt


def kernel(x, positions, norm_pre, norm_post, ev_w_in, ev_lb_logits, ev_a_onorm, ev_b_ln_w,
           ev_b_ln_b, ev_b_ws, ev_b_bias, ev_w_out, od_w_in, od_q_norm, od_w_qb, od_kv_norm,
           od_w_kvb, od_w_out):
    bsz, t, d = x.shape
    m = bsz * t
    xf = x.reshape(m, d)
    tm_big = min(1024, m)
    tm = min(512, m)

    z = rms_matmul(xf, norm_pre[0].reshape(1, d), ev_w_in[0], tm=tm_big, tn=1792)
    mix_a, mix_b = even_mixers(z.reshape(bsz, t, -1), ev_lb_logits, ev_a_onorm[0], ev_b_ln_w[0],
                               ev_b_ln_b[0], ev_b_ws[0], ev_b_bias[0], layer_j=0, tb=min(2048, t))
    x1, h1 = out_proj([mix_a.reshape(m, -1), mix_b.reshape(m, -1)], ev_w_out[0],
                      xf, norm_post[0], norm_pre[1], tm=tm)

    w_in_b, wq_all, wk, wvt = _odd_weights(od_w_in[0], od_w_qb[0], od_w_kvb[0])
    c, pe, gate = odd_in_proj(h1, w_in_b, tm=tm)
    inv_freq = ROPE_THETA ** (-jnp.arange(0, C_ROPE, 2, dtype=F32) / C_ROPE)
    freq = jnp.tile(inv_freq, LANES // (C_ROPE // 2)).reshape(1, LANES)
    scale = float((C_NOPE + C_ROPE) ** -0.5 * LOG2E)
    tm_e = min(tm, t)
    pos = positions.reshape(m // tm_e, tm_e // LANES, LANES)
    qn, qpe, kn, vt, kpe = mla_expand(c, pe, pos, freq, od_q_norm[0], od_kv_norm[0], wq_all, wk,
                                      wvt, bsz=bsz, tm=tm_e, scale=scale)
    r3 = lambda a: a.reshape(bsz, t, -1)
    o = attention(r3(qn), r3(qpe), r3(kn), r3(kpe), vt, r3(gate), tq=min(256, t))
    (x2,) = out_proj([o.reshape(m, -1)], od_w_out[0], x1, norm_post[1], None, tm=tm)
    return x2.reshape(bsz, t, d)
```

```python
import functools

import jax
import jax.numpy as jnp
from jax import lax
from jax.experimental import pallas as pl
from jax.experimental.pallas import tpu as pltpu

F32 = jnp.float32
BF16 = jnp.bfloat16

EPS = 1e-6
LANES = 128
BF16_SUBLANES = 16
VMEM_LIMIT_BYTES = 56 * 1024 * 1024
VMEM_LIMIT_BYTES_IN_PROJ = 62 * 1024 * 1024

A_HEADS = 8
A_DK = 128
A_CHUNK = 64
A_SUB = 16
B_GROUPS = 8
B_DIM = 128
B_CHUNK = 128
C_HEADS = 16
C_RANK = 512
C_NOPE = 128
C_ROPE = 64
C_V = 128
ROPE_THETA = 10000.0
LOG2E = 1.4426950408889634
ATTN_HEADS_PER_STEP = 4

NT_DIMS = (((1,), (1,)), ((), ()))
TN_DIMS = (((0,), (0,)), ((), ()))


def _params(*sem, vmem_limit=VMEM_LIMIT_BYTES):
    return pltpu.CompilerParams(dimension_semantics=sem, vmem_limit_bytes=vmem_limit)


def _rms(x, w):
    ms = jnp.mean(x * x, axis=-1, keepdims=True)
    return x * lax.rsqrt(ms + EPS) * w


def _silu(x):
    return x * (1.0 / (1.0 + jnp.exp(-x)))


def _column_tile(row):
    return jnp.broadcast_to(row, (LANES, LANES)).T


def _rms_matmul_kernel(x_ref, g_ref, w_ref, o_ref, h_ref):
    @pl.when(pl.program_id(1) == 0)
    def _():
        h_ref[...] = _rms(x_ref[...], g_ref[...]).astype(BF16)

    o_ref[...] = jnp.dot(h_ref[...], w_ref[...], preferred_element_type=F32).astype(o_ref.dtype)


def rms_matmul(x, g, w, *, tm, tn):
    m, k = x.shape
    n = w.shape[1]
    return pl.pallas_call(
        _rms_matmul_kernel,
        grid=(m // tm, n // tn),
        in_specs=[
            pl.BlockSpec((tm, k), lambda i, j: (i, 0)),
            pl.BlockSpec((1, k), lambda i, j: (0, 0)),
            pl.BlockSpec((k, tn), lambda i, j: (0, j)),
        ],
        out_specs=pl.BlockSpec((tm, tn), lambda i, j: (i, j)),
        out_shape=jax.ShapeDtypeStruct((m, n), BF16),
        scratch_shapes=[pltpu.VMEM((tm, k), BF16)],
        compiler_params=_params("parallel", "arbitrary", vmem_limit=VMEM_LIMIT_BYTES_IN_PROJ),
        name="rms_matmul",
    )(x, g, w)


def _hgrn2_block(q, zf, v, lb, st):
    c, s = A_CHUNK, A_SUB
    m = c // s
    tb = q.shape[0]
    nch = tb // c
    tril_mask = (lax.broadcasted_iota(jnp.int32, (c, c), 1)
                 <= lax.broadcasted_iota(jnp.int32, (c, c), 0))
    tril = jnp.where(tril_mask, 1.0, 0.0).astype(BF16)

    t = jnp.exp(-jnp.abs(zf))
    u = (1.0 - lb) * (1.0 / (1.0 + t))
    w = t * u
    pos = zf >= 0
    logf = jnp.log2(lb + jnp.where(pos, u, w))
    kk = jnp.where(pos, w, u)

    hi = logf.astype(BF16)
    r1 = logf - hi.astype(F32)
    mid = r1.astype(BF16)
    lo = (r1 - mid.astype(F32)).astype(BF16)
    terms = jnp.concatenate([hi, mid, lo], axis=1)
    b_chunks = []
    for n in range(nch):
        y = jnp.dot(tril, terms[n * c:(n + 1) * c], preferred_element_type=F32)
        b_chunks.append((y[:, :A_DK] + y[:, A_DK:2 * A_DK]) + y[:, 2 * A_DK:])
    b = jnp.concatenate(b_chunks, axis=0)

    shape4 = (nch, m, s, A_DK)
    b4 = b.reshape(shape4)
    g4 = (b4 - logf.reshape(shape4))[:, :, 0:1, :]
    e4 = b4[:, :, s - 1:s, :]
    bl4 = e4[:, m - 1:m]
    lc4 = b4 - g4
    kk4 = kk.reshape(shape4)
    q_sub = (q.reshape(shape4) * jnp.exp2(lc4)).astype(BF16)
    k_diag = (kk4 * jnp.exp2(-lc4)).astype(BF16)
    k_end = (kk4 * jnp.exp2(e4 - b4)).astype(BF16)
    q_in = q_sub * jnp.exp2(g4).astype(BF16)
    k_in = k_end * jnp.exp2(bl4 - e4).astype(BF16)
    decay = jnp.exp2(bl4)
    k_cross = {(i, j): k_end[:, j] * jnp.exp2(g4[:, i] - e4[:, j]).astype(BF16)
               for i in range(m) for j in range(i - 1)}
    vb = v.astype(BF16)
    zeros = jnp.zeros((s, A_DK), BF16)

    o_intra, kv = [], []
    for n in range(nch):
        score_rows = []
        for i in range(m):
            parts = [k_end[n, j] if j == i - 1 else k_cross[(i, j)][n] for j in range(i)]
            parts.append(k_diag[n, i])
            parts.extend([zeros] * (m - 1 - i))
            k_i = jnp.concatenate(parts, axis=0)
            score_rows.append(lax.dot_general(q_sub[n, i], k_i, NT_DIMS,
                                              preferred_element_type=F32))
        scores = jnp.concatenate(score_rows, axis=0)
        scores = jnp.where(tril_mask, scores, 0.0).astype(BF16)
        vb_n = vb[n * c:(n + 1) * c]
        o_intra.append(jnp.dot(scores, vb_n, preferred_element_type=F32))
        kv.append(lax.dot_general(vb_n, k_in[n].reshape(c, A_DK), TN_DIMS,
                                  preferred_element_type=F32))

    states = []
    for n in range(nch):
        states.append(st.astype(BF16))
        st = st * decay[n, 0] + kv[n]
    outs = [o_intra[n] + lax.dot_general(q_in[n].reshape(c, A_DK), states[n], NT_DIMS,
                                         preferred_element_type=F32) for n in range(nch)]
    return jnp.concatenate(outs, axis=0), st


def _even_mixer_kernel(q_ref, f_ref, i_ref, ga_ref, u_ref, v_ref, gb_ref,
                       lbl_ref, onorm_ref, lnw_ref, lnb_ref, ws_ref, bias_ref,
                       oa_ref, ob_ref, st_ref, *, layer_j):
    tb = q_ref.shape[0]

    @pl.when(pl.program_id(2) == 0)
    def _():
        st_ref[...] = jnp.zeros_like(st_ref)

    logits = lbl_ref[...]
    e = jnp.exp(logits - jnp.max(logits, axis=0, keepdims=True))
    lb = jnp.sum(e[:layer_j + 1], axis=0, keepdims=True) / jnp.sum(e, axis=0, keepdims=True)

    o, st = _hgrn2_block(q_ref[...].astype(F32), f_ref[...].astype(F32), i_ref[...].astype(F32),
                         lb, st_ref[...])
    st_ref[...] = st
    oa_ref[...] = (_rms(o, onorm_ref[...]) * _silu(ga_ref[...].astype(F32))).astype(oa_ref.dtype)

    cb = B_CHUNK
    w_causal = jnp.where(lax.broadcasted_iota(jnp.int32, (cb, cb), 1)
                         <= lax.broadcasted_iota(jnp.int32, (cb, cb), 0),
                         ws_ref[...], 0.0).astype(BF16)
    vv = v_ref[...].astype(F32)
    mu = jnp.mean(vv, axis=-1, keepdims=True)
    xc = vv - mu
    vg = (xc * lax.rsqrt(jnp.mean(xc * xc, axis=-1, keepdims=True) + EPS) * lnw_ref[...]
          + lnb_ref[...]).astype(BF16)
    sv = jnp.concatenate([jnp.dot(w_causal, vg[n * cb:(n + 1) * cb], preferred_element_type=F32)
                          for n in range(tb // cb)], axis=0)
    bias = _column_tile(bias_ref[...])
    sv = (sv.reshape(tb // cb, cb, B_DIM) + bias).reshape(tb, B_DIM)
    ob_ref[...] = (u_ref[...].astype(F32) * sv * _silu(gb_ref[...].astype(F32))).astype(ob_ref.dtype)


def even_mixers(z, lb_logits, a_onorm, ln_w, ln_b, ws, bias, *, layer_j, tb):
    bsz, t, _ = z.shape
    h = A_HEADS
    n_rows = lb_logits.shape[0]
    lbl = lb_logits.reshape(n_rows, h, A_DK).transpose(1, 0, 2)

    def sec(k):
        return pl.BlockSpec((None, tb, LANES), lambda b, j, s, k=k: (b, s, k * h + j))

    per_head = lambda shape: pl.BlockSpec((None,) + shape, lambda b, j, s: (j, 0, 0))
    out_spec = pl.BlockSpec((None, tb, LANES), lambda b, j, s: (b, s, j))
    out_sds = jax.ShapeDtypeStruct((bsz, t, h * LANES), BF16)
    return pl.pallas_call(
        functools.partial(_even_mixer_kernel, layer_j=layer_j),
        grid=(bsz, h, t // tb),
        in_specs=[sec(k) for k in range(7)] + [
            per_head((n_rows, LANES)),
            pl.BlockSpec((1, LANES), lambda b, j, s: (0, 0)),
            per_head((1, LANES)),
            per_head((1, LANES)),
            per_head((B_CHUNK, B_CHUNK)),
            per_head((1, B_CHUNK)),
        ],
        out_specs=[out_spec, out_spec],
        out_shape=[out_sds, out_sds],
        scratch_shapes=[pltpu.VMEM((A_DK, A_DK), F32)],
        compiler_params=_params("parallel", "parallel", "arbitrary"),
        name="even_mixers",
    )(z, z, z, z, z, z, z, lbl, a_onorm.reshape(1, LANES),
      ln_w.reshape(B_GROUPS, 1, B_DIM), ln_b.reshape(B_GROUPS, 1, B_DIM),
      ws, bias.reshape(B_GROUPS, 1, B_CHUNK))


def _out_proj_kernel(*refs, n_in, with_next):
    a_refs = refs[:n_in]
    w_refs = refs[n_in:2 * n_in]
    x_ref, post_ref = refs[2 * n_in:2 * n_in + 2]
    rest = refs[2 * n_in + 2:]
    if with_next:
        pre_ref, xo_ref, ho_ref = rest
    else:
        (xo_ref,) = rest
    tm = x_ref.shape[0]
    n_sub = 2 if tm % 32 == 0 else 1
    sub = tm // n_sub

    def project(k):
        rows = slice(k * sub, (k + 1) * sub)
        y = jnp.dot(a_refs[0][rows, :], w_refs[0][...], preferred_element_type=F32)
        for a_ref, w_ref in zip(a_refs[1:], w_refs[1:]):
            y = y + jnp.dot(a_ref[rows, :], w_ref[...], preferred_element_type=F32)
        return y

    def finish(k, y):
        rows = slice(k * sub, (k + 1) * sub)
        x_new = x_ref[rows, :] + _rms(y, post_ref[...])
        xo_ref[rows, :] = x_new
        if with_next:
            ho_ref[rows, :] = _rms(x_new, pre_ref[...]).astype(BF16)

    ys = [project(0)]
    for k in range(n_sub):
        if k + 1 < n_sub:
            ys.append(project(k + 1))
        finish(k, ys[k])


def out_proj(acts, w, x, post_w, next_pre_w, *, tm):
    m, d = x.shape
    n_in = len(acts)
    ka = acts[0].shape[1]
    assert all(a.shape[1] == ka for a in acts) and w.shape == (n_in * ka, d)
    with_next = next_pre_w is not None
    row = lambda width: pl.BlockSpec((tm, width), lambda i: (i, 0))
    const = lambda shape: pl.BlockSpec(shape, lambda i: (0, 0))
    in_specs = ([row(ka) for _ in acts]
                + [pl.BlockSpec((ka, d), lambda i, r=r: (r, 0)) for r in range(n_in)]
                + [row(d), const((1, d))])
    args = list(acts) + [w] * n_in + [x, post_w.reshape(1, d)]
    out_specs = [row(d)]
    out_shape = [jax.ShapeDtypeStruct((m, d), F32)]
    if with_next:
        in_specs.append(const((1, d)))
        args.append(next_pre_w.reshape(1, d))
        out_specs.append(row(d))
        out_shape.append(jax.ShapeDtypeStruct((m, d), BF16))
    return pl.pallas_call(
        functools.partial(_out_proj_kernel, n_in=n_in, with_next=with_next),
        grid=(m // tm,),
        in_specs=in_specs,
        out_specs=out_specs,
        out_shape=out_shape,
        compiler_params=_params("parallel"),
        name="out_proj",
    )(*args)


def _lane_iota():
    return lax.broadcasted_iota(jnp.int32, (1, LANES), 1)


def _join_halves(left, right):
    half = LANES // 2
    return pltpu.roll(jnp.where(_lane_iota() < half, right, left), half, axis=1)


def _odd_in_proj_kernel(h_ref, wt_ref, c_ref, pe_ref, gate_ref):
    h = h_ref[...]
    n_c = c_ref.shape[1]
    n_pe = n_c + C_ROPE
    nt = lambda rows: lax.dot_general(h, wt_ref[rows, :], NT_DIMS, preferred_element_type=F32)
    c_ref[...] = nt(slice(0, n_c)).astype(c_ref.dtype)
    pe_ref[:, :C_ROPE] = nt(slice(n_c, n_pe)).astype(pe_ref.dtype)
    pe_ref[:, C_ROPE:] = jnp.zeros((pe_ref.shape[0], LANES - C_ROPE), pe_ref.dtype)
    gate_ref[...] = nt(slice(n_pe, wt_ref.shape[0])).astype(gate_ref.dtype)


def odd_in_proj(h, wt, *, tm):
    m, k = h.shape
    widths = [2 * C_RANK, LANES, C_HEADS * C_V]
    w = wt
    return pl.pallas_call(
        _odd_in_proj_kernel,
        grid=(m // tm,),
        in_specs=[pl.BlockSpec((tm, k), lambda i: (i, 0)),
                  pl.BlockSpec(w.shape, lambda i: (0, 0))],
        out_specs=[pl.BlockSpec((tm, wd), lambda i: (i, 0)) for wd in widths],
        out_shape=[jax.ShapeDtypeStruct((m, wd), BF16) for wd in widths],
        compiler_params=_params("parallel"),
        name="odd_in_proj",
    )(h, w)


def _mla_expand_kernel(c_ref, pe_ref, pos_ref, freq_ref, qnorm_ref, kvnorm_ref, wq_ref, wkv_ref,
                       qn_ref, qpe_ref, kn_ref, vt_ref, kpe_ref, wk_ref, wvt_ref, *, scale):
    @pl.when(pl.program_id(0) == 0)
    def _():
        for h in range(C_HEADS):
            lo = h * (C_NOPE + C_V)
            wk_ref[:, h * C_NOPE:(h + 1) * C_NOPE] = wkv_ref[:, lo:lo + C_NOPE].astype(wk_ref.dtype)
            wvt_ref[h * C_V:(h + 1) * C_V, :] = wkv_ref[:, lo + C_NOPE:lo + C_NOPE + C_V].T.astype(
                wvt_ref.dtype)

    c = c_ref[...].astype(F32)
    cq = _rms(c[:, :C_RANK], qnorm_ref[...]).astype(BF16)
    ckv = _rms(c[:, C_RANK:], kvnorm_ref[...]).astype(BF16)

    q = jnp.dot(cq, wq_ref[...], preferred_element_type=F32)
    kn_ref[...] = jnp.dot(ckv, wk_ref[...], preferred_element_type=F32).astype(kn_ref.dtype)
    vt_ref[...] = lax.dot_general(wvt_ref[...], ckv, NT_DIMS,
                                  preferred_element_type=F32).astype(vt_ref.dtype)

    pos = pos_ref[...].astype(F32)
    pos = jnp.concatenate([_column_tile(pos[g:g + 1]) for g in range(pos.shape[0])], axis=0)
    ang = pos * freq_ref[...]
    cos = jnp.cos(ang)
    sin = jnp.sin(ang)
    lane = _lane_iota()
    first_half = lane < C_ROPE
    low_rot = (lane % C_ROPE) < C_ROPE // 2

    def rope(x):
        rot = jnp.where(low_rot, -pltpu.roll(x, LANES - C_ROPE // 2, axis=1),
                        pltpu.roll(x, C_ROPE // 2, axis=1))
        return x * cos + rot * sin

    for p in range(C_HEADS // 2):
        a, b, c3 = (q[:, (3 * p + k) * LANES:(3 * p + k + 1) * LANES] for k in range(3))
        qn_ref[:, 2 * p * LANES:(2 * p + 1) * LANES] = (a * scale).astype(qn_ref.dtype)
        qn_ref[:, (2 * p + 1) * LANES:(2 * p + 2) * LANES] = (
            _join_halves(b, c3) * scale).astype(qn_ref.dtype)
        pe_pair = jnp.where(first_half, b, c3)
        qpe_ref[:, p * LANES:(p + 1) * LANES] = (rope(pe_pair) * scale).astype(qpe_ref.dtype)

    k_pe = rope(pe_ref[...].astype(F32))
    kpe_ref[...] = jnp.where(first_half, k_pe,
                             pltpu.roll(k_pe, C_ROPE, axis=1)).astype(kpe_ref.dtype)


def mla_expand(c, pe, pos, freq, q_norm, kv_norm, wq, wkv, *, bsz, tm, scale):
    m = c.shape[0]
    t = m // bsz
    nb = t // tm
    row = lambda width: pl.BlockSpec((tm, width), lambda i: (i, 0))
    const = lambda shape: pl.BlockSpec(shape, lambda i: (0, 0))
    n_nope = C_HEADS * C_NOPE
    n_v = C_HEADS * C_V
    widths = [n_nope, C_HEADS * C_ROPE, n_nope, None, LANES]
    out_specs = [row(wd) if wd else pl.BlockSpec((None, n_v, tm), lambda i: (i // nb, 0, i % nb))
                 for wd in widths]
    out_shape = [jax.ShapeDtypeStruct((m, wd) if wd else (bsz, n_v, t), BF16) for wd in widths]
    return pl.pallas_call(
        functools.partial(_mla_expand_kernel, scale=scale),
        grid=(m // tm,),
        in_specs=[row(c.shape[1]), row(pe.shape[1]),
                  pl.BlockSpec((None, tm // LANES, LANES), lambda i: (i, 0, 0)), const((1, LANES)),
                  const((1, C_RANK)), const((1, C_RANK)), const(wq.shape), const(wkv.shape)],
        out_specs=out_specs,
        out_shape=out_shape,
        scratch_shapes=[pltpu.VMEM((C_RANK, n_nope), BF16), pltpu.VMEM((n_v, C_RANK), BF16)],
        compiler_params=_params("arbitrary"),
        name="mla_expand",
    )(c, pe, pos, freq, q_norm.reshape(1, C_RANK), kv_norm.reshape(1, C_RANK), wq, wkv)


def _attn_kernel(qn_ref, qpe_ref, kn_ref, kpe_ref, vt_ref, g_ref, o_ref, k_scr, v_scr, *, tq):
    t = qn_ref.shape[0]
    heads = tuple(range(qn_ref.shape[1] // LANES))
    for hh in heads:
        k_scr[hh, :, :LANES] = kn_ref[:, hh * LANES:(hh + 1) * LANES]
        k_scr[hh, :, LANES:] = kpe_ref[...]
        v_scr[hh, :C_V] = vt_ref[hh * C_V:(hh + 1) * C_V]
        v_scr[hh, C_V:] = jnp.ones((v_scr.shape[1] - C_V, t), v_scr.dtype)

    lane = lax.broadcasted_iota(jnp.int32, (1, LANES), 1)
    causal = (lax.broadcasted_iota(jnp.int32, (tq, tq), 0)
              <= lax.broadcasted_iota(jnp.int32, (tq, tq), 1))
    neg = jnp.finfo(BF16).min

    def scores(qi, hh):
        r0 = qi * tq
        end = r0 + tq
        q_pe = qpe_ref[r0:end, (hh // 2) * LANES:(hh // 2 + 1) * LANES]
        q_pe = jnp.where((lane // C_ROPE) == hh % 2, q_pe, jnp.zeros_like(q_pe))
        q = jnp.concatenate([qn_ref[r0:end, hh * LANES:(hh + 1) * LANES], q_pe], axis=1)
        return lax.dot_general(k_scr[hh, :end, :], q, NT_DIMS,
                               preferred_element_type=F32).astype(BF16)

    def col_max(qi, s):
        r0 = qi * tq
        m = jnp.max(jnp.where(causal, s[r0:], neg), axis=0, keepdims=True)
        if qi > 0:
            m = jnp.maximum(m, jnp.max(s[:r0], axis=0, keepdims=True))
        return m

    def probs(qi, s, m):
        r0 = qi * tq
        p_d = jnp.where(causal, jnp.exp2(s[r0:] - m), jnp.zeros((), BF16))
        p_o = jnp.exp2(s[:r0] - m) if qi > 0 else None
        return p_d, p_o

    def values(qi, hh, p_d, p_o):
        r0 = qi * tq
        end = r0 + tq
        vt = v_scr.at[hh]
        acc = jnp.dot(vt[:, r0:end], p_d, preferred_element_type=F32)
        if qi > 0:
            acc = acc + jnp.dot(vt[:, :r0], p_o, preferred_element_type=F32)
        o = (acc[:C_V] * (1.0 / acc[C_V:C_V + 1])).T
        cols = slice(hh * C_V, (hh + 1) * C_V)
        o_ref[r0:end, cols] = (o * _silu(g_ref[r0:end, cols].astype(F32))).astype(o_ref.dtype)

    nq = t // tq
    s = [scores(0, hh) for hh in heads]
    sm = [col_max(0, s[hh]) for hh in heads]
    for qi in range(nq):
        last = qi + 1 == nq
        for hh in heads:
            s_next = None if last else scores(qi + 1, hh)
            values(qi, hh, *probs(qi, s[hh], sm[hh]))
            if not last:
                s[hh] = s_next
                sm[hh] = col_max(qi + 1, s_next)


def attention(qn, qpe, kn, kpe, vt, gate, *, tq):
    bsz, t, _ = qn.shape
    g = ATTN_HEADS_PER_STEP
    group = lambda width: pl.BlockSpec((None, t, width), lambda b, p: (b, 0, p))
    shared = pl.BlockSpec((None, t, LANES), lambda b, p: (b, 0, 0))
    group_t = pl.BlockSpec((None, g * C_V, t), lambda b, p: (b, p, 0))
    return pl.pallas_call(
        functools.partial(_attn_kernel, tq=tq),
        grid=(bsz, C_HEADS // g),
        in_specs=[group(g * LANES), group(g * C_ROPE), group(g * LANES), shared, group_t,
                  group(g * C_V)],
        out_specs=group(g * C_V),
        out_shape=jax.ShapeDtypeStruct((bsz, t, C_HEADS * C_V), BF16),
        scratch_shapes=[pltpu.VMEM((g, t, 2 * LANES), BF16),
                        pltpu.VMEM((g, C_V + BF16_SUBLANES, t), BF16)],
        compiler_params=_params("parallel", "parallel"),
        name="mla_attention",
    )(qn, qpe, kn, kpe, vt, gate)


def kernel(x, positions, norm_pre, norm_post, ev_w_in, ev_lb_logits, ev_a_onorm, ev_b_ln_w,
           ev_b_ln_b, ev_b_ws, ev_b_bias, ev_w_out, od_w_in, od_q_norm, od_w_qb, od_kv_norm,
           od_w_kvb, od_w_out):
    bsz, t, d = x.shape
    m = bsz * t
    xf = x.reshape(m, d)
    tm_big = min(1024, m)
    tm = min(512, m)

    z = rms_matmul(xf, norm_pre[0].reshape(1, d), ev_w_in[0], tm=tm_big, tn=1792)
    mix_a, mix_b = even_mixers(z.reshape(bsz, t, -1), ev_lb_logits, ev_a_onorm[0], ev_b_ln_w[0],
                               ev_b_ln_b[0], ev_b_ws[0], ev_b_bias[0], layer_j=0, tb=min(2048, t))
    x1, h1 = out_proj([mix_a.reshape(m, -1), mix_b.reshape(m, -1)], ev_w_out[0],
                      xf, norm_post[0], norm_pre[1], tm=tm)

    c, pe, gate = odd_in_proj(h1, od_w_in[0].T, tm=tm)
    inv_freq = ROPE_THETA ** (-jnp.arange(0, C_ROPE, 2, dtype=F32) / C_ROPE)
    freq = jnp.tile(inv_freq, LANES // (C_ROPE // 2)).reshape(1, LANES)
    scale = float((C_NOPE + C_ROPE) ** -0.5 * LOG2E)
    tm_e = min(tm, t)
    pos = positions.reshape(m // tm_e, tm_e // LANES, LANES)
    qn, qpe, kn, vt, kpe = mla_expand(c, pe, pos, freq, od_q_norm[0], od_kv_norm[0], od_w_qb[0],
                                      od_w_kvb[0], bsz=bsz, tm=tm_e, scale=scale)
    r3 = lambda a: a.reshape(bsz, t, -1)
    o = attention(r3(qn), r3(qpe), r3(kn), r3(kpe), vt, r3(gate), tq=min(256, t))
    (x2,) = out_proj([o.reshape(m, -1)], od_w_out[0], x1, norm_post[1], None, tm=tm)
    return x2.reshape(bsz, t, d)
```

```python
import functools

import jax
import jax.numpy as jnp
from jax import lax
from jax.experimental import pallas as pl
from jax.experimental.pallas import tpu as pltpu

F32 = jnp.float32
BF16 = jnp.bfloat16

EPS = 1e-6
LANES = 128
BF16_SUBLANES = 16
VMEM_LIMIT_BYTES = 56 * 1024 * 1024
VMEM_LIMIT_BYTES_IN_PROJ = 62 * 1024 * 1024

A_HEADS = 8
A_DK = 128
A_CHUNK = 64
A_SUB = 16
B_GROUPS = 8
B_DIM = 128
B_CHUNK = 128
C_HEADS = 16
C_RANK = 512
C_NOPE = 128
C_ROPE = 64
C_V = 128
ROPE_THETA = 10000.0
LOG2E = 1.4426950408889634
ATTN_HEADS_PER_STEP = 4

IN_PROJ_TM, IN_PROJ_TN = 1024, 1792
ROW_TILE = 512
MIXER_BLOCK = 2048
ATTN_TQ = 256

NT_DIMS = (((1,), (1,)), ((), ()))
TN_DIMS = (((0,), (0,)), ((), ()))


def _params(*sem, vmem_limit=VMEM_LIMIT_BYTES):
    return pltpu.CompilerParams(dimension_semantics=sem, vmem_limit_bytes=vmem_limit)


def _rms(x, w):
    ms = jnp.mean(x * x, axis=-1, keepdims=True)
    return x * lax.rsqrt(ms + EPS) * w


def _silu(x):
    return x * (1.0 / (1.0 + jnp.exp(-x)))


def _column_tile(row):
    return jnp.broadcast_to(row, (LANES, LANES)).T


def _rms_matmul_kernel(x_ref, g_ref, w_ref, o_ref, h_ref):
    @pl.when(pl.program_id(1) == 0)
    def _():
        h_ref[...] = _rms(x_ref[...], g_ref[...]).astype(BF16)

    o_ref[...] = jnp.dot(h_ref[...], w_ref[...], preferred_element_type=F32).astype(o_ref.dtype)


def rms_matmul(x, g, w, *, tm, tn):
    m, k = x.shape
    n = w.shape[1]
    return pl.pallas_call(
        _rms_matmul_kernel,
        grid=(m // tm, n // tn),
        in_specs=[
            pl.BlockSpec((tm, k), lambda i, j: (i, 0)),
            pl.BlockSpec((1, k), lambda i, j: (0, 0)),
            pl.BlockSpec((k, tn), lambda i, j: (0, j)),
        ],
        out_specs=pl.BlockSpec((tm, tn), lambda i, j: (i, j)),
        out_shape=jax.ShapeDtypeStruct((m, n), BF16),
        scratch_shapes=[pltpu.VMEM((tm, k), BF16)],
        compiler_params=_params("parallel", "arbitrary", vmem_limit=VMEM_LIMIT_BYTES_IN_PROJ),
        name="rms_matmul",
    )(x, g, w)


def _hgrn2_block(q, zf, v, lb, st):
    c, s = A_CHUNK, A_SUB
    m = c // s
    tb = q.shape[0]
    nch = tb // c
    tril_mask = (lax.broadcasted_iota(jnp.int32, (c, c), 1)
                 <= lax.broadcasted_iota(jnp.int32, (c, c), 0))
    tril = jnp.where(tril_mask, 1.0, 0.0).astype(BF16)

    t = jnp.exp(-jnp.abs(zf))
    u = (1.0 - lb) * (1.0 / (1.0 + t))
    w = t * u
    pos = zf >= 0
    logf = jnp.log2(lb + jnp.where(pos, u, w))
    kk = jnp.where(pos, w, u)

    hi = logf.astype(BF16)
    r1 = logf - hi.astype(F32)
    mid = r1.astype(BF16)
    lo = (r1 - mid.astype(F32)).astype(BF16)
    terms = jnp.concatenate([hi, mid, lo], axis=1)
    b_chunks = []
    for n in range(nch):
        y = jnp.dot(tril, terms[n * c:(n + 1) * c], preferred_element_type=F32)
        b_chunks.append((y[:, :A_DK] + y[:, A_DK:2 * A_DK]) + y[:, 2 * A_DK:])
    b = jnp.concatenate(b_chunks, axis=0)

    shape4 = (nch, m, s, A_DK)
    b4 = b.reshape(shape4)
    g4 = (b4 - logf.reshape(shape4))[:, :, 0:1, :]
    e4 = b4[:, :, s - 1:s, :]
    bl4 = e4[:, m - 1:m]
    lc4 = b4 - g4
    kk4 = kk.reshape(shape4)
    q_sub = (q.reshape(shape4) * jnp.exp2(lc4)).astype(BF16)
    k_diag = (kk4 * jnp.exp2(-lc4)).astype(BF16)
    k_end = (kk4 * jnp.exp2(e4 - b4)).astype(BF16)
    q_in = q_sub * jnp.exp2(g4).astype(BF16)
    k_in = k_end * jnp.exp2(bl4 - e4).astype(BF16)
    decay = jnp.exp2(bl4)
    k_cross = {(i, j): k_end[:, j] * jnp.exp2(g4[:, i] - e4[:, j]).astype(BF16)
               for i in range(m) for j in range(i - 1)}
    vb = v.astype(BF16)
    zeros = jnp.zeros((s, A_DK), BF16)

    o_intra, kv = [], []
    for n in range(nch):
        score_rows = []
        for i in range(m):
            parts = [k_end[n, j] if j == i - 1 else k_cross[(i, j)][n] for j in range(i)]
            parts.append(k_diag[n, i])
            parts.extend([zeros] * (m - 1 - i))
            k_i = jnp.concatenate(parts, axis=0)
            score_rows.append(lax.dot_general(q_sub[n, i], k_i, NT_DIMS,
                                              preferred_element_type=F32))
        scores = jnp.concatenate(score_rows, axis=0)
        scores = jnp.where(tril_mask, scores, 0.0).astype(BF16)
        vb_n = vb[n * c:(n + 1) * c]
        o_intra.append(jnp.dot(scores, vb_n, preferred_element_type=F32))
        kv.append(lax.dot_general(vb_n, k_in[n].reshape(c, A_DK), TN_DIMS,
                                  preferred_element_type=F32))

    states = []
    for n in range(nch):
        states.append(st.astype(BF16))
        st = st * decay[n, 0] + kv[n]
    outs = [o_intra[n] + lax.dot_general(q_in[n].reshape(c, A_DK), states[n], NT_DIMS,
                                         preferred_element_type=F32) for n in range(nch)]
    return jnp.concatenate(outs, axis=0), st


def _even_mixer_kernel(q_ref, f_ref, i_ref, ga_ref, u_ref, v_ref, gb_ref,
                       lbl_ref, onorm_ref, lnw_ref, lnb_ref, ws_ref, bias_ref,
                       oa_ref, ob_ref, st_ref, *, layer_j):
    tb = q_ref.shape[0]

    @pl.when(pl.program_id(2) == 0)
    def _():
        st_ref[...] = jnp.zeros_like(st_ref)

    logits = lbl_ref[...]
    e = jnp.exp(logits - jnp.max(logits, axis=0, keepdims=True))
    lb = jnp.sum(e[:layer_j + 1], axis=0, keepdims=True) / jnp.sum(e, axis=0, keepdims=True)

    o, st = _hgrn2_block(q_ref[...].astype(F32), f_ref[...].astype(F32), i_ref[...].astype(F32),
                         lb, st_ref[...])
    st_ref[...] = st
    oa_ref[...] = (_rms(o, onorm_ref[...]) * _silu(ga_ref[...].astype(F32))).astype(oa_ref.dtype)

    cb = B_CHUNK
    w_causal = jnp.where(lax.broadcasted_iota(jnp.int32, (cb, cb), 1)
                         <= lax.broadcasted_iota(jnp.int32, (cb, cb), 0),
                         ws_ref[...], 0.0).astype(BF16)
    vv = v_ref[...].astype(F32)
    mu = jnp.mean(vv, axis=-1, keepdims=True)
    xc = vv - mu
    vg = (xc * lax.rsqrt(jnp.mean(xc * xc, axis=-1, keepdims=True) + EPS) * lnw_ref[...]
          + lnb_ref[...]).astype(BF16)
    sv = jnp.concatenate([jnp.dot(w_causal, vg[n * cb:(n + 1) * cb], preferred_element_type=F32)
                          for n in range(tb // cb)], axis=0)
    bias = _column_tile(bias_ref[...])
    sv = (sv.reshape(tb // cb, cb, B_DIM) + bias).reshape(tb, B_DIM)
    ob_ref[...] = (u_ref[...].astype(F32) * sv * _silu(gb_ref[...].astype(F32))).astype(ob_ref.dtype)


def even_mixers(z, lb_logits, a_onorm, ln_w, ln_b, ws, bias, *, layer_j, tb):
    bsz, t, _ = z.shape
    h = A_HEADS
    n_rows = lb_logits.shape[0]
    lbl = lb_logits.reshape(n_rows, h, A_DK).transpose(1, 0, 2)

    def sec(k):
        return pl.BlockSpec((None, tb, LANES), lambda b, j, s, k=k: (b, s, k * h + j))

    per_head = lambda shape: pl.BlockSpec((None,) + shape, lambda b, j, s: (j, 0, 0))
    out_spec = pl.BlockSpec((None, tb, LANES), lambda b, j, s: (b, s, j))
    out_sds = jax.ShapeDtypeStruct((bsz, t, h * LANES), BF16)
    return pl.pallas_call(
        functools.partial(_even_mixer_kernel, layer_j=layer_j),
        grid=(bsz, h, t // tb),
        in_specs=[sec(k) for k in range(7)] + [
            per_head((n_rows, LANES)),
            pl.BlockSpec((1, LANES), lambda b, j, s: (0, 0)),
            per_head((1, LANES)),
            per_head((1, LANES)),
            per_head((B_CHUNK, B_CHUNK)),
            per_head((1, B_CHUNK)),
        ],
        out_specs=[out_spec, out_spec],
        out_shape=[out_sds, out_sds],
        scratch_shapes=[pltpu.VMEM((A_DK, A_DK), F32)],
        compiler_params=_params("parallel", "parallel", "arbitrary"),
        name="even_mixers",
    )(z, z, z, z, z, z, z, lbl, a_onorm.reshape(1, LANES),
      ln_w.reshape(B_GROUPS, 1, B_DIM), ln_b.reshape(B_GROUPS, 1, B_DIM),
      ws, bias.reshape(B_GROUPS, 1, B_CHUNK))


def _out_proj_kernel(*refs, n_in, with_next):
    a_refs = refs[:n_in]
    w_refs = refs[n_in:2 * n_in]
    x_ref, post_ref = refs[2 * n_in:2 * n_in + 2]
    rest = refs[2 * n_in + 2:]
    if with_next:
        pre_ref, xo_ref, ho_ref = rest
    else:
        (xo_ref,) = rest
    tm = x_ref.shape[0]
    n_sub = 2 if tm % 32 == 0 else 1
    sub = tm // n_sub

    def project(k):
        rows = slice(k * sub, (k + 1) * sub)
        y = jnp.dot(a_refs[0][rows, :], w_refs[0][...], preferred_element_type=F32)
        for a_ref, w_ref in zip(a_refs[1:], w_refs[1:]):
            y = y + jnp.dot(a_ref[rows, :], w_ref[...], preferred_element_type=F32)
        return y

    def finish(k, y):
        rows = slice(k * sub, (k + 1) * sub)
        x_new = x_ref[rows, :] + _rms(y, post_ref[...])
        xo_ref[rows, :] = x_new
        if with_next:
            ho_ref[rows, :] = _rms(x_new, pre_ref[...]).astype(BF16)

    ys = [project(0)]
    for k in range(n_sub):
        if k + 1 < n_sub:
            ys.append(project(k + 1))
        finish(k, ys[k])


def out_proj(acts, w, x, post_w, next_pre_w, *, tm):
    m, d = x.shape
    n_in = len(acts)
    ka = acts[0].shape[1]
    assert all(a.shape[1] == ka for a in acts) and w.shape == (n_in * ka, d)
    with_next = next_pre_w is not None
    row = lambda width: pl.BlockSpec((tm, width), lambda i: (i, 0))
    const = lambda shape: pl.BlockSpec(shape, lambda i: (0, 0))
    in_specs = ([row(ka) for _ in acts]
                + [pl.BlockSpec((ka, d), lambda i, r=r: (r, 0)) for r in range(n_in)]
                + [row(d), const((1, d))])
    args = list(acts) + [w] * n_in + [x, post_w.reshape(1, d)]
    out_specs = [row(d)]
    out_shape = [jax.ShapeDtypeStruct((m, d), F32)]
    if with_next:
        in_specs.append(const((1, d)))
        args.append(next_pre_w.reshape(1, d))
        out_specs.append(row(d))
        out_shape.append(jax.ShapeDtypeStruct((m, d), BF16))
    return pl.pallas_call(
        functools.partial(_out_proj_kernel, n_in=n_in, with_next=with_next),
        grid=(m // tm,),
        in_specs=in_specs,
        out_specs=out_specs,
        out_shape=out_shape,
        compiler_params=_params("parallel"),
        name="out_proj",
    )(*args)


def _lane_iota():
    return lax.broadcasted_iota(jnp.int32, (1, LANES), 1)


def _join_halves(left, right):
    half = LANES // 2
    return pltpu.roll(jnp.where(_lane_iota() < half, right, left), half, axis=1)


def _odd_in_proj_kernel(h_ref, wt_ref, c_ref, pe_ref, gate_ref):
    h = h_ref[...]
    n_c = c_ref.shape[1]
    n_pe = n_c + C_ROPE
    nt = lambda rows: lax.dot_general(h, wt_ref[rows, :], NT_DIMS, preferred_element_type=F32)
    c_ref[...] = nt(slice(0, n_c)).astype(c_ref.dtype)
    pe_ref[:, :C_ROPE] = nt(slice(n_c, n_pe)).astype(pe_ref.dtype)
    pe_ref[:, C_ROPE:] = jnp.zeros((pe_ref.shape[0], LANES - C_ROPE), pe_ref.dtype)
    gate_ref[...] = nt(slice(n_pe, wt_ref.shape[0])).astype(gate_ref.dtype)


def odd_in_proj(h, wt, *, tm):
    m, k = h.shape
    widths = [2 * C_RANK, LANES, C_HEADS * C_V]
    w = wt
    return pl.pallas_call(
        _odd_in_proj_kernel,
        grid=(m // tm,),
        in_specs=[pl.BlockSpec((tm, k), lambda i: (i, 0)),
                  pl.BlockSpec(w.shape, lambda i: (0, 0))],
        out_specs=[pl.BlockSpec((tm, wd), lambda i: (i, 0)) for wd in widths],
        out_shape=[jax.ShapeDtypeStruct((m, wd), BF16) for wd in widths],
        compiler_params=_params("parallel"),
        name="odd_in_proj",
    )(h, w)


def _mla_expand_kernel(c_ref, pe_ref, pos_ref, freq_ref, qnorm_ref, kvnorm_ref, wq_ref, wkv_ref,
                       qn_ref, qpe_ref, kn_ref, vt_ref, kpe_ref, wk_ref, wvt_ref, *, scale):
    @pl.when(pl.program_id(0) == 0)
    def _():
        for h in range(C_HEADS):
            lo = h * (C_NOPE + C_V)
            wk_ref[:, h * C_NOPE:(h + 1) * C_NOPE] = wkv_ref[:, lo:lo + C_NOPE].astype(wk_ref.dtype)
            wvt_ref[h * C_V:(h + 1) * C_V, :] = wkv_ref[:, lo + C_NOPE:lo + C_NOPE + C_V].T.astype(
                wvt_ref.dtype)

    lane = _lane_iota()
    first_half = lane < C_ROPE
    low_rot = (lane % C_ROPE) < C_ROPE // 2
    tm = c_ref.shape[0]
    n_sub = 2 if tm % (2 * LANES) == 0 else 1
    sub = tm // n_sub

    def project(k):
        rows = slice(k * sub, (k + 1) * sub)
        c = c_ref[rows, :].astype(F32)
        cq = _rms(c[:, :C_RANK], qnorm_ref[...]).astype(BF16)
        ckv = _rms(c[:, C_RANK:], kvnorm_ref[...]).astype(BF16)
        q = jnp.dot(cq, wq_ref[...], preferred_element_type=F32)
        kn_ref[rows, :] = jnp.dot(ckv, wk_ref[...], preferred_element_type=F32).astype(kn_ref.dtype)
        vt_ref[:, rows] = lax.dot_general(wvt_ref[...], ckv, NT_DIMS,
                                          preferred_element_type=F32).astype(vt_ref.dtype)
        return q

    def finish(k, q):
        rows = slice(k * sub, (k + 1) * sub)
        groups = slice(k * sub // LANES, (k + 1) * sub // LANES)
        pos = pos_ref[groups, :].astype(F32)
        pos = jnp.concatenate([_column_tile(pos[g:g + 1]) for g in range(pos.shape[0])], axis=0)
        ang = pos * freq_ref[...]
        cos = jnp.cos(ang)
        sin = jnp.sin(ang)

        def rope(x):
            rot = jnp.where(low_rot, -pltpu.roll(x, LANES - C_ROPE // 2, axis=1),
                            pltpu.roll(x, C_ROPE // 2, axis=1))
            return x * cos + rot * sin

        for p in range(C_HEADS // 2):
            a, b, c3 = (q[:, (3 * p + j) * LANES:(3 * p + j + 1) * LANES] for j in range(3))
            qn_ref[rows, 2 * p * LANES:(2 * p + 1) * LANES] = (a * scale).astype(qn_ref.dtype)
            qn_ref[rows, (2 * p + 1) * LANES:(2 * p + 2) * LANES] = (
                _join_halves(b, c3) * scale).astype(qn_ref.dtype)
            pe_pair = jnp.where(first_half, b, c3)
            qpe_ref[rows, p * LANES:(p + 1) * LANES] = (rope(pe_pair) * scale).astype(qpe_ref.dtype)

        k_pe = rope(pe_ref[rows, :].astype(F32))
        kpe_ref[rows, :] = jnp.where(first_half, k_pe,
                                     pltpu.roll(k_pe, C_ROPE, axis=1)).astype(kpe_ref.dtype)

    qs = [project(0)]
    for k in range(n_sub):
        if k + 1 < n_sub:
            qs.append(project(k + 1))
        finish(k, qs[k])


def mla_expand(c, pe, pos, freq, q_norm, kv_norm, wq, wkv, *, bsz, tm, scale):
    m = c.shape[0]
    t = m // bsz
    nb = t // tm
    row = lambda width: pl.BlockSpec((tm, width), lambda i: (i, 0))
    const = lambda shape: pl.BlockSpec(shape, lambda i: (0, 0))
    n_nope = C_HEADS * C_NOPE
    n_v = C_HEADS * C_V
    widths = [n_nope, C_HEADS * C_ROPE, n_nope, None, LANES]
    out_specs = [row(wd) if wd else pl.BlockSpec((None, n_v, tm), lambda i: (i // nb, 0, i % nb))
                 for wd in widths]
    out_shape = [jax.ShapeDtypeStruct((m, wd) if wd else (bsz, n_v, t), BF16) for wd in widths]
    return pl.pallas_call(
        functools.partial(_mla_expand_kernel, scale=scale),
        grid=(m // tm,),
        in_specs=[row(c.shape[1]), row(pe.shape[1]),
                  pl.BlockSpec((None, tm // LANES, LANES), lambda i: (i, 0, 0)), const((1, LANES)),
                  const((1, C_RANK)), const((1, C_RANK)), const(wq.shape), const(wkv.shape)],
        out_specs=out_specs,
        out_shape=out_shape,
        scratch_shapes=[pltpu.VMEM((C_RANK, n_nope), BF16), pltpu.VMEM((n_v, C_RANK), BF16)],
        compiler_params=_params("arbitrary"),
        name="mla_expand",
    )(c, pe, pos, freq, q_norm.reshape(1, C_RANK), kv_norm.reshape(1, C_RANK), wq, wkv)


def _attn_kernel(qn_ref, qpe_ref, kn_ref, kpe_ref, vt_ref, g_ref, o_ref, k_scr, v_scr, *, tq):
    t = qn_ref.shape[0]
    heads = tuple(range(qn_ref.shape[1] // LANES))
    for hh in heads:
        k_scr[hh, :, :LANES] = kn_ref[:, hh * LANES:(hh + 1) * LANES]
        k_scr[hh, :, LANES:] = kpe_ref[...]
        v_scr[hh, :C_V] = vt_ref[hh * C_V:(hh + 1) * C_V]
        v_scr[hh, C_V:] = jnp.ones((v_scr.shape[1] - C_V, t), v_scr.dtype)

    lane = lax.broadcasted_iota(jnp.int32, (1, LANES), 1)
    causal = (lax.broadcasted_iota(jnp.int32, (tq, tq), 0)
              <= lax.broadcasted_iota(jnp.int32, (tq, tq), 1))
    neg = jnp.finfo(BF16).min

    def scores(qi, hh):
        r0 = qi * tq
        end = r0 + tq
        q_pe = qpe_ref[r0:end, (hh // 2) * LANES:(hh // 2 + 1) * LANES]
        q_pe = jnp.where((lane // C_ROPE) == hh % 2, q_pe, jnp.zeros_like(q_pe))
        q = jnp.concatenate([qn_ref[r0:end, hh * LANES:(hh + 1) * LANES], q_pe], axis=1)
        return lax.dot_general(k_scr[hh, :end, :], q, NT_DIMS,
                               preferred_element_type=F32).astype(BF16)

    def col_max(qi, s):
        r0 = qi * tq
        m = jnp.max(jnp.where(causal, s[r0:], neg), axis=0, keepdims=True)
        if qi > 0:
            m = jnp.maximum(m, jnp.max(s[:r0], axis=0, keepdims=True))
        return m

    def probs(qi, s, m):
        r0 = qi * tq
        p_d = jnp.where(causal, jnp.exp2(s[r0:] - m), jnp.zeros((), BF16))
        p_o = jnp.exp2(s[:r0] - m) if qi > 0 else None
        return p_d, p_o

    def values(qi, hh, p_d, p_o):
        r0 = qi * tq
        end = r0 + tq
        vt = v_scr.at[hh]
        acc = jnp.dot(vt[:, r0:end], p_d, preferred_element_type=F32)
        if qi > 0:
            acc = acc + jnp.dot(vt[:, :r0], p_o, preferred_element_type=F32)
        o = (acc[:C_V] * (1.0 / acc[C_V:C_V + 1])).T
        cols = slice(hh * C_V, (hh + 1) * C_V)
        o_ref[r0:end, cols] = (o * _silu(g_ref[r0:end, cols].astype(F32))).astype(o_ref.dtype)

    nq = t // tq
    s = [scores(0, hh) for hh in heads]
    sm = [col_max(0, s[hh]) for hh in heads]
    for qi in range(nq):
        last = qi + 1 == nq
        for hh in heads:
            s_next = None if last else scores(qi + 1, hh)
            values(qi, hh, *probs(qi, s[hh], sm[hh]))
            if not last:
                s[hh] = s_next
                sm[hh] = col_max(qi + 1, s_next)


def attention(qn, qpe, kn, kpe, vt, gate, *, tq):
    bsz, t, _ = qn.shape
    g = ATTN_HEADS_PER_STEP
    group = lambda width: pl.BlockSpec((None, t, width), lambda b, p: (b, 0, p))
    shared = pl.BlockSpec((None, t, LANES), lambda b, p: (b, 0, 0))
    group_t = pl.BlockSpec((None, g * C_V, t), lambda b, p: (b, p, 0))
    return pl.pallas_call(
        functools.partial(_attn_kernel, tq=tq),
        grid=(bsz, C_HEADS // g),
        in_specs=[group(g * LANES), group(g * C_ROPE), group(g * LANES), shared, group_t,
                  group(g * C_V)],
        out_specs=group(g * C_V),
        out_shape=jax.ShapeDtypeStruct((bsz, t, C_HEADS * C_V), BF16),
        scratch_shapes=[pltpu.VMEM((g, t, 2 * LANES), BF16),
                        pltpu.VMEM((g, C_V + BF16_SUBLANES, t), BF16)],
        compiler_params=_params("parallel", "parallel"),
        name="mla_attention",
    )(qn, qpe, kn, kpe, vt, gate)


def kernel(x, positions, norm_pre, norm_post, ev_w_in, ev_lb_logits, ev_a_onorm, ev_b_ln_w,
           ev_b_ln_b, ev_b_ws, ev_b_bias, ev_w_out, od_w_in, od_q_norm, od_w_qb, od_kv_norm,
           od_w_kvb, od_w_out):
    bsz, t, d = x.shape
    m = bsz * t
    xf = x.reshape(m, d)
    tm = min(ROW_TILE, m)

    z = rms_matmul(xf, norm_pre[0].reshape(1, d), ev_w_in[0], tm=min(IN_PROJ_TM, m),
                   tn=IN_PROJ_TN)
    mix_a, mix_b = even_mixers(z.reshape(bsz, t, -1), ev_lb_logits, ev_a_onorm[0], ev_b_ln_w[0],
                               ev_b_ln_b[0], ev_b_ws[0], ev_b_bias[0], layer_j=0,
                               tb=min(MIXER_BLOCK, t))
    x1, h1 = out_proj([mix_a.reshape(m, -1), mix_b.reshape(m, -1)], ev_w_out[0],
                      xf, norm_post[0], norm_pre[1], tm=tm)

    c, pe, gate = odd_in_proj(h1, od_w_in[0].T, tm=tm)
    inv_freq = ROPE_THETA ** (-jnp.arange(0, C_ROPE, 2, dtype=F32) / C_ROPE)
    freq = jnp.tile(inv_freq, LANES // (C_ROPE // 2)).reshape(1, LANES)
    scale = float((C_NOPE + C_ROPE) ** -0.5 * LOG2E)
    tm_e = min(tm, t)
    pos = positions.reshape(m // tm_e, tm_e // LANES, LANES)
    qn, qpe, kn, vt, kpe = mla_expand(c, pe, pos, freq, od_q_norm[0], od_kv_norm[0], od_w_qb[0],
                                      od_w_kvb[0], bsz=bsz, tm=tm_e, scale=scale)
    r3 = lambda a: a.reshape(bsz, t, -1)
    o = attention(r3(qn), r3(qpe), r3(kn), r3(kpe), vt, r3(gate), tq=min(ATTN_TQ, t))
    (x2,) = out_proj([o.reshape(m, -1)], od_w_out[0], x1, norm_post[1], None, tm=tm)
    return x2.reshape(bsz, t, d)
```

```python
import functools

import jax
import jax.numpy as jnp
from jax import lax
from jax.experimental import pallas as pl
from jax.experimental.pallas import tpu as pltpu

F32 = jnp.float32
BF16 = jnp.bfloat16

EPS = 1e-6
LANES = 128
BF16_SUBLANES = 16
VMEM_LIMIT_BYTES = 56 * 1024 * 1024
VMEM_LIMIT_BYTES_IN_PROJ = 62 * 1024 * 1024

A_HEADS = 8
A_DK = 128
A_CHUNK = 64
A_SUB = 16
B_GROUPS = 8
B_DIM = 128
B_CHUNK = 128
C_HEADS = 16
C_RANK = 512
C_NOPE = 128
C_ROPE = 64
C_V = 128
ROPE_THETA = 10000.0
LOG2E = 1.4426950408889634
ATTN_HEADS_PER_STEP = 4

IN_PROJ_TM, IN_PROJ_TN = 1024, 1792
ROW_TILE = 512
MIXER_BLOCK = 2048
ATTN_TQ = 256

NT_DIMS = (((1,), (1,)), ((), ()))
TN_DIMS = (((0,), (0,)), ((), ()))


def _params(*sem, vmem_limit=VMEM_LIMIT_BYTES):
    return pltpu.CompilerParams(dimension_semantics=sem, vmem_limit_bytes=vmem_limit)


def _rms(x, w):
    ms = jnp.mean(x * x, axis=-1, keepdims=True)
    return x * lax.rsqrt(ms + EPS) * w


def _silu(x):
    return x * (1.0 / (1.0 + jnp.exp(-x)))


def _column_tile(row):
    return jnp.broadcast_to(row, (LANES, LANES)).T


def _rms_matmul_kernel(x_ref, g_ref, w_ref, o_ref, h_ref):
    @pl.when(pl.program_id(1) == 0)
    def _():
        h_ref[...] = _rms(x_ref[...], g_ref[...]).astype(BF16)

    o_ref[...] = jnp.dot(h_ref[...], w_ref[...], preferred_element_type=F32).astype(o_ref.dtype)


def rms_matmul(x, g, w, *, tm, tn):
    m, k = x.shape
    n = w.shape[1]
    return pl.pallas_call(
        _rms_matmul_kernel,
        grid=(m // tm, n // tn),
        in_specs=[
            pl.BlockSpec((tm, k), lambda i, j: (i, 0)),
            pl.BlockSpec((1, k), lambda i, j: (0, 0)),
            pl.BlockSpec((k, tn), lambda i, j: (0, j)),
        ],
        out_specs=pl.BlockSpec((tm, tn), lambda i, j: (i, j)),
        out_shape=jax.ShapeDtypeStruct((m, n), BF16),
        scratch_shapes=[pltpu.VMEM((tm, k), BF16)],
        compiler_params=_params("parallel", "arbitrary", vmem_limit=VMEM_LIMIT_BYTES_IN_PROJ),
        name="rms_matmul",
    )(x, g, w)


def _hgrn2_block(q, zf, v, lb, st):
    c, s = A_CHUNK, A_SUB
    m = c // s
    tb = q.shape[0]
    nch = tb // c
    tril_mask = (lax.broadcasted_iota(jnp.int32, (c, c), 1)
                 <= lax.broadcasted_iota(jnp.int32, (c, c), 0))
    tril = jnp.where(tril_mask, 1.0, 0.0).astype(BF16)

    t = jnp.exp(-jnp.abs(zf))
    u = (1.0 - lb) * (1.0 / (1.0 + t))
    w = t * u
    pos = zf >= 0
    logf = jnp.log2(lb + jnp.where(pos, u, w))
    kk = jnp.where(pos, w, u)

    hi = logf.astype(BF16)
    r1 = logf - hi.astype(F32)
    mid = r1.astype(BF16)
    lo = (r1 - mid.astype(F32)).astype(BF16)
    terms = jnp.concatenate([hi, mid, lo], axis=1)
    b_chunks = []
    for n in range(nch):
        y = jnp.dot(tril, terms[n * c:(n + 1) * c], preferred_element_type=F32)
        b_chunks.append((y[:, :A_DK] + y[:, A_DK:2 * A_DK]) + y[:, 2 * A_DK:])
    b = jnp.concatenate(b_chunks, axis=0)

    shape4 = (nch, m, s, A_DK)
    b4 = b.reshape(shape4)
    g4 = (b4 - logf.reshape(shape4))[:, :, 0:1, :]
    e4 = b4[:, :, s - 1:s, :]
    bl4 = e4[:, m - 1:m]
    lc4 = b4 - g4
    kk4 = kk.reshape(shape4)
    q_sub = (q.reshape(shape4) * jnp.exp2(lc4)).astype(BF16)
    k_diag = (kk4 * jnp.exp2(-lc4)).astype(BF16)
    k_end = (kk4 * jnp.exp2(e4 - b4)).astype(BF16)
    q_in = q_sub * jnp.exp2(g4).astype(BF16)
    k_in = k_end * jnp.exp2(bl4 - e4).astype(BF16)
    decay = jnp.exp2(bl4)
    k_cross = {(i, j): k_end[:, j] * jnp.exp2(g4[:, i] - e4[:, j]).astype(BF16)
               for i in range(m) for j in range(i - 1)}
    vb = v.astype(BF16)
    zeros = jnp.zeros((s, A_DK), BF16)

    o_intra, kv = [], []
    for n in range(nch):
        score_rows = []
        for i in range(m):
            parts = [k_end[n, j] if j == i - 1 else k_cross[(i, j)][n] for j in range(i)]
            parts.append(k_diag[n, i])
            parts.extend([zeros] * (m - 1 - i))
            k_i = jnp.concatenate(parts, axis=0)
            score_rows.append(lax.dot_general(q_sub[n, i], k_i, NT_DIMS,
                                              preferred_element_type=F32))
        scores = jnp.concatenate(score_rows, axis=0)
        scores = jnp.where(tril_mask, scores, 0.0).astype(BF16)
        vb_n = vb[n * c:(n + 1) * c]
        o_intra.append(jnp.dot(scores, vb_n, preferred_element_type=F32))
        kv.append(lax.dot_general(vb_n, k_in[n].reshape(c, A_DK), TN_DIMS,
                                  preferred_element_type=F32))

    states = []
    for n in range(nch):
        states.append(st.astype(BF16))
        st = st * decay[n, 0] + kv[n]
    outs = [o_intra[n] + lax.dot_general(q_in[n].reshape(c, A_DK), states[n], NT_DIMS,
                                         preferred_element_type=F32) for n in range(nch)]
    return jnp.concatenate(outs, axis=0), st


def _even_mixer_kernel(q_ref, f_ref, i_ref, ga_ref, u_ref, v_ref, gb_ref,
                       lbl_ref, onorm_ref, lnw_ref, lnb_ref, ws_ref, bias_ref,
                       oa_ref, ob_ref, st_ref, *, layer_j):
    tb = q_ref.shape[0]

    @pl.when(pl.program_id(2) == 0)
    def _():
        st_ref[...] = jnp.zeros_like(st_ref)

    logits = lbl_ref[...]
    e = jnp.exp(logits - jnp.max(logits, axis=0, keepdims=True))
    lb = jnp.sum(e[:layer_j + 1], axis=0, keepdims=True) / jnp.sum(e, axis=0, keepdims=True)

    o, st = _hgrn2_block(q_ref[...].astype(F32), f_ref[...].astype(F32), i_ref[...].astype(F32),
                         lb, st_ref[...])
    st_ref[...] = st
    oa_ref[...] = (_rms(o, onorm_ref[...]) * _silu(ga_ref[...].astype(F32))).astype(oa_ref.dtype)

    cb = B_CHUNK
    w_causal = jnp.where(lax.broadcasted_iota(jnp.int32, (cb, cb), 1)
                         <= lax.broadcasted_iota(jnp.int32, (cb, cb), 0),
                         ws_ref[...], 0.0).astype(BF16)
    vv = v_ref[...].astype(F32)
    mu = jnp.mean(vv, axis=-1, keepdims=True)
    xc = vv - mu
    vg = (xc * lax.rsqrt(jnp.mean(xc * xc, axis=-1, keepdims=True) + EPS) * lnw_ref[...]
          + lnb_ref[...]).astype(BF16)
    sv = jnp.concatenate([jnp.dot(w_causal, vg[n * cb:(n + 1) * cb], preferred_element_type=F32)
                          for n in range(tb // cb)], axis=0)
    bias = _column_tile(bias_ref[...])
    sv = (sv.reshape(tb // cb, cb, B_DIM) + bias).reshape(tb, B_DIM)
    ob_ref[...] = (u_ref[...].astype(F32) * sv * _silu(gb_ref[...].astype(F32))).astype(ob_ref.dtype)


def even_mixers(z, lb_logits, a_onorm, ln_w, ln_b, ws, bias, *, layer_j, tb):
    bsz, t, _ = z.shape
    h = A_HEADS
    n_rows = lb_logits.shape[0]
    lbl = lb_logits.reshape(n_rows, h, A_DK).transpose(1, 0, 2)

    def sec(k):
        return pl.BlockSpec((None, tb, LANES), lambda b, j, s, k=k: (b, s, k * h + j))

    per_head = lambda shape: pl.BlockSpec((None,) + shape, lambda b, j, s: (j, 0, 0))
    out_spec = pl.BlockSpec((None, tb, LANES), lambda b, j, s: (b, s, j))
    out_sds = jax.ShapeDtypeStruct((bsz, t, h * LANES), BF16)
    return pl.pallas_call(
        functools.partial(_even_mixer_kernel, layer_j=layer_j),
        grid=(bsz, h, t // tb),
        in_specs=[sec(k) for k in range(7)] + [
            per_head((n_rows, LANES)),
            pl.BlockSpec((1, LANES), lambda b, j, s: (0, 0)),
            per_head((1, LANES)),
            per_head((1, LANES)),
            per_head((B_CHUNK, B_CHUNK)),
            per_head((1, B_CHUNK)),
        ],
        out_specs=[out_spec, out_spec],
        out_shape=[out_sds, out_sds],
        scratch_shapes=[pltpu.VMEM((A_DK, A_DK), F32)],
        compiler_params=_params("parallel", "parallel", "arbitrary"),
        name="even_mixers",
    )(z, z, z, z, z, z, z, lbl, a_onorm.reshape(1, LANES),
      ln_w.reshape(B_GROUPS, 1, B_DIM), ln_b.reshape(B_GROUPS, 1, B_DIM),
      ws, bias.reshape(B_GROUPS, 1, B_CHUNK))


def _out_proj_kernel(*refs, n_in, with_next):
    a_refs = refs[:n_in]
    w_refs = refs[n_in:2 * n_in]
    x_ref, post_ref = refs[2 * n_in:2 * n_in + 2]
    rest = refs[2 * n_in + 2:]
    if with_next:
        pre_ref, xo_ref, ho_ref = rest
    else:
        (xo_ref,) = rest
    tm = x_ref.shape[0]
    n_sub = 2 if tm % 32 == 0 else 1
    sub = tm // n_sub

    def project(k):
        rows = slice(k * sub, (k + 1) * sub)
        y = jnp.dot(a_refs[0][rows, :], w_refs[0][...], preferred_element_type=F32)
        for a_ref, w_ref in zip(a_refs[1:], w_refs[1:]):
            y = y + jnp.dot(a_ref[rows, :], w_ref[...], preferred_element_type=F32)
        return y

    def finish(k, y):
        rows = slice(k * sub, (k + 1) * sub)
        x_new = x_ref[rows, :] + _rms(y, post_ref[...])
        xo_ref[rows, :] = x_new
        if with_next:
            ho_ref[rows, :] = _rms(x_new, pre_ref[...]).astype(BF16)

    ys = [project(0)]
    for k in range(n_sub):
        if k + 1 < n_sub:
            ys.append(project(k + 1))
        finish(k, ys[k])


def out_proj(acts, w, x, post_w, next_pre_w, *, tm):
    m, d = x.shape
    n_in = len(acts)
    ka = acts[0].shape[1]
    assert all(a.shape[1] == ka for a in acts) and w.shape == (n_in * ka, d)
    with_next = next_pre_w is not None
    row = lambda width: pl.BlockSpec((tm, width), lambda i: (i, 0))
    const = lambda shape: pl.BlockSpec(shape, lambda i: (0, 0))
    in_specs = ([row(ka) for _ in acts]
                + [pl.BlockSpec((ka, d), lambda i, r=r: (r, 0)) for r in range(n_in)]
                + [row(d), const((1, d))])
    args = list(acts) + [w] * n_in + [x, post_w.reshape(1, d)]
    out_specs = [row(d)]
    out_shape = [jax.ShapeDtypeStruct((m, d), F32)]
    if with_next:
        in_specs.append(const((1, d)))
        args.append(next_pre_w.reshape(1, d))
        out_specs.append(row(d))
        out_shape.append(jax.ShapeDtypeStruct((m, d), BF16))
    return pl.pallas_call(
        functools.partial(_out_proj_kernel, n_in=n_in, with_next=with_next),
        grid=(m // tm,),
        in_specs=in_specs,
        out_specs=out_specs,
        out_shape=out_shape,
        compiler_params=_params("parallel"),
        name="out_proj",
    )(*args)


def _lane_iota():
    return lax.broadcasted_iota(jnp.int32, (1, LANES), 1)


def _join_halves(left, right):
    half = LANES // 2
    return pltpu.roll(jnp.where(_lane_iota() < half, right, left), half, axis=1)


def _odd_in_proj_kernel(h_ref, wt_ref, c_ref, pe_ref, gate_ref):
    h = h_ref[...]
    n_c = c_ref.shape[1]
    n_pe = n_c + C_ROPE
    nt = lambda rows: lax.dot_general(h, wt_ref[rows, :], NT_DIMS, preferred_element_type=F32)
    c_ref[...] = nt(slice(0, n_c)).astype(c_ref.dtype)
    pe_ref[:, :C_ROPE] = nt(slice(n_c, n_pe)).astype(pe_ref.dtype)
    pe_ref[:, C_ROPE:] = jnp.zeros((pe_ref.shape[0], LANES - C_ROPE), pe_ref.dtype)
    gate_ref[...] = nt(slice(n_pe, wt_ref.shape[0])).astype(gate_ref.dtype)


def odd_in_proj(h, wt, *, tm):
    m, k = h.shape
    widths = [2 * C_RANK, LANES, C_HEADS * C_V]
    w = wt
    return pl.pallas_call(
        _odd_in_proj_kernel,
        grid=(m // tm,),
        in_specs=[pl.BlockSpec((tm, k), lambda i: (i, 0)),
                  pl.BlockSpec(w.shape, lambda i: (0, 0))],
        out_specs=[pl.BlockSpec((tm, wd), lambda i: (i, 0)) for wd in widths],
        out_shape=[jax.ShapeDtypeStruct((m, wd), BF16) for wd in widths],
        compiler_params=_params("parallel"),
        name="odd_in_proj",
    )(h, w)


def _mla_expand_kernel(c_ref, pe_ref, pos_ref, freq_ref, qnorm_ref, kvnorm_ref, wq_ref, wkv_ref,
                       qn_ref, qpe_ref, kn_ref, vt_ref, kpe_ref, wk_ref, wvt_ref, *, scale):
    @pl.when(pl.program_id(0) == 0)
    def _():
        for h in range(C_HEADS):
            lo = h * (C_NOPE + C_V)
            wk_ref[:, h * C_NOPE:(h + 1) * C_NOPE] = wkv_ref[:, lo:lo + C_NOPE].astype(wk_ref.dtype)
            wvt_ref[h * C_V:(h + 1) * C_V, :] = wkv_ref[:, lo + C_NOPE:lo + C_NOPE + C_V].T.astype(
                wvt_ref.dtype)

    lane = _lane_iota()
    first_half = lane < C_ROPE
    low_rot = (lane % C_ROPE) < C_ROPE // 2
    tm = c_ref.shape[0]
    n_sub = 2 if tm % (2 * LANES) == 0 else 1
    sub = tm // n_sub

    def project(k):
        rows = slice(k * sub, (k + 1) * sub)
        c = c_ref[rows, :].astype(F32)
        cq = _rms(c[:, :C_RANK], qnorm_ref[...]).astype(BF16)
        ckv = _rms(c[:, C_RANK:], kvnorm_ref[...]).astype(BF16)
        q = jnp.dot(cq, wq_ref[...], preferred_element_type=F32)
        kn_ref[rows, :] = jnp.dot(ckv, wk_ref[...], preferred_element_type=F32).astype(kn_ref.dtype)
        vt_ref[:, rows] = lax.dot_general(wvt_ref[...], ckv, NT_DIMS,
                                          preferred_element_type=F32).astype(vt_ref.dtype)
        return q

    def finish(k, q):
        rows = slice(k * sub, (k + 1) * sub)
        groups = slice(k * sub // LANES, (k + 1) * sub // LANES)
        pos = pos_ref[groups, :].astype(F32)
        pos = jnp.concatenate([_column_tile(pos[g:g + 1]) for g in range(pos.shape[0])], axis=0)
        ang = pos * freq_ref[...]
        cos = jnp.cos(ang)
        sin = jnp.sin(ang)

        def rope(x):
            rot = jnp.where(low_rot, -pltpu.roll(x, LANES - C_ROPE // 2, axis=1),
                            pltpu.roll(x, C_ROPE // 2, axis=1))
            return x * cos + rot * sin

        for p in range(C_HEADS // 2):
            a, b, c3 = (q[:, (3 * p + j) * LANES:(3 * p + j + 1) * LANES] for j in range(3))
            qn_ref[rows, 2 * p * LANES:(2 * p + 1) * LANES] = (a * scale).astype(qn_ref.dtype)
            qn_ref[rows, (2 * p + 1) * LANES:(2 * p + 2) * LANES] = (
                _join_halves(b, c3) * scale).astype(qn_ref.dtype)
            pe_pair = jnp.where(first_half, b, c3)
            qpe_ref[rows, p * LANES:(p + 1) * LANES] = (rope(pe_pair) * scale).astype(qpe_ref.dtype)

        k_pe = rope(pe_ref[rows, :].astype(F32))
        kpe_ref[rows, :] = jnp.where(first_half, k_pe,
                                     pltpu.roll(k_pe, C_ROPE, axis=1)).astype(kpe_ref.dtype)

    qs = [project(0)]
    for k in range(n_sub):
        if k + 1 < n_sub:
            qs.append(project(k + 1))
        finish(k, qs[k])


def mla_expand(c, pe, pos, freq, q_norm, kv_norm, wq, wkv, *, bsz, tm, scale):
    m = c.shape[0]
    t = m // bsz
    nb = t // tm
    row = lambda width: pl.BlockSpec((tm, width), lambda i: (i, 0))
    const = lambda shape: pl.BlockSpec(shape, lambda i: (0, 0))
    n_nope = C_HEADS * C_NOPE
    n_v = C_HEADS * C_V
    widths = [n_nope, C_HEADS * C_ROPE, n_nope, None, LANES]
    out_specs = [row(wd) if wd else pl.BlockSpec((None, None, n_v, tm), lambda i: (i // nb, i % nb, 0, 0))
                 for wd in widths]
    out_shape = [jax.ShapeDtypeStruct((m, wd) if wd else (bsz, nb, n_v, tm), BF16) for wd in widths]
    return pl.pallas_call(
        functools.partial(_mla_expand_kernel, scale=scale),
        grid=(m // tm,),
        in_specs=[row(c.shape[1]), row(pe.shape[1]),
                  pl.BlockSpec((None, tm // LANES, LANES), lambda i: (i, 0, 0)), const((1, LANES)),
                  const((1, C_RANK)), const((1, C_RANK)), const(wq.shape), const(wkv.shape)],
        out_specs=out_specs,
        out_shape=out_shape,
        scratch_shapes=[pltpu.VMEM((C_RANK, n_nope), BF16), pltpu.VMEM((n_v, C_RANK), BF16)],
        compiler_params=_params("arbitrary"),
        name="mla_expand",
    )(c, pe, pos, freq, q_norm.reshape(1, C_RANK), kv_norm.reshape(1, C_RANK), wq, wkv)


def _attn_kernel(qn_ref, qpe_ref, kn_ref, kpe_ref, vt_ref, g_ref, o_ref, k_scr, v_scr, *, tq):
    t = qn_ref.shape[0]
    heads = tuple(range(qn_ref.shape[1] // LANES))
    for hh in heads:
        k_scr[hh, :, :LANES] = kn_ref[:, hh * LANES:(hh + 1) * LANES]
        k_scr[hh, :, LANES:] = kpe_ref[...]
        for j in range(vt_ref.shape[0]):
            tw = vt_ref.shape[2]
            v_scr[hh, :C_V, j * tw:(j + 1) * tw] = vt_ref[j, hh * C_V:(hh + 1) * C_V, :]
        v_scr[hh, C_V:] = jnp.ones((v_scr.shape[1] - C_V, t), v_scr.dtype)

    lane = lax.broadcasted_iota(jnp.int32, (1, LANES), 1)
    causal = (lax.broadcasted_iota(jnp.int32, (tq, tq), 0)
              <= lax.broadcasted_iota(jnp.int32, (tq, tq), 1))
    neg = jnp.finfo(BF16).min

    def scores(qi, hh):
        r0 = qi * tq
        end = r0 + tq
        q_pe = qpe_ref[r0:end, (hh // 2) * LANES:(hh // 2 + 1) * LANES]
        q_pe = jnp.where((lane // C_ROPE) == hh % 2, q_pe, jnp.zeros_like(q_pe))
        q = jnp.concatenate([qn_ref[r0:end, hh * LANES:(hh + 1) * LANES], q_pe], axis=1)
        return lax.dot_general(k_scr[hh, :end, :], q, NT_DIMS,
                               preferred_element_type=F32).astype(BF16)

    def col_max(qi, s):
        r0 = qi * tq
        m = jnp.max(jnp.where(causal, s[r0:], neg), axis=0, keepdims=True)
        if qi > 0:
            m = jnp.maximum(m, jnp.max(s[:r0], axis=0, keepdims=True))
        return m

    def probs(qi, s, m):
        r0 = qi * tq
        p_d = jnp.where(causal, jnp.exp2(s[r0:] - m), jnp.zeros((), BF16))
        p_o = jnp.exp2(s[:r0] - m) if qi > 0 else None
        return p_d, p_o

    def values(qi, hh, p_d, p_o):
        r0 = qi * tq
        end = r0 + tq
        vt = v_scr.at[hh]
        acc = jnp.dot(vt[:, r0:end], p_d, preferred_element_type=F32)
        if qi > 0:
            acc = acc + jnp.dot(vt[:, :r0], p_o, preferred_element_type=F32)
        o = (acc[:C_V] * (1.0 / acc[C_V:C_V + 1])).T
        cols = slice(hh * C_V, (hh + 1) * C_V)
        o_ref[r0:end, cols] = (o * _silu(g_ref[r0:end, cols].astype(F32))).astype(o_ref.dtype)

    nq = t // tq
    s = [scores(0, hh) for hh in heads]
    sm = [col_max(0, s[hh]) for hh in heads]
    for qi in range(nq):
        last = qi + 1 == nq
        for hh in heads:
            s_next = None if last else scores(qi + 1, hh)
            values(qi, hh, *probs(qi, s[hh], sm[hh]))
            if not last:
                s[hh] = s_next
                sm[hh] = col_max(qi + 1, s_next)


def attention(qn, qpe, kn, kpe, vt, gate, *, tq):
    bsz, t, _ = qn.shape
    g = ATTN_HEADS_PER_STEP
    group = lambda width: pl.BlockSpec((None, t, width), lambda b, p: (b, 0, p))
    shared = pl.BlockSpec((None, t, LANES), lambda b, p: (b, 0, 0))
    group_t = pl.BlockSpec((None, vt.shape[1], g * C_V, vt.shape[3]), lambda b, p: (b, 0, p, 0))
    return pl.pallas_call(
        functools.partial(_attn_kernel, tq=tq),
        grid=(bsz, C_HEADS // g),
        in_specs=[group(g * LANES), group(g * C_ROPE), group(g * LANES), shared, group_t,
                  group(g * C_V)],
        out_specs=group(g * C_V),
        out_shape=jax.ShapeDtypeStruct((bsz, t, C_HEADS * C_V), BF16),
        scratch_shapes=[pltpu.VMEM((g, t, 2 * LANES), BF16),
                        pltpu.VMEM((g, C_V + BF16_SUBLANES, t), BF16)],
        compiler_params=_params("parallel", "parallel"),
        name="mla_attention",
    )(qn, qpe, kn, kpe, vt, gate)


def kernel(x, positions, norm_pre, norm_post, ev_w_in, ev_lb_logits, ev_a_onorm, ev_b_ln_w,
           ev_b_ln_b, ev_b_ws, ev_b_bias, ev_w_out, od_w_in, od_q_norm, od_w_qb, od_kv_norm,
           od_w_kvb, od_w_out):
    bsz, t, d = x.shape
    m = bsz * t
    xf = x.reshape(m, d)
    tm = min(ROW_TILE, m)

    z = rms_matmul(xf, norm_pre[0].reshape(1, d), ev_w_in[0], tm=min(IN_PROJ_TM, m),
                   tn=IN_PROJ_TN)
    mix_a, mix_b = even_mixers(z.reshape(bsz, t, -1), ev_lb_logits, ev_a_onorm[0], ev_b_ln_w[0],
                               ev_b_ln_b[0], ev_b_ws[0], ev_b_bias[0], layer_j=0,
                               tb=min(MIXER_BLOCK, t))
    x1, h1 = out_proj([mix_a.reshape(m, -1), mix_b.reshape(m, -1)], ev_w_out[0],
                      xf, norm_post[0], norm_pre[1], tm=tm)

    c, pe, gate = odd_in_proj(h1, od_w_in[0].T, tm=tm)
    inv_freq = ROPE_THETA ** (-jnp.arange(0, C_ROPE, 2, dtype=F32) / C_ROPE)
    freq = jnp.tile(inv_freq, LANES // (C_ROPE // 2)).reshape(1, LANES)
    scale = float((C_NOPE + C_ROPE) ** -0.5 * LOG2E)
    tm_e = min(tm, t)
    pos = positions.reshape(m // tm_e, tm_e // LANES, LANES)
    qn, qpe, kn, vt, kpe = mla_expand(c, pe, pos, freq, od_q_norm[0], od_kv_norm[0], od_w_qb[0],
                                      od_w_kvb[0], bsz=bsz, tm=tm_e, scale=scale)
    r3 = lambda a: a.reshape(bsz, t, -1)
    o = attention(r3(qn), r3(qpe), r3(kn), r3(kpe), vt, r3(gate), tq=min(ATTN_TQ, t))
    (x2,) = out_proj([o.reshape(m, -1)], od_w_out[0], x1, norm_post[1], None, tm=tm)
    return x2.reshape(bsz, t, d)
```

```python
import functools

import jax
import jax.numpy as jnp
from jax import lax
from jax.experimental import pallas as pl
from jax.experimental.pallas import tpu as pltpu

F32 = jnp.float32
BF16 = jnp.bfloat16

EPS = 1e-6
LANES = 128
BF16_SUBLANES = 16
VMEM_LIMIT_BYTES = 56 * 1024 * 1024
VMEM_LIMIT_BYTES_IN_PROJ = 62 * 1024 * 1024

A_HEADS = 8
A_DK = 128
A_CHUNK = 64
A_SUB = 16
B_GROUPS = 8
B_DIM = 128
B_CHUNK = 128
C_HEADS = 16
C_RANK = 512
C_NOPE = 128
C_ROPE = 64
C_V = 128
ROPE_THETA = 10000.0
LOG2E = 1.4426950408889634
ATTN_HEADS_PER_STEP = 4

IN_PROJ_TM, IN_PROJ_TN = 1024, 1792
ROW_TILE = 512
MIXER_BLOCK = 2048
ATTN_TQ = 256

NT_DIMS = (((1,), (1,)), ((), ()))
TN_DIMS = (((0,), (0,)), ((), ()))


def _params(*sem, vmem_limit=VMEM_LIMIT_BYTES):
    return pltpu.CompilerParams(dimension_semantics=sem, vmem_limit_bytes=vmem_limit)


def _rms(x, w):
    ms = jnp.mean(x * x, axis=-1, keepdims=True)
    return x * lax.rsqrt(ms + EPS) * w


def _silu(x):
    return x * (1.0 / (1.0 + jnp.exp(-x)))


def _column_tile(row):
    return jnp.broadcast_to(row, (LANES, LANES)).T


def _rms_matmul_kernel(x_ref, g_ref, w_ref, o_ref, h_ref):
    @pl.when(pl.program_id(1) == 0)
    def _():
        h_ref[...] = _rms(x_ref[...], g_ref[...]).astype(BF16)

    o_ref[...] = jnp.dot(h_ref[...], w_ref[...], preferred_element_type=F32).astype(o_ref.dtype)


def rms_matmul(x, g, w, *, tm, tn):
    m, k = x.shape
    n = w.shape[1]
    return pl.pallas_call(
        _rms_matmul_kernel,
        grid=(m // tm, n // tn),
        in_specs=[
            pl.BlockSpec((tm, k), lambda i, j: (i, 0)),
            pl.BlockSpec((1, k), lambda i, j: (0, 0)),
            pl.BlockSpec((k, tn), lambda i, j: (0, j)),
        ],
        out_specs=pl.BlockSpec((tm, tn), lambda i, j: (i, j)),
        out_shape=jax.ShapeDtypeStruct((m, n), BF16),
        scratch_shapes=[pltpu.VMEM((tm, k), BF16)],
        compiler_params=_params("parallel", "arbitrary", vmem_limit=VMEM_LIMIT_BYTES_IN_PROJ),
        name="rms_matmul",
    )(x, g, w)


def _hgrn2_block(q, zf, v, lb, st):
    c, s = A_CHUNK, A_SUB
    m = c // s
    tb = q.shape[0]
    nch = tb // c
    tril_mask = (lax.broadcasted_iota(jnp.int32, (c, c), 1)
                 <= lax.broadcasted_iota(jnp.int32, (c, c), 0))
    tril = jnp.where(tril_mask, 1.0, 0.0).astype(BF16)

    t = jnp.exp(-jnp.abs(zf))
    u = (1.0 - lb) * (1.0 / (1.0 + t))
    w = t * u
    pos = zf >= 0
    logf = jnp.log2(lb + jnp.where(pos, u, w))
    kk = jnp.where(pos, w, u)

    hi = logf.astype(BF16)
    r1 = logf - hi.astype(F32)
    mid = r1.astype(BF16)
    lo = (r1 - mid.astype(F32)).astype(BF16)
    terms = jnp.concatenate([hi, mid, lo], axis=1)
    b_chunks = []
    for n in range(nch):
        y = jnp.dot(tril, terms[n * c:(n + 1) * c], preferred_element_type=F32)
        b_chunks.append((y[:, :A_DK] + y[:, A_DK:2 * A_DK]) + y[:, 2 * A_DK:])
    b = jnp.concatenate(b_chunks, axis=0)

    shape4 = (nch, m, s, A_DK)
    b4 = b.reshape(shape4)
    g4 = (b4 - logf.reshape(shape4))[:, :, 0:1, :]
    e4 = b4[:, :, s - 1:s, :]
    bl4 = e4[:, m - 1:m]
    lc4 = b4 - g4
    kk4 = kk.reshape(shape4)
    q_sub = (q.reshape(shape4) * jnp.exp2(lc4)).astype(BF16)
    k_diag = (kk4 * jnp.exp2(-lc4)).astype(BF16)
    k_end = (kk4 * jnp.exp2(e4 - b4)).astype(BF16)
    q_in = q_sub * jnp.exp2(g4).astype(BF16)
    k_in = k_end * jnp.exp2(bl4 - e4).astype(BF16)
    decay = jnp.exp2(bl4)
    k_cross = {(i, j): k_end[:, j] * jnp.exp2(g4[:, i] - e4[:, j]).astype(BF16)
               for i in range(m) for j in range(i - 1)}
    vb = v.astype(BF16)
    zeros = jnp.zeros((s, A_DK), BF16)

    o_intra, kv = [], []
    for n in range(nch):
        score_rows = []
        for i in range(m):
            parts = [k_end[n, j] if j == i - 1 else k_cross[(i, j)][n] for j in range(i)]
            parts.append(k_diag[n, i])
            parts.extend([zeros] * (m - 1 - i))
            k_i = jnp.concatenate(parts, axis=0)
            score_rows.append(lax.dot_general(q_sub[n, i], k_i, NT_DIMS,
                                              preferred_element_type=F32))
        scores = jnp.concatenate(score_rows, axis=0)
        scores = jnp.where(tril_mask, scores, 0.0).astype(BF16)
        vb_n = vb[n * c:(n + 1) * c]
        o_intra.append(jnp.dot(scores, vb_n, preferred_element_type=F32))
        kv.append(lax.dot_general(vb_n, k_in[n].reshape(c, A_DK), TN_DIMS,
                                  preferred_element_type=F32))

    states = []
    for n in range(nch):
        states.append(st.astype(BF16))
        st = st * decay[n, 0] + kv[n]
    outs = [o_intra[n] + lax.dot_general(q_in[n].reshape(c, A_DK), states[n], NT_DIMS,
                                         preferred_element_type=F32) for n in range(nch)]
    return jnp.concatenate(outs, axis=0), st


def _even_mixer_kernel(q_ref, f_ref, i_ref, ga_ref, u_ref, v_ref, gb_ref,
                       lbl_ref, onorm_ref, lnw_ref, lnb_ref, ws_ref, bias_ref,
                       oa_ref, ob_ref, st_ref, *, layer_j):
    tb = q_ref.shape[0]

    @pl.when(pl.program_id(2) == 0)
    def _():
        st_ref[...] = jnp.zeros_like(st_ref)

    logits = lbl_ref[...]
    e = jnp.exp(logits - jnp.max(logits, axis=0, keepdims=True))
    lb = jnp.sum(e[:layer_j + 1], axis=0, keepdims=True) / jnp.sum(e, axis=0, keepdims=True)

    o, st = _hgrn2_block(q_ref[...].astype(F32), f_ref[...].astype(F32), i_ref[...].astype(F32),
                         lb, st_ref[...])
    st_ref[...] = st
    oa_ref[...] = (_rms(o, onorm_ref[...]) * _silu(ga_ref[...].astype(F32))).astype(oa_ref.dtype)

    cb = B_CHUNK
    w_causal = jnp.where(lax.broadcasted_iota(jnp.int32, (cb, cb), 1)
                         <= lax.broadcasted_iota(jnp.int32, (cb, cb), 0),
                         ws_ref[...], 0.0).astype(BF16)
    vv = v_ref[...].astype(F32)
    mu = jnp.mean(vv, axis=-1, keepdims=True)
    xc = vv - mu
    vg = (xc * lax.rsqrt(jnp.mean(xc * xc, axis=-1, keepdims=True) + EPS) * lnw_ref[...]
          + lnb_ref[...]).astype(BF16)
    sv = jnp.concatenate([jnp.dot(w_causal, vg[n * cb:(n + 1) * cb], preferred_element_type=F32)
                          for n in range(tb // cb)], axis=0)
    bias = _column_tile(bias_ref[...])
    sv = (sv.reshape(tb // cb, cb, B_DIM) + bias).reshape(tb, B_DIM)
    ob_ref[...] = (u_ref[...].astype(F32) * sv * _silu(gb_ref[...].astype(F32))).astype(ob_ref.dtype)


def even_mixers(z, lb_logits, a_onorm, ln_w, ln_b, ws, bias, *, layer_j, tb):
    bsz, t, _ = z.shape
    h = A_HEADS
    n_rows = lb_logits.shape[0]
    lbl = lb_logits.reshape(n_rows, h, A_DK).transpose(1, 0, 2)

    def sec(k):
        return pl.BlockSpec((None, tb, LANES), lambda b, j, s, k=k: (b, s, k * h + j))

    per_head = lambda shape: pl.BlockSpec((None,) + shape, lambda b, j, s: (j, 0, 0))
    out_spec = pl.BlockSpec((None, tb, LANES), lambda b, j, s: (b, s, j))
    out_sds = jax.ShapeDtypeStruct((bsz, t, h * LANES), BF16)
    return pl.pallas_call(
        functools.partial(_even_mixer_kernel, layer_j=layer_j),
        grid=(bsz, h, t // tb),
        in_specs=[sec(k) for k in range(7)] + [
            per_head((n_rows, LANES)),
            pl.BlockSpec((1, LANES), lambda b, j, s: (0, 0)),
            per_head((1, LANES)),
            per_head((1, LANES)),
            per_head((B_CHUNK, B_CHUNK)),
            per_head((1, B_CHUNK)),
        ],
        out_specs=[out_spec, out_spec],
        out_shape=[out_sds, out_sds],
        scratch_shapes=[pltpu.VMEM((A_DK, A_DK), F32)],
        compiler_params=_params("parallel", "parallel", "arbitrary"),
        name="even_mixers",
    )(z, z, z, z, z, z, z, lbl, a_onorm.reshape(1, LANES),
      ln_w.reshape(B_GROUPS, 1, B_DIM), ln_b.reshape(B_GROUPS, 1, B_DIM),
      ws, bias.reshape(B_GROUPS, 1, B_CHUNK))


def _out_proj_kernel(*refs, n_in, with_next):
    a_refs = refs[:n_in]
    w_refs = refs[n_in:2 * n_in]
    x_ref, post_ref = refs[2 * n_in:2 * n_in + 2]
    rest = refs[2 * n_in + 2:]
    if with_next:
        pre_ref, xo_ref, ho_ref, wb_ref = rest
    else:
        xo_ref, wb_ref = rest
    tm = x_ref.shape[0]
    n_sub = 2 if tm % 32 == 0 else 1
    sub = tm // n_sub
    ka = a_refs[0].shape[1]

    @pl.when(pl.program_id(0) == 0)
    def _():
        for r, w_ref in enumerate(w_refs):
            wb_ref[r * ka:(r + 1) * ka, :] = w_ref[...].astype(wb_ref.dtype)

    def project(k):
        rows = slice(k * sub, (k + 1) * sub)
        y = jnp.dot(a_refs[0][rows, :], wb_ref[0:ka, :], preferred_element_type=F32)
        for r in range(1, n_in):
            y = y + jnp.dot(a_refs[r][rows, :], wb_ref[r * ka:(r + 1) * ka, :],
                            preferred_element_type=F32)
        return y

    def finish(k, y):
        rows = slice(k * sub, (k + 1) * sub)
        x_new = x_ref[rows, :] + _rms(y, post_ref[...])
        xo_ref[rows, :] = x_new
        if with_next:
            ho_ref[rows, :] = _rms(x_new, pre_ref[...]).astype(BF16)

    ys = [project(0)]
    for k in range(n_sub):
        if k + 1 < n_sub:
            ys.append(project(k + 1))
        finish(k, ys[k])


def out_proj(acts, w, x, post_w, next_pre_w, *, tm):
    m, d = x.shape
    n_in = len(acts)
    ka = acts[0].shape[1]
    assert all(a.shape[1] == ka for a in acts) and w.shape == (n_in * ka, d)
    with_next = next_pre_w is not None
    row = lambda width: pl.BlockSpec((tm, width), lambda i: (i, 0))
    const = lambda shape: pl.BlockSpec(shape, lambda i: (0, 0))
    in_specs = ([row(ka) for _ in acts]
                + [pl.BlockSpec((ka, d), lambda i, r=r: (r, 0)) for r in range(n_in)]
                + [row(d), const((1, d))])
    args = list(acts) + [w] * n_in + [x, post_w.reshape(1, d)]
    out_specs = [row(d)]
    out_shape = [jax.ShapeDtypeStruct((m, d), F32)]
    if with_next:
        in_specs.append(const((1, d)))
        args.append(next_pre_w.reshape(1, d))
        out_specs.append(row(d))
        out_shape.append(jax.ShapeDtypeStruct((m, d), BF16))
    return pl.pallas_call(
        functools.partial(_out_proj_kernel, n_in=n_in, with_next=with_next),
        grid=(m // tm,),
        in_specs=in_specs,
        out_specs=out_specs,
        out_shape=out_shape,
        scratch_shapes=[pltpu.VMEM(w.shape, BF16)],
        compiler_params=_params("arbitrary", vmem_limit=VMEM_LIMIT_BYTES_IN_PROJ),
        name="out_proj",
    )(*args)


def _lane_iota():
    return lax.broadcasted_iota(jnp.int32, (1, LANES), 1)


def _join_halves(left, right):
    half = LANES // 2
    return pltpu.roll(jnp.where(_lane_iota() < half, right, left), half, axis=1)


def _odd_in_proj_kernel(h_ref, wt_ref, c_ref, pe_ref, gate_ref):
    h = h_ref[...]
    n_c = c_ref.shape[1]
    n_pe = n_c + C_ROPE
    nt = lambda rows: lax.dot_general(h, wt_ref[rows, :], NT_DIMS, preferred_element_type=F32)
    c_ref[...] = nt(slice(0, n_c)).astype(c_ref.dtype)
    pe_ref[:, :C_ROPE] = nt(slice(n_c, n_pe)).astype(pe_ref.dtype)
    pe_ref[:, C_ROPE:] = jnp.zeros((pe_ref.shape[0], LANES - C_ROPE), pe_ref.dtype)
    gate_ref[...] = nt(slice(n_pe, wt_ref.shape[0])).astype(gate_ref.dtype)


def odd_in_proj(h, wt, *, tm):
    m, k = h.shape
    widths = [2 * C_RANK, LANES, C_HEADS * C_V]
    w = wt
    return pl.pallas_call(
        _odd_in_proj_kernel,
        grid=(m // tm,),
        in_specs=[pl.BlockSpec((tm, k), lambda i: (i, 0)),
                  pl.BlockSpec(w.shape, lambda i: (0, 0))],
        out_specs=[pl.BlockSpec((tm, wd), lambda i: (i, 0)) for wd in widths],
        out_shape=[jax.ShapeDtypeStruct((m, wd), BF16) for wd in widths],
        compiler_params=_params("parallel"),
        name="odd_in_proj",
    )(h, w)


def _mla_expand_kernel(c_ref, pe_ref, pos_ref, freq_ref, qnorm_ref, kvnorm_ref, wq_ref, wkv_ref,
                       qn_ref, qpe_ref, kn_ref, vt_ref, kpe_ref, wk_ref, wvt_ref, *, scale):
    @pl.when(pl.program_id(0) == 0)
    def _():
        for h in range(C_HEADS):
            lo = h * (C_NOPE + C_V)
            wk_ref[:, h * C_NOPE:(h + 1) * C_NOPE] = wkv_ref[:, lo:lo + C_NOPE].astype(wk_ref.dtype)
            wvt_ref[h * C_V:(h + 1) * C_V, :] = wkv_ref[:, lo + C_NOPE:lo + C_NOPE + C_V].T.astype(
                wvt_ref.dtype)

    lane = _lane_iota()
    first_half = lane < C_ROPE
    low_rot = (lane % C_ROPE) < C_ROPE // 2
    tm = c_ref.shape[0]
    n_sub = 2 if tm % (2 * LANES) == 0 else 1
    sub = tm // n_sub

    def project(k):
        rows = slice(k * sub, (k + 1) * sub)
        c = c_ref[rows, :].astype(F32)
        cq = _rms(c[:, :C_RANK], qnorm_ref[...]).astype(BF16)
        ckv = _rms(c[:, C_RANK:], kvnorm_ref[...]).astype(BF16)
        q = jnp.dot(cq, wq_ref[...], preferred_element_type=F32)
        kn_ref[rows, :] = jnp.dot(ckv, wk_ref[...], preferred_element_type=F32).astype(kn_ref.dtype)
        vt_ref[:, rows] = lax.dot_general(wvt_ref[...], ckv, NT_DIMS,
                                          preferred_element_type=F32).astype(vt_ref.dtype)
        return q

    def finish(k, q):
        rows = slice(k * sub, (k + 1) * sub)
        groups = slice(k * sub // LANES, (k + 1) * sub // LANES)
        pos = pos_ref[groups, :].astype(F32)
        pos = jnp.concatenate([_column_tile(pos[g:g + 1]) for g in range(pos.shape[0])], axis=0)
        ang = pos * freq_ref[...]
        cos = jnp.cos(ang)
        sin = jnp.sin(ang)

        def rope(x):
            rot = jnp.where(low_rot, -pltpu.roll(x, LANES - C_ROPE // 2, axis=1),
                            pltpu.roll(x, C_ROPE // 2, axis=1))
            return x * cos + rot * sin

        for p in range(C_HEADS // 2):
            a, b, c3 = (q[:, (3 * p + j) * LANES:(3 * p + j + 1) * LANES] for j in range(3))
            qn_ref[rows, 2 * p * LANES:(2 * p + 1) * LANES] = (a * scale).astype(qn_ref.dtype)
            qn_ref[rows, (2 * p + 1) * LANES:(2 * p + 2) * LANES] = (
                _join_halves(b, c3) * scale).astype(qn_ref.dtype)
            pe_pair = jnp.where(first_half, b, c3)
            qpe_ref[rows, p * LANES:(p + 1) * LANES] = (rope(pe_pair) * scale).astype(qpe_ref.dtype)

        k_pe = rope(pe_ref[rows, :].astype(F32))
        kpe_ref[rows, :] = jnp.where(first_half, k_pe,
                                     pltpu.roll(k_pe, C_ROPE, axis=1)).astype(kpe_ref.dtype)

    qs = [project(0)]
    for k in range(n_sub):
        if k + 1 < n_sub:
            qs.append(project(k + 1))
        finish(k, qs[k])


def mla_expand(c, pe, pos, freq, q_norm, kv_norm, wq, wkv, *, bsz, tm, scale):
    m = c.shape[0]
    t = m // bsz
    nb = t // tm
    row = lambda width: pl.BlockSpec((tm, width), lambda i: (i, 0))
    const = lambda shape: pl.BlockSpec(shape, lambda i: (0, 0))
    n_nope = C_HEADS * C_NOPE
    n_v = C_HEADS * C_V
    widths = [n_nope, C_HEADS * C_ROPE, n_nope, None, LANES]
    out_specs = [row(wd) if wd else pl.BlockSpec((None, n_v, tm), lambda i: (i // nb, 0, i % nb))
                 for wd in widths]
    out_shape = [jax.ShapeDtypeStruct((m, wd) if wd else (bsz, n_v, t), BF16) for wd in widths]
    return pl.pallas_call(
        functools.partial(_mla_expand_kernel, scale=scale),
        grid=(m // tm,),
        in_specs=[row(c.shape[1]), row(pe.shape[1]),
                  pl.BlockSpec((None, tm // LANES, LANES), lambda i: (i, 0, 0)), const((1, LANES)),
                  const((1, C_RANK)), const((1, C_RANK)), const(wq.shape), const(wkv.shape)],
        out_specs=out_specs,
        out_shape=out_shape,
        scratch_shapes=[pltpu.VMEM((C_RANK, n_nope), BF16), pltpu.VMEM((n_v, C_RANK), BF16)],
        compiler_params=_params("arbitrary"),
        name="mla_expand",
    )(c, pe, pos, freq, q_norm.reshape(1, C_RANK), kv_norm.reshape(1, C_RANK), wq, wkv)


def _attn_kernel(qn_ref, qpe_ref, kn_ref, kpe_ref, vt_ref, g_ref, o_ref, k_scr, v_scr, *, tq):
    t = qn_ref.shape[0]
    heads = tuple(range(qn_ref.shape[1] // LANES))
    for hh in heads:
        k_scr[hh, :, :LANES] = kn_ref[:, hh * LANES:(hh + 1) * LANES]
        k_scr[hh, :, LANES:] = kpe_ref[...]
        v_scr[hh, :C_V] = vt_ref[hh * C_V:(hh + 1) * C_V]
        v_scr[hh, C_V:] = jnp.ones((v_scr.shape[1] - C_V, t), v_scr.dtype)

    lane = lax.broadcasted_iota(jnp.int32, (1, LANES), 1)
    causal = (lax.broadcasted_iota(jnp.int32, (tq, tq), 0)
              <= lax.broadcasted_iota(jnp.int32, (tq, tq), 1))
    neg = jnp.finfo(BF16).min

    def scores(qi, hh):
        r0 = qi * tq
        end = r0 + tq
        q_pe = qpe_ref[r0:end, (hh // 2) * LANES:(hh // 2 + 1) * LANES]
        q_pe = jnp.where((lane // C_ROPE) == hh % 2, q_pe, jnp.zeros_like(q_pe))
        q = jnp.concatenate([qn_ref[r0:end, hh * LANES:(hh + 1) * LANES], q_pe], axis=1)
        return lax.dot_general(k_scr[hh, :end, :], q, NT_DIMS,
                               preferred_element_type=F32).astype(BF16)

    def col_max(qi, s):
        r0 = qi * tq
        m = jnp.max(jnp.where(causal, s[r0:], neg), axis=0, keepdims=True)
        if qi > 0:
            m = jnp.maximum(m, jnp.max(s[:r0], axis=0, keepdims=True))
        return m

    def probs(qi, s, m):
        r0 = qi * tq
        p_d = jnp.where(causal, jnp.exp2(s[r0:] - m), jnp.zeros((), BF16))
        p_o = jnp.exp2(s[:r0] - m) if qi > 0 else None
        return p_d, p_o

    def values(qi, hh, p_d, p_o):
        r0 = qi * tq
        end = r0 + tq
        vt = v_scr.at[hh]
        acc = jnp.dot(vt[:, r0:end], p_d, preferred_element_type=F32)
        if qi > 0:
            acc = acc + jnp.dot(vt[:, :r0], p_o, preferred_element_type=F32)
        o = (acc[:C_V] * (1.0 / acc[C_V:C_V + 1])).T
        cols = slice(hh * C_V, (hh + 1) * C_V)
        o_ref[r0:end, cols] = (o * _silu(g_ref[r0:end, cols].astype(F32))).astype(o_ref.dtype)

    nq = t // tq
    s = [scores(0, hh) for hh in heads]
    sm = [col_max(0, s[hh]) for hh in heads]
    for qi in range(nq):
        last = qi + 1 == nq
        for hh in heads:
            s_next = None if last else scores(qi + 1, hh)
            values(qi, hh, *probs(qi, s[hh], sm[hh]))
            if not last:
                s[hh] = s_next
                sm[hh] = col_max(qi + 1, s_next)


def attention(qn, qpe, kn, kpe, vt, gate, *, tq):
    bsz, t, _ = qn.shape
    g = ATTN_HEADS_PER_STEP
    group = lambda width: pl.BlockSpec((None, t, width), lambda b, p: (b, 0, p))
    shared = pl.BlockSpec((None, t, LANES), lambda b, p: (b, 0, 0))
    group_t = pl.BlockSpec((None, g * C_V, t), lambda b, p: (b, p, 0))
    return pl.pallas_call(
        functools.partial(_attn_kernel, tq=tq),
        grid=(bsz, C_HEADS // g),
        in_specs=[group(g * LANES), group(g * C_ROPE), group(g * LANES), shared, group_t,
                  group(g * C_V)],
        out_specs=group(g * C_V),
        out_shape=jax.ShapeDtypeStruct((bsz, t, C_HEADS * C_V), BF16),
        scratch_shapes=[pltpu.VMEM((g, t, 2 * LANES), BF16),
                        pltpu.VMEM((g, C_V + BF16_SUBLANES, t), BF16)],
        compiler_params=_params("parallel", "parallel"),
        name="mla_attention",
    )(qn, qpe, kn, kpe, vt, gate)


def kernel(x, positions, norm_pre, norm_post, ev_w_in, ev_lb_logits, ev_a_onorm, ev_b_ln_w,
           ev_b_ln_b, ev_b_ws, ev_b_bias, ev_w_out, od_w_in, od_q_norm, od_w_qb, od_kv_norm,
           od_w_kvb, od_w_out):
    bsz, t, d = x.shape
    m = bsz * t
    xf = x.reshape(m, d)
    tm = min(ROW_TILE, m)

    z = rms_matmul(xf, norm_pre[0].reshape(1, d), ev_w_in[0], tm=min(IN_PROJ_TM, m),
                   tn=IN_PROJ_TN)
    mix_a, mix_b = even_mixers(z.reshape(bsz, t, -1), ev_lb_logits, ev_a_onorm[0], ev_b_ln_w[0],
                               ev_b_ln_b[0], ev_b_ws[0], ev_b_bias[0], layer_j=0,
                               tb=min(MIXER_BLOCK, t))
    x1, h1 = out_proj([mix_a.reshape(m, -1), mix_b.reshape(m, -1)], ev_w_out[0],
                      xf, norm_post[0], norm_pre[1], tm=tm)

    c, pe, gate = odd_in_proj(h1, od_w_in[0].T, tm=tm)
    inv_freq = ROPE_THETA ** (-jnp.arange(0, C_ROPE, 2, dtype=F32) / C_ROPE)
    freq = jnp.tile(inv_freq, LANES // (C_ROPE // 2)).reshape(1, LANES)
    scale = float((C_NOPE + C_ROPE) ** -0.5 * LOG2E)
    tm_e = min(tm, t)
    pos = positions.reshape(m // tm_e, tm_e // LANES, LANES)
    qn, qpe, kn, vt, kpe = mla_expand(c, pe, pos, freq, od_q_norm[0], od_kv_norm[0], od_w_qb[0],
                                      od_w_kvb[0], bsz=bsz, tm=tm_e, scale=scale)
    r3 = lambda a: a.reshape(bsz, t, -1)
    o = attention(r3(qn), r3(qpe), r3(kn), r3(kpe), vt, r3(gate), tq=min(ATTN_TQ, t))
    (x2,) = out_proj([o.reshape(m, -1)], od_w_out[0], x1, norm_post[1], None, tm=tm)
    return x2.reshape(bsz, t, d)
```

```python
import functools

import jax
import jax.numpy as jnp
from jax import lax
from jax.experimental import pallas as pl
from jax.experimental.pallas import tpu as pltpu

F32 = jnp.float32
BF16 = jnp.bfloat16

EPS = 1e-6
LANES = 128
BF16_SUBLANES = 16
VMEM_LIMIT_BYTES = 56 * 1024 * 1024
VMEM_LIMIT_BYTES_IN_PROJ = 62 * 1024 * 1024

A_HEADS = 8
A_DK = 128
A_CHUNK = 64
A_SUB = 16
B_GROUPS = 8
B_DIM = 128
B_CHUNK = 128
C_HEADS = 16
C_RANK = 512
C_NOPE = 128
C_ROPE = 64
C_V = 128
ROPE_THETA = 10000.0
LOG2E = 1.4426950408889634
ATTN_HEADS_PER_STEP = 4

IN_PROJ_TM, IN_PROJ_TN = 1024, 1792
ROW_TILE = 512
MIXER_BLOCK = 2048
ATTN_TQ = 256

NT_DIMS = (((1,), (1,)), ((), ()))
TN_DIMS = (((0,), (0,)), ((), ()))


def _params(*sem, vmem_limit=VMEM_LIMIT_BYTES):
    return pltpu.CompilerParams(dimension_semantics=sem, vmem_limit_bytes=vmem_limit)


def _rms(x, w):
    ms = jnp.mean(x * x, axis=-1, keepdims=True)
    return x * lax.rsqrt(ms + EPS) * w


def _silu(x):
    return x * (1.0 / (1.0 + jnp.exp(-x)))


def _column_tile(row):
    return jnp.broadcast_to(row, (LANES, LANES)).T


def _rms_matmul_kernel(x_ref, g_ref, w_ref, o_ref, h_ref):
    first = pl.program_id(1) == 0
    half = x_ref.shape[0] // 2

    @pl.when(first)
    def _():
        for rows in (slice(0, half), slice(half, 2 * half)):
            h_ref[rows, :] = _rms(x_ref[rows, :], g_ref[...]).astype(BF16)
            o_ref[rows, :] = jnp.dot(h_ref[rows, :], w_ref[...],
                                     preferred_element_type=F32).astype(o_ref.dtype)

    @pl.when(jnp.logical_not(first))
    def _():
        o_ref[...] = jnp.dot(h_ref[...], w_ref[...], preferred_element_type=F32).astype(o_ref.dtype)


def rms_matmul(x, g, w, *, tm, tn):
    m, k = x.shape
    n = w.shape[1]
    return pl.pallas_call(
        _rms_matmul_kernel,
        grid=(m // tm, n // tn),
        in_specs=[
            pl.BlockSpec((tm, k), lambda i, j: (i, 0)),
            pl.BlockSpec((1, k), lambda i, j: (0, 0)),
            pl.BlockSpec((k, tn), lambda i, j: (0, j)),
        ],
        out_specs=pl.BlockSpec((tm, tn), lambda i, j: (i, j)),
        out_shape=jax.ShapeDtypeStruct((m, n), BF16),
        scratch_shapes=[pltpu.VMEM((tm, k), BF16)],
        compiler_params=_params("parallel", "arbitrary", vmem_limit=VMEM_LIMIT_BYTES_IN_PROJ),
        name="rms_matmul",
    )(x, g, w)


def _hgrn2_block(q, zf, v, lb, st):
    c, s = A_CHUNK, A_SUB
    m = c // s
    tb = q.shape[0]
    nch = tb // c
    tril_mask = (lax.broadcasted_iota(jnp.int32, (c, c), 1)
                 <= lax.broadcasted_iota(jnp.int32, (c, c), 0))
    tril = jnp.where(tril_mask, 1.0, 0.0).astype(BF16)

    t = jnp.exp(-jnp.abs(zf))
    u = (1.0 - lb) * (1.0 / (1.0 + t))
    w = t * u
    pos = zf >= 0
    logf = jnp.log2(lb + jnp.where(pos, u, w))
    kk = jnp.where(pos, w, u)

    hi = logf.astype(BF16)
    r1 = logf - hi.astype(F32)
    mid = r1.astype(BF16)
    lo = (r1 - mid.astype(F32)).astype(BF16)
    terms = jnp.concatenate([hi, mid, lo], axis=1)
    b_chunks = []
    for n in range(nch):
        y = jnp.dot(tril, terms[n * c:(n + 1) * c], preferred_element_type=F32)
        b_chunks.append((y[:, :A_DK] + y[:, A_DK:2 * A_DK]) + y[:, 2 * A_DK:])
    b = jnp.concatenate(b_chunks, axis=0)

    shape4 = (nch, m, s, A_DK)
    b4 = b.reshape(shape4)
    g4 = (b4 - logf.reshape(shape4))[:, :, 0:1, :]
    e4 = b4[:, :, s - 1:s, :]
    bl4 = e4[:, m - 1:m]
    lc4 = b4 - g4
    kk4 = kk.reshape(shape4)
    q_sub = (q.reshape(shape4) * jnp.exp2(lc4)).astype(BF16)
    k_diag = (kk4 * jnp.exp2(-lc4)).astype(BF16)
    k_end = (kk4 * jnp.exp2(e4 - b4)).astype(BF16)
    q_in = q_sub * jnp.exp2(g4).astype(BF16)
    k_in = k_end * jnp.exp2(bl4 - e4).astype(BF16)
    decay = jnp.exp2(bl4)
    k_cross = {(i, j): k_end[:, j] * jnp.exp2(g4[:, i] - e4[:, j]).astype(BF16)
               for i in range(m) for j in range(i - 1)}
    vb = v.astype(BF16)
    zeros = jnp.zeros((s, A_DK), BF16)

    o_intra, kv = [], []
    for n in range(nch):
        score_rows = []
        for i in range(m):
            parts = [k_end[n, j] if j == i - 1 else k_cross[(i, j)][n] for j in range(i)]
            parts.append(k_diag[n, i])
            parts.extend([zeros] * (m - 1 - i))
            k_i = jnp.concatenate(parts, axis=0)
            score_rows.append(lax.dot_general(q_sub[n, i], k_i, NT_DIMS,
                                              preferred_element_type=F32))
        scores = jnp.concatenate(score_rows, axis=0)
        scores = jnp.where(tril_mask, scores, 0.0).astype(BF16)
        vb_n = vb[n * c:(n + 1) * c]
        o_intra.append(jnp.dot(scores, vb_n, preferred_element_type=F32))
        kv.append(lax.dot_general(vb_n, k_in[n].reshape(c, A_DK), TN_DIMS,
                                  preferred_element_type=F32))

    states = []
    for n in range(nch):
        states.append(st.astype(BF16))
        st = st * decay[n, 0] + kv[n]
    outs = [o_intra[n] + lax.dot_general(q_in[n].reshape(c, A_DK), states[n], NT_DIMS,
                                         preferred_element_type=F32) for n in range(nch)]
    return jnp.concatenate(outs, axis=0), st


def _even_mixer_kernel(q_ref, f_ref, i_ref, ga_ref, u_ref, v_ref, gb_ref,
                       lbl_ref, onorm_ref, lnw_ref, lnb_ref, ws_ref, bias_ref,
                       oa_ref, ob_ref, st_ref, *, layer_j):
    tb = q_ref.shape[0]

    @pl.when(pl.program_id(2) == 0)
    def _():
        st_ref[...] = jnp.zeros_like(st_ref)

    logits = lbl_ref[...]
    e = jnp.exp(logits - jnp.max(logits, axis=0, keepdims=True))
    lb = jnp.sum(e[:layer_j + 1], axis=0, keepdims=True) / jnp.sum(e, axis=0, keepdims=True)

    o, st = _hgrn2_block(q_ref[...].astype(F32), f_ref[...].astype(F32), i_ref[...].astype(F32),
                         lb, st_ref[...])
    st_ref[...] = st
    oa_ref[...] = (_rms(o, onorm_ref[...]) * _silu(ga_ref[...].astype(F32))).astype(oa_ref.dtype)

    cb = B_CHUNK
    w_causal = jnp.where(lax.broadcasted_iota(jnp.int32, (cb, cb), 1)
                         <= lax.broadcasted_iota(jnp.int32, (cb, cb), 0),
                         ws_ref[...], 0.0).astype(BF16)
    vv = v_ref[...].astype(F32)
    mu = jnp.mean(vv, axis=-1, keepdims=True)
    xc = vv - mu
    vg = (xc * lax.rsqrt(jnp.mean(xc * xc, axis=-1, keepdims=True) + EPS) * lnw_ref[...]
          + lnb_ref[...]).astype(BF16)
    sv = jnp.concatenate([jnp.dot(w_causal, vg[n * cb:(n + 1) * cb], preferred_element_type=F32)
                          for n in range(tb // cb)], axis=0)
    bias = _column_tile(bias_ref[...])
    sv = (sv.reshape(tb // cb, cb, B_DIM) + bias).reshape(tb, B_DIM)
    ob_ref[...] = (u_ref[...].astype(F32) * sv * _silu(gb_ref[...].astype(F32))).astype(ob_ref.dtype)


def even_mixers(z, lb_logits, a_onorm, ln_w, ln_b, ws, bias, *, layer_j, tb):
    bsz, t, _ = z.shape
    h = A_HEADS
    n_rows = lb_logits.shape[0]
    lbl = lb_logits.reshape(n_rows, h, A_DK).transpose(1, 0, 2)

    def sec(k):
        return pl.BlockSpec((None, tb, LANES), lambda b, j, s, k=k: (b, s, k * h + j))

    per_head = lambda shape: pl.BlockSpec((None,) + shape, lambda b, j, s: (j, 0, 0))
    out_spec = pl.BlockSpec((None, tb, LANES), lambda b, j, s: (b, s, j))
    out_sds = jax.ShapeDtypeStruct((bsz, t, h * LANES), BF16)
    return pl.pallas_call(
        functools.partial(_even_mixer_kernel, layer_j=layer_j),
        grid=(bsz, h, t // tb),
        in_specs=[sec(k) for k in range(7)] + [
            per_head((n_rows, LANES)),
            pl.BlockSpec((1, LANES), lambda b, j, s: (0, 0)),
            per_head((1, LANES)),
            per_head((1, LANES)),
            per_head((B_CHUNK, B_CHUNK)),
            per_head((1, B_CHUNK)),
        ],
        out_specs=[out_spec, out_spec],
        out_shape=[out_sds, out_sds],
        scratch_shapes=[pltpu.VMEM((A_DK, A_DK), F32)],
        compiler_params=_params("parallel", "parallel", "arbitrary"),
        name="even_mixers",
    )(z, z, z, z, z, z, z, lbl, a_onorm.reshape(1, LANES),
      ln_w.reshape(B_GROUPS, 1, B_DIM), ln_b.reshape(B_GROUPS, 1, B_DIM),
      ws, bias.reshape(B_GROUPS, 1, B_CHUNK))


def _out_proj_kernel(*refs, n_in, with_next):
    a_refs = refs[:n_in]
    w_refs = refs[n_in:2 * n_in]
    x_ref, post_ref = refs[2 * n_in:2 * n_in + 2]
    rest = refs[2 * n_in + 2:]
    if with_next:
        pre_ref, xo_ref, ho_ref = rest
    else:
        (xo_ref,) = rest
    tm = x_ref.shape[0]
    n_sub = 2 if tm % 32 == 0 else 1
    sub = tm // n_sub

    def project(k):
        rows = slice(k * sub, (k + 1) * sub)
        y = jnp.dot(a_refs[0][rows, :], w_refs[0][...], preferred_element_type=F32)
        for a_ref, w_ref in zip(a_refs[1:], w_refs[1:]):
            y = y + jnp.dot(a_ref[rows, :], w_ref[...], preferred_element_type=F32)
        return y

    def finish(k, y):
        rows = slice(k * sub, (k + 1) * sub)
        x_new = x_ref[rows, :] + _rms(y, post_ref[...])
        xo_ref[rows, :] = x_new
        if with_next:
            ho_ref[rows, :] = _rms(x_new, pre_ref[...]).astype(BF16)

    ys = [project(0)]
    for k in range(n_sub):
        if k + 1 < n_sub:
            ys.append(project(k + 1))
        finish(k, ys[k])


def out_proj(acts, w, x, post_w, next_pre_w, *, tm):
    m, d = x.shape
    n_in = len(acts)
    ka = acts[0].shape[1]
    assert all(a.shape[1] == ka for a in acts) and w.shape == (n_in * ka, d)
    with_next = next_pre_w is not None
    row = lambda width: pl.BlockSpec((tm, width), lambda i: (i, 0))
    const = lambda shape: pl.BlockSpec(shape, lambda i: (0, 0))
    in_specs = ([row(ka) for _ in acts]
                + [pl.BlockSpec((ka, d), lambda i, r=r: (r, 0)) for r in range(n_in)]
                + [row(d), const((1, d))])
    args = list(acts) + [w] * n_in + [x, post_w.reshape(1, d)]
    out_specs = [row(d)]
    out_shape = [jax.ShapeDtypeStruct((m, d), F32)]
    if with_next:
        in_specs.append(const((1, d)))
        args.append(next_pre_w.reshape(1, d))
        out_specs.append(row(d))
        out_shape.append(jax.ShapeDtypeStruct((m, d), BF16))
    return pl.pallas_call(
        functools.partial(_out_proj_kernel, n_in=n_in, with_next=with_next),
        grid=(m // tm,),
        in_specs=in_specs,
        out_specs=out_specs,
        out_shape=out_shape,
        compiler_params=_params("parallel"),
        name="out_proj",
    )(*args)


def _lane_iota():
    return lax.broadcasted_iota(jnp.int32, (1, LANES), 1)


def _join_halves(left, right):
    half = LANES // 2
    return pltpu.roll(jnp.where(_lane_iota() < half, right, left), half, axis=1)


def _odd_in_proj_kernel(h_ref, wt_ref, c_ref, pe_ref, gate_ref):
    h = h_ref[...]
    n_c = c_ref.shape[1]
    n_pe = n_c + C_ROPE
    nt = lambda rows: lax.dot_general(h, wt_ref[rows, :], NT_DIMS, preferred_element_type=F32)
    c_ref[...] = nt(slice(0, n_c)).astype(c_ref.dtype)
    pe_ref[:, :C_ROPE] = nt(slice(n_c, n_pe)).astype(pe_ref.dtype)
    pe_ref[:, C_ROPE:] = jnp.zeros((pe_ref.shape[0], LANES - C_ROPE), pe_ref.dtype)
    gate_ref[...] = nt(slice(n_pe, wt_ref.shape[0])).astype(gate_ref.dtype)


def odd_in_proj(h, wt, *, tm):
    m, k = h.shape
    widths = [2 * C_RANK, LANES, C_HEADS * C_V]
    w = wt
    return pl.pallas_call(
        _odd_in_proj_kernel,
        grid=(m // tm,),
        in_specs=[pl.BlockSpec((tm, k), lambda i: (i, 0)),
                  pl.BlockSpec(w.shape, lambda i: (0, 0))],
        out_specs=[pl.BlockSpec((tm, wd), lambda i: (i, 0)) for wd in widths],
        out_shape=[jax.ShapeDtypeStruct((m, wd), BF16) for wd in widths],
        compiler_params=_params("parallel"),
        name="odd_in_proj",
    )(h, w)


def _mla_expand_kernel(c_ref, pe_ref, pos_ref, freq_ref, qnorm_ref, kvnorm_ref, wq_ref, wkv_ref,
                       qn_ref, qpe_ref, kn_ref, vt_ref, kpe_ref, wk_ref, wvt_ref, *, scale):
    @pl.when(pl.program_id(0) == 0)
    def _():
        for h in range(C_HEADS):
            lo = h * (C_NOPE + C_V)
            wk_ref[:, h * C_NOPE:(h + 1) * C_NOPE] = wkv_ref[:, lo:lo + C_NOPE].astype(wk_ref.dtype)
            wvt_ref[h * C_V:(h + 1) * C_V, :] = wkv_ref[:, lo + C_NOPE:lo + C_NOPE + C_V].T.astype(
                wvt_ref.dtype)

    lane = _lane_iota()
    first_half = lane < C_ROPE
    low_rot = (lane % C_ROPE) < C_ROPE // 2
    tm = c_ref.shape[0]
    n_sub = 2 if tm % (2 * LANES) == 0 else 1
    sub = tm // n_sub

    def project(k):
        rows = slice(k * sub, (k + 1) * sub)
        c = c_ref[rows, :].astype(F32)
        cq = _rms(c[:, :C_RANK], qnorm_ref[...]).astype(BF16)
        ckv = _rms(c[:, C_RANK:], kvnorm_ref[...]).astype(BF16)
        q = jnp.dot(cq, wq_ref[...], preferred_element_type=F32)
        kn_ref[rows, :] = jnp.dot(ckv, wk_ref[...], preferred_element_type=F32).astype(kn_ref.dtype)
        vt_ref[:, rows] = lax.dot_general(wvt_ref[...], ckv, NT_DIMS,
                                          preferred_element_type=F32).astype(vt_ref.dtype)
        return q

    def finish(k, q):
        rows = slice(k * sub, (k + 1) * sub)
        groups = slice(k * sub // LANES, (k + 1) * sub // LANES)
        pos = pos_ref[groups, :].astype(F32)
        pos = jnp.concatenate([_column_tile(pos[g:g + 1]) for g in range(pos.shape[0])], axis=0)
        ang = pos * freq_ref[...]
        cos = jnp.cos(ang)
        sin = jnp.sin(ang)

        def rope(x):
            rot = jnp.where(low_rot, -pltpu.roll(x, LANES - C_ROPE // 2, axis=1),
                            pltpu.roll(x, C_ROPE // 2, axis=1))
            return x * cos + rot * sin

        for p in range(C_HEADS // 2):
            a, b, c3 = (q[:, (3 * p + j) * LANES:(3 * p + j + 1) * LANES] for j in range(3))
            qn_ref[rows, 2 * p * LANES:(2 * p + 1) * LANES] = (a * scale).astype(qn_ref.dtype)
            qn_ref[rows, (2 * p + 1) * LANES:(2 * p + 2) * LANES] = (
                _join_halves(b, c3) * scale).astype(qn_ref.dtype)
            pe_pair = jnp.where(first_half, b, c3)
            qpe_ref[rows, p * LANES:(p + 1) * LANES] = (rope(pe_pair) * scale).astype(qpe_ref.dtype)

        k_pe = rope(pe_ref[rows, :].astype(F32))
        kpe_ref[rows, :] = jnp.where(first_half, k_pe,
                                     pltpu.roll(k_pe, C_ROPE, axis=1)).astype(kpe_ref.dtype)

    qs = [project(0)]
    for k in range(n_sub):
        if k + 1 < n_sub:
            qs.append(project(k + 1))
        finish(k, qs[k])


def mla_expand(c, pe, pos, freq, q_norm, kv_norm, wq, wkv, *, bsz, tm, scale):
    m = c.shape[0]
    t = m // bsz
    nb = t // tm
    row = lambda width: pl.BlockSpec((tm, width), lambda i: (i, 0))
    const = lambda shape: pl.BlockSpec(shape, lambda i: (0, 0))
    n_nope = C_HEADS * C_NOPE
    n_v = C_HEADS * C_V
    widths = [n_nope, C_HEADS * C_ROPE, n_nope, None, LANES]
    out_specs = [row(wd) if wd else pl.BlockSpec((None, n_v, tm), lambda i: (i // nb, 0, i % nb))
                 for wd in widths]
    out_shape = [jax.ShapeDtypeStruct((m, wd) if wd else (bsz, n_v, t), BF16) for wd in widths]
    return pl.pallas_call(
        functools.partial(_mla_expand_kernel, scale=scale),
        grid=(m // tm,),
        in_specs=[row(c.shape[1]), row(pe.shape[1]),
                  pl.BlockSpec((None, tm // LANES, LANES), lambda i: (i, 0, 0)), const((1, LANES)),
                  const((1, C_RANK)), const((1, C_RANK)), const(wq.shape), const(wkv.shape)],
        out_specs=out_specs,
        out_shape=out_shape,
        scratch_shapes=[pltpu.VMEM((C_RANK, n_nope), BF16), pltpu.VMEM((n_v, C_RANK), BF16)],
        compiler_params=_params("arbitrary"),
        name="mla_expand",
    )(c, pe, pos, freq, q_norm.reshape(1, C_RANK), kv_norm.reshape(1, C_RANK), wq, wkv)


def _attn_kernel(qn_ref, qpe_ref, kn_ref, kpe_ref, vt_ref, g_ref, o_ref, k_scr, v_scr, *, tq):
    t = qn_ref.shape[0]
    heads = tuple(range(qn_ref.shape[1] // LANES))
    for hh in heads:
        k_scr[hh, :, :LANES] = kn_ref[:, hh * LANES:(hh + 1) * LANES]
        k_scr[hh, :, LANES:] = kpe_ref[...]
        v_scr[hh, :C_V] = vt_ref[hh * C_V:(hh + 1) * C_V]
        v_scr[hh, C_V:] = jnp.ones((v_scr.shape[1] - C_V, t), v_scr.dtype)

    lane = lax.broadcasted_iota(jnp.int32, (1, LANES), 1)
    causal = (lax.broadcasted_iota(jnp.int32, (tq, tq), 0)
              <= lax.broadcasted_iota(jnp.int32, (tq, tq), 1))
    neg = jnp.finfo(BF16).min

    def scores(qi, hh):
        r0 = qi * tq
        end = r0 + tq
        q_pe = qpe_ref[r0:end, (hh // 2) * LANES:(hh // 2 + 1) * LANES]
        q_pe = jnp.where((lane // C_ROPE) == hh % 2, q_pe, jnp.zeros_like(q_pe))
        q = jnp.concatenate([qn_ref[r0:end, hh * LANES:(hh + 1) * LANES], q_pe], axis=1)
        return lax.dot_general(k_scr[hh, :end, :], q, NT_DIMS,
                               preferred_element_type=F32).astype(BF16)

    def col_max(qi, s):
        r0 = qi * tq
        m = jnp.max(jnp.where(causal, s[r0:], neg), axis=0, keepdims=True)
        if qi > 0:
            m = jnp.maximum(m, jnp.max(s[:r0], axis=0, keepdims=True))
        return m

    def probs(qi, s, m):
        r0 = qi * tq
        p_d = jnp.where(causal, jnp.exp2(s[r0:] - m), jnp.zeros((), BF16))
        p_o = jnp.exp2(s[:r0] - m) if qi > 0 else None
        return p_d, p_o

    def values(qi, hh, p_d, p_o):
        r0 = qi * tq
        end = r0 + tq
        vt = v_scr.at[hh]
        acc = jnp.dot(vt[:, r0:end], p_d, preferred_element_type=F32)
        if qi > 0:
            acc = acc + jnp.dot(vt[:, :r0], p_o, preferred_element_type=F32)
        o = (acc[:C_V] * (1.0 / acc[C_V:C_V + 1])).T
        cols = slice(hh * C_V, (hh + 1) * C_V)
        o_ref[r0:end, cols] = (o * _silu(g_ref[r0:end, cols].astype(F32))).astype(o_ref.dtype)

    nq = t // tq
    s = [scores(0, hh) for hh in heads]
    sm = [col_max(0, s[hh]) for hh in heads]
    for qi in range(nq):
        last = qi + 1 == nq
        for hh in heads:
            s_next = None if last else scores(qi + 1, hh)
            values(qi, hh, *probs(qi, s[hh], sm[hh]))
            if not last:
                s[hh] = s_next
                sm[hh] = col_max(qi + 1, s_next)


def attention(qn, qpe, kn, kpe, vt, gate, *, tq):
    bsz, t, _ = qn.shape
    g = ATTN_HEADS_PER_STEP
    group = lambda width: pl.BlockSpec((None, t, width), lambda b, p: (b, 0, p))
    shared = pl.BlockSpec((None, t, LANES), lambda b, p: (b, 0, 0))
    group_t = pl.BlockSpec((None, g * C_V, t), lambda b, p: (b, p, 0))
    return pl.pallas_call(
        functools.partial(_attn_kernel, tq=tq),
        grid=(bsz, C_HEADS // g),
        in_specs=[group(g * LANES), group(g * C_ROPE), group(g * LANES), shared, group_t,
                  group(g * C_V)],
        out_specs=group(g * C_V),
        out_shape=jax.ShapeDtypeStruct((bsz, t, C_HEADS * C_V), BF16),
        scratch_shapes=[pltpu.VMEM((g, t, 2 * LANES), BF16),
                        pltpu.VMEM((g, C_V + BF16_SUBLANES, t), BF16)],
        compiler_params=_params("parallel", "parallel"),
        name="mla_attention",
    )(qn, qpe, kn, kpe, vt, gate)


def kernel(x, positions, norm_pre, norm_post, ev_w_in, ev_lb_logits, ev_a_onorm, ev_b_ln_w,
           ev_b_ln_b, ev_b_ws, ev_b_bias, ev_w_out, od_w_in, od_q_norm, od_w_qb, od_kv_norm,
           od_w_kvb, od_w_out):
    bsz, t, d = x.shape
    m = bsz * t
    xf = x.reshape(m, d)
    tm = min(ROW_TILE, m)

    z = rms_matmul(xf, norm_pre[0].reshape(1, d), ev_w_in[0], tm=min(IN_PROJ_TM, m),
                   tn=IN_PROJ_TN)
    mix_a, mix_b = even_mixers(z.reshape(bsz, t, -1), ev_lb_logits, ev_a_onorm[0], ev_b_ln_w[0],
                               ev_b_ln_b[0], ev_b_ws[0], ev_b_bias[0], layer_j=0,
                               tb=min(MIXER_BLOCK, t))
    x1, h1 = out_proj([mix_a.reshape(m, -1), mix_b.reshape(m, -1)], ev_w_out[0],
                      xf, norm_post[0], norm_pre[1], tm=tm)

    c, pe, gate = odd_in_proj(h1, od_w_in[0].T, tm=tm)
    inv_freq = ROPE_THETA ** (-jnp.arange(0, C_ROPE, 2, dtype=F32) / C_ROPE)
    freq = jnp.tile(inv_freq, LANES // (C_ROPE // 2)).reshape(1, LANES)
    scale = float((C_NOPE + C_ROPE) ** -0.5 * LOG2E)
    tm_e = min(tm, t)
    pos = positions.reshape(m // tm_e, tm_e // LANES, LANES)
    qn, qpe, kn, vt, kpe = mla_expand(c, pe, pos, freq, od_q_norm[0], od_kv_norm[0], od_w_qb[0],
                                      od_w_kvb[0], bsz=bsz, tm=tm_e, scale=scale)
    r3 = lambda a: a.reshape(bsz, t, -1)
    o = attention(r3(qn), r3(qpe), r3(kn), r3(kpe), vt, r3(gate), tq=min(ATTN_TQ, t))
    (x2,) = out_proj([o.reshape(m, -1)], od_w_out[0], x1, norm_post[1], None, tm=tm)
    return x2.reshape(bsz, t, d)
```
